```python
import math
import jax
import jax.numpy as jnp
from jax import lax
import numpy as np


D_MODEL = 1024
BATCH = 16
SEQ = 4096
DEPTH = 2

CTX_LEN = 256
GRID_W = 64
F32 = jnp.float32
BRANCH_WIDTH = 512
N_BRANCH = 3
M_HEADS = 4
M_HEAD_DIM = BRANCH_WIDTH // M_HEADS
M_WIDTH = M_HEADS * M_HEAD_DIM
M_GATES = 4 * M_HEADS
M_CHUNK = 64
M_CONV = 3
W_HEADS = 8
W_KV_HEADS = 2
W_HEAD_DIM = BRANCH_WIDTH // W_HEADS
W_WIDTH = W_HEADS * W_HEAD_DIM
W_KV_WIDTH = W_KV_HEADS * W_HEAD_DIM
WINDOW = 128
W_BLOCK = 128
DF_HEADS = 4
DF_V_DIM = BRANCH_WIDTH // DF_HEADS
DF_QK_DIM = DF_V_DIM // 2
DF_WIDTH = DF_HEADS * DF_V_DIM
DF_QK_WIDTH = DF_HEADS * 2 * DF_QK_DIM
DF_BLOCK = 128
ROPE_DIM = 64
ROPE_BASE = 10000.0
FFN_HIDDEN = -(-8 * D_MODEL // (3 * 256)) * 256
EPS = 1e-6
IN_SIZES = (M_WIDTH, M_WIDTH, M_WIDTH, M_WIDTH, M_GATES,
            W_WIDTH, W_KV_WIDTH, W_KV_WIDTH,
            DF_QK_WIDTH, DF_QK_WIDTH, DF_WIDTH,
            N_BRANCH * D_MODEL)
IN_WIDTH = sum(IN_SIZES)
IN_OFFSETS = tuple(int(o) for o in np.cumsum(IN_SIZES)[:-1])

kernel_name = 'hybrid_mlstm_swa_diffattn_prefix_block'


def rms_norm(x, g):
    xf = x.astype(F32)
    y = xf * lax.rsqrt(jnp.mean(xf * xf, axis=-1, keepdims=True) + EPS)
    return (y * g.astype(F32)).astype(x.dtype)


def head_rms_norm(y, g):
    H, dh = y.shape[-2:]
    return rms_norm(y, g.reshape(H, dh)).reshape(y.shape[:-2] + (H * dh,))


def modulate(x, g, shift, scale):
    return rms_norm(x, g) * (1 + scale) + shift


def adaln(cond, w_mod, b_mod):
    return jnp.split(jax.nn.silu(cond) @ w_mod + b_mod, 6, axis=-1)


def split_heads(a, h):
    return a.reshape(a.shape[:-1] + (h, a.shape[-1] // h))


def axial_rope_tables(n_tokens):
    rows = n_tokens // GRID_W
    r, col = jnp.meshgrid(jnp.arange(rows), jnp.arange(GRID_W), indexing='ij')
    half = ROPE_DIM // 2
    inv = ROPE_BASE ** (-jnp.arange(0, half, 2, dtype=F32) / half)
    ang_r = r.reshape(-1, 1).astype(F32) * inv
    ang_c = col.reshape(-1, 1).astype(F32) * inv
    ang = jnp.concatenate([ang_r, ang_r, ang_c, ang_c], axis=-1)
    return jnp.cos(ang), jnp.sin(ang)


def apply_rope(x, cos, sin):
    xf = x.astype(F32)
    a = xf.reshape(xf.shape[:-1] + (2, 2, ROPE_DIM // 4))
    rot = jnp.concatenate([-a[..., 1:, :], a[..., :1, :]], axis=-2).reshape(xf.shape)
    return (xf * cos[:, None, :] + rot * sin[:, None, :]).astype(x.dtype)


def short_conv(u, w):
    K = w.shape[0]
    p = K // 2
    T = u.shape[1]
    up = jnp.pad(u, ((0, 0), (p, p), (0, 0)))
    return sum(up[:, j:j + T] * w[j] for j in range(K))


def mlstm_inputs(mq, mk, mv, mg, b_gate, conv_w):
    B, T, _ = mq.shape
    qk = jax.nn.silu(short_conv(jnp.concatenate([mq, mk], axis=-1), conv_w))
    heads = lambda a: jnp.transpose(a.astype(F32).reshape(B, T, M_HEADS, M_HEAD_DIM), (0, 2, 1, 3))
    q = heads(qk[..., :M_WIDTH])
    k = heads(qk[..., M_WIDTH:]) * M_HEAD_DIM ** -0.5
    v = heads(mv)
    g = jnp.transpose((mg + b_gate).astype(F32).reshape(B, T, 4, M_HEADS), (2, 0, 3, 1))
    return q, k, v, g


def mlstm_zero_state(b):
    return (jnp.zeros((b, M_HEADS, M_HEAD_DIM, M_HEAD_DIM), F32),
            jnp.zeros((b, M_HEADS, M_HEAD_DIM), F32),
            jnp.zeros((b, M_HEADS), F32))


def mlstm_scan(q, k, v, i_pre, log_f, state, return_h):
    B, H, T, d = q.shape
    nc = T // M_CHUNK
    chunks = lambda a: jnp.moveaxis(a.reshape((B, H, nc, M_CHUNK) + a.shape[3:]), 2, 0)
    causal = jnp.tril(jnp.ones((M_CHUNK, M_CHUNK), dtype=bool))

    def step(carry, inp):
        C, n, m = carry
        qc, kc, vc, ic, fc = inp
        b = jnp.cumsum(fc, axis=-1)
        b_end = b[..., -1]
        log_end = b_end[..., None] - b + ic
        m_new = jnp.maximum(b_end + m, log_end.max(-1))
        w_end = jnp.exp(log_end - m_new[..., None])
        decay = jnp.exp(b_end + m - m_new)
        C_new = decay[..., None, None] * C + jnp.einsum('bhs,bhsv,bhsk->bhvk', w_end, vc, kc)
        n_new = decay[..., None] * n + jnp.einsum('bhs,bhsk->bhk', w_end, kc)
        if not return_h:
            return (C_new, n_new, m_new), None
        log_w = jnp.where(causal, b[..., :, None] - b[..., None, :] + ic[..., None, :], -jnp.inf)
        log_inter = b + m[..., None]
        m_row = jnp.maximum(log_inter, log_w.max(-1))
        w_intra = jnp.exp(log_w - m_row[..., None])
        w_inter = jnp.exp(log_inter - m_row)
        s = jnp.einsum('bhtk,bhsk->bhts', qc, kc) * w_intra
        num = jnp.einsum('bhts,bhsv->bhtv', s, vc) + w_inter[..., None] * jnp.einsum('bhvk,bhtk->bhtv', C, qc)
        den = s.sum(-1) + w_inter * jnp.einsum('bhk,bhtk->bht', n, qc)
        h = num / jnp.maximum(jnp.abs(den), jnp.exp(-m_row))[..., None]
        return (C_new, n_new, m_new), h

    carry, hs = lax.scan(step, state, tuple(chunks(a) for a in (q, k, v, i_pre, log_f)))
    h = jnp.moveaxis(hs, 0, 2).reshape(B, H, T, d) if return_h else None
    return h, carry


def mlstm_bidir(q, k, v, g, st_fwd, st_bwd, return_h):
    i_f, f_f, i_b, f_b = g
    h_f, st_f = mlstm_scan(q, k, v, i_f, jax.nn.log_sigmoid(f_f), st_fwd, return_h)
    rev = lambda a: jnp.flip(a, axis=2)
    h_b, st_b = mlstm_scan(rev(q), rev(k), rev(v), rev(i_b), rev(jax.nn.log_sigmoid(f_b)), st_bwd, return_h)
    h = h_f + rev(h_b) if return_h else None
    return h, st_f, st_b


def mlstm_out(h, o_pre, norm_w):
    B, H, T, d = h.shape
    y = jnp.transpose(h, (0, 2, 1, 3)) * jax.nn.sigmoid(o_pre.astype(F32)).reshape(B, T, H, d)
    return head_rms_norm(y, norm_w).astype(o_pre.dtype)


def _sink_attend(qn, key_sets, sink_l):
    logits = []
    for k, _, valid in key_sets:
        s = jnp.einsum('bqgrd,bkgd->bgrqk', qn, k)
        logits.append(s if valid is None else jnp.where(valid, s, -jnp.inf))
    m = sink_l
    for s in logits:
        m = jnp.maximum(m, s.max(-1))
    e = [jnp.exp(s - m[..., None]) for s in logits]
    den = jnp.exp(sink_l - m) + sum(ei.sum(-1) for ei in e)
    num = sum(jnp.einsum('bgrqk,bkgd->bgrqd', ei, ks[1]) for ei, ks in zip(e, key_sets))
    return jnp.transpose(num / den[..., None], (0, 3, 1, 2, 4))


def window_attention_latent(q, k, v, kc, vc, sink):
    B, T, H, d = q.shape
    G = k.shape[2]
    R = H // G
    nb = T // W_BLOCK
    qb = jnp.moveaxis(q.astype(F32).reshape(B, nb, W_BLOCK, G, R, d), 1, 0) * d ** -0.5
    pad = ((0, 0), (W_BLOCK, W_BLOCK), (0, 0), (0, 0))
    kp = jnp.pad(k.astype(F32), pad)
    vp = jnp.pad(v.astype(F32), pad)
    ctx_set = (kc.astype(F32), vc.astype(F32), None)
    sink_l = sink.astype(F32).reshape(G, R, 1)
    q_off = jnp.arange(W_BLOCK)
    k_off = jnp.arange(3 * W_BLOCK) - W_BLOCK

    def one_block(args):
        n, qn = args
        start = n * W_BLOCK
        kn = lax.dynamic_slice_in_dim(kp, start, 3 * W_BLOCK, axis=1)
        vn = lax.dynamic_slice_in_dim(vp, start, 3 * W_BLOCK, axis=1)
        s_pos = start + k_off
        valid = (jnp.abs(q_off[:, None] - k_off[None, :]) <= WINDOW) & ((s_pos >= 0) & (s_pos < T))[None, :]
        return _sink_attend(qn, ((kn, vn, valid), ctx_set), sink_l)

    out = lax.map(one_block, (jnp.arange(nb), qb))
    return jnp.moveaxis(out, 0, 1).reshape(B, T, H * d)


def window_attention_ctx(q, k, v, sink):
    B, Tc, H, d = q.shape
    G = k.shape[2]
    qn = q.astype(F32).reshape(B, Tc, G, H // G, d) * d ** -0.5
    o = _sink_attend(qn, ((k.astype(F32), v.astype(F32), None),), sink.astype(F32).reshape(G, H // G, 1))
    return o.reshape(B, Tc, H * d)


def _diff_attend(qn, k, v, lam):
    s = jnp.einsum('bqhcd,bkhcd->bchqk', qn, k)
    p = jax.nn.softmax(s, axis=-1)
    return jnp.einsum('bhqk,bkhv->bqhv', p[:, 0] - lam * p[:, 1], v)


def diff_attention_latent(q, k, v, kc, vc, lam):
    B, T, H, _, d = q.shape
    nb = T // DF_BLOCK
    k_all = jnp.concatenate([kc, k], axis=1).astype(F32)
    v_all = jnp.concatenate([vc, v], axis=1).astype(F32)
    qb = jnp.moveaxis(q.astype(F32).reshape(B, nb, DF_BLOCK, H, 2, d), 1, 0) * d ** -0.5
    out = lax.map(lambda qn: _diff_attend(qn, k_all, v_all, lam), qb)
    return jnp.moveaxis(out, 0, 1).reshape(B, T, H, -1)


def diff_out(y, d_norm, lam_init, dt):
    return (head_rms_norm(y, d_norm) * (1 - lam_init)).astype(dt)


def merge_branches(ys, gate_pre, w_branch, w_out):
    B, T, _ = gate_pre.shape
    gates = jax.nn.sigmoid(gate_pre).reshape(B, T, N_BRANCH, D_MODEL)
    merged = sum(gates[..., i, :] * (ys[i] @ w_branch[i]) for i in range(N_BRANCH))
    return merged @ w_out


def swiglu(h, w_in, w_out):
    gate, up = jnp.split(h @ w_in, 2, axis=-1)
    return (jax.nn.silu(gate) * up) @ w_out


def mixer_sublayer(h, hc, w_in, b_gate, conv_w, m_norm, sink, lam, lam_init, d_norm,
                   w_branch, w_out, cos, sin, need_ctx):
    dt = h.dtype
    B, T, _ = h.shape
    Tc = hc.shape[1]
    p = jnp.split(h @ w_in, IN_OFFSETS, axis=-1)
    pc = jnp.split(hc @ w_in, IN_OFFSETS, axis=-1)
    q, k, v, g = mlstm_inputs(p[0], p[1], p[2], p[4], b_gate, conv_w)
    qc, kc, vc, gc = mlstm_inputs(pc[0], pc[1], pc[2], pc[4], b_gate, conv_w)
    zero = mlstm_zero_state(hc.shape[0])
    hm_c, st_f, st_b = mlstm_bidir(qc, kc, vc, gc, zero, zero, need_ctx)
    hm, _, _ = mlstm_bidir(q, k, v, g, st_f, st_b, True)
    ya = mlstm_out(hm, p[3], m_norm)
    wq = apply_rope(split_heads(p[5], W_HEADS), cos, sin)
    wk = apply_rope(split_heads(p[6], W_KV_HEADS), cos, sin)
    wv = split_heads(p[7], W_KV_HEADS)
    wkc = split_heads(pc[6], W_KV_HEADS)
    wvc = split_heads(pc[7], W_KV_HEADS)
    yb = window_attention_latent(wq, wk, wv, wkc, wvc, sink).astype(dt)
    dq = apply_rope(p[8].reshape(B, T, DF_HEADS * 2, DF_QK_DIM), cos, sin).reshape(B, T, DF_HEADS, 2, DF_QK_DIM)
    dk = apply_rope(p[9].reshape(B, T, DF_HEADS * 2, DF_QK_DIM), cos, sin).reshape(B, T, DF_HEADS, 2, DF_QK_DIM)
    dv = split_heads(p[10], DF_HEADS)
    dkc = pc[9].reshape(B, Tc, DF_HEADS, 2, DF_QK_DIM)
    dvc = split_heads(pc[10], DF_HEADS)
    yc = diff_out(diff_attention_latent(dq, dk, dv, dkc, dvc, lam), d_norm, lam_init, dt)
    out = merge_branches((ya, yb, yc), p[11], w_branch, w_out)
    if not need_ctx:
        return out, None
    ya_c = mlstm_out(hm_c, pc[3], m_norm)
    yb_c = window_attention_ctx(split_heads(pc[5], W_HEADS), wkc, wvc, sink).astype(dt)
    dqc = pc[8].reshape(B, Tc, DF_HEADS, 2, DF_QK_DIM).astype(F32) * DF_QK_DIM ** -0.5
    yc_c = diff_out(_diff_attend(dqc, dkc.astype(F32), dvc.astype(F32), lam), d_norm, lam_init, dt)
    out_c = merge_branches((ya_c, yb_c, yc_c), pc[11], w_branch, w_out)
    return out, out_c


def setup_inputs(seed: int = 0) -> dict:
    key = jax.random.key(seed)
    ks = jax.random.split(key, 24)
    nrm = lambda k, shape, s: jax.random.normal(k, shape, F32) * s
    L = DEPTH
    D = D_MODEL
    fb = jnp.linspace(3.0, 6.0, M_HEADS)
    zb = jnp.zeros((M_HEADS,), F32)
    gate_base = jnp.concatenate([zb, fb, zb, fb])
    return {
        'x': nrm(ks[0], (BATCH, SEQ, D), 1.0),
        'c': nrm(ks[1], (BATCH, D), 1.0),
        'ctx': nrm(ks[2], (BATCH, CTX_LEN, D), 1.0),
        'c_ctx': nrm(ks[3], (D,), 1.0),
        'w_mod': nrm(ks[4], (L, D, 6 * D), 0.5 * D ** -0.5),
        'b_mod': nrm(ks[5], (L, 6 * D), 0.02),
        'g_mix': 1.0 + nrm(ks[6], (L, D), 0.02),
        'g_ffn': 1.0 + nrm(ks[7], (L, D), 0.02),
        'w_in': nrm(ks[8], (L, D, IN_WIDTH), D ** -0.5),
        'b_gate': gate_base + nrm(ks[9], (L, M_GATES), 0.1),
        'conv_w': nrm(ks[10], (L, M_CONV, 2 * M_WIDTH), M_CONV ** -0.5),
        'm_norm': 1.0 + nrm(ks[11], (L, M_WIDTH), 0.02),
        'sink': nrm(ks[12], (L, W_HEADS), 0.5),
        'lam_q1': nrm(ks[13], (L, DF_QK_DIM), 0.1),
        'lam_k1': nrm(ks[14], (L, DF_QK_DIM), 0.1),
        'lam_q2': nrm(ks[15], (L, DF_QK_DIM), 0.1),
        'lam_k2': nrm(ks[16], (L, DF_QK_DIM), 0.1),
        'd_norm': 1.0 + nrm(ks[17], (L, DF_WIDTH), 0.02),
        'w_branch': nrm(ks[18], (L, N_BRANCH, BRANCH_WIDTH, D), BRANCH_WIDTH ** -0.5),
        'w_out': nrm(ks[19], (L, D, D), D ** -0.5),
        'w_ffn_in': nrm(ks[20], (L, D, 2 * FFN_HIDDEN), D ** -0.5),
        'w_ffn_out': nrm(ks[21], (L, FFN_HIDDEN, D), FFN_HIDDEN ** -0.5),
        'g_final': 1.0 + nrm(ks[22], (D,), 0.02),
    }


def reference(x, c, ctx, c_ctx, w_mod, b_mod, g_mix, g_ffn, w_in, b_gate, conv_w, m_norm, sink,
              lam_q1, lam_k1, lam_q2, lam_k2, d_norm, w_branch, w_out, w_ffn_in, w_ffn_out, g_final):
    T = x.shape[1]
    cos, sin = axial_rope_tables(T)
    xc = ctx
    for l in range(DEPTH):
        need_ctx = l < DEPTH - 1
        sh1, sc1, gt1, sh2, sc2, gt2 = adaln(c[:, None, :], w_mod[l], b_mod[l])
        sh1c, sc1c, gt1c, sh2c, sc2c, gt2c = adaln(c_ctx, w_mod[l], b_mod[l])
        lam_init = 0.8 - 0.6 * math.exp(-0.3 * l)
        lam = (jnp.exp(jnp.sum(lam_q1[l].astype(F32) * lam_k1[l].astype(F32)))
               - jnp.exp(jnp.sum(lam_q2[l].astype(F32) * lam_k2[l].astype(F32))) + lam_init)
        out, out_c = mixer_sublayer(modulate(x, g_mix[l], sh1, sc1), modulate(xc, g_mix[l], sh1c, sc1c),
                                    w_in[l], b_gate[l], conv_w[l], m_norm[l], sink[l], lam, lam_init,
                                    d_norm[l], w_branch[l], w_out[l], cos, sin, need_ctx)
        x = x + gt1 * out
        x = x + gt2 * swiglu(modulate(x, g_ffn[l], sh2, sc2), w_ffn_in[l], w_ffn_out[l])
        if need_ctx:
            xc = xc + gt1c * out_c
            xc = xc + gt2c * swiglu(modulate(xc, g_ffn[l], sh2c, sc2c), w_ffn_in[l], w_ffn_out[l])
    return rms_norm(x, g_final)
```

```python
import functools
import math

import jax
import jax.numpy as jnp
import numpy as np
from jax import lax
from jax.experimental import pallas as pl
from jax.experimental.pallas import tpu as pltpu

F32 = jnp.float32
BF16 = jnp.bfloat16

GRID_W = 64
BRANCH = 512
N_BRANCH = 3
M_HEADS = 4
M_DIM = 128
W_HEADS = 8
W_KV_HEADS = 2
W_DIM = 64
DF_HEADS = 4
ROPE_DIM = 64
ROPE_BASE = 10000.0
EPS = 1e-6
LANES = 128
CHUNK = 128
VMEM_LIMIT = 56 * 1024 * 1024

_NT = (((1,), (1,)), ((), ()))


def _dot(a, b):
    return jnp.dot(a, b, preferred_element_type=F32)


def _dot_nt(a, b):
    return lax.dot_general(a, b, _NT, preferred_element_type=F32)


def _params(sem):
    return pltpu.CompilerParams(dimension_semantics=sem, vmem_limit_bytes=VMEM_LIMIT)


def _resident(shape, index_map):
    return pl.BlockSpec(shape, index_map, pipeline_mode=pl.Buffered(1))


def _adaln_kernel(c_ref, w_ref, b_ref, o_ref):
    c = c_ref[...]
    s = (c * jax.nn.sigmoid(c)).astype(BF16)
    o_ref[0] = _dot(s, w_ref[0].astype(BF16)) + b_ref[0]


def _adaln(cc, w_mod, b_mod):
    L, D, N = w_mod.shape
    R = cc.shape[0]
    tn = 1536
    return pl.pallas_call(
        _adaln_kernel,
        grid=(L, N // tn),
        in_specs=[pl.BlockSpec((R, D), lambda l, j: (0, 0)),
                  pl.BlockSpec((1, D, tn), lambda l, j: (l, 0, j)),
                  pl.BlockSpec((1, 1, tn), lambda l, j: (l, 0, j))],
        out_specs=pl.BlockSpec((1, R, tn), lambda l, j: (l, 0, j)),
        out_shape=jax.ShapeDtypeStruct((L, R, N), F32),
        compiler_params=_params(("arbitrary", "arbitrary")),
        name="adaln",
    )(cc, w_mod, b_mod.reshape(L, 1, N))


def _modnorm(x, g, shift, scale):
    ms = jnp.mean(x * x, axis=-1, keepdims=True)
    return x * lax.rsqrt(ms + EPS) * (g * (1.0 + scale)) + shift


_SEGS = (("mqk", 1024, BF16, "plain"), ("mv", 512, BF16, "plain"), ("mo", 512, BF16, "plain"),
         ("mg", 128, F32, "plain"), ("wq", 512, BF16, "ropeq"), ("wk", 256, BF16, "rope"),
         ("wv", 256, BF16, "plain"), ("dq", 512, BF16, "ropeq"), ("dk", 512, BF16, "rope"),
         ("dv", 512, BF16, "plain"), ("gate", 3072, BF16, "plain"))
_EXT_WIDTH = sum(s[1] for s in _SEGS)
_COL_CHUNK = 512


def _inproj_kernel(x_ref, mod_ref, g_ref, cos_ref, sa_ref, sb_ref, w_ref, *out_refs, d, use_rope):
    x = x_ref[...]
    mod = mod_ref[0]
    h = _modnorm(x, g_ref[...], mod[:, 0:d], mod[:, d:2 * d]).astype(BF16)
    if use_rope:
        cos, sa, sb = cos_ref[...], sa_ref[...], sb_ref[...]
    off = 0
    for (name, width, dt, kind), o_ref in zip(_SEGS, out_refs):
        for c0 in range(0, width, _COL_CHUNK):
            cw = min(_COL_CHUNK, width - c0)
            acc = _dot(h, w_ref[:, off + c0:off + c0 + cw])
            if kind != "plain":
                tiles = []
                for t in range(cw // LANES):
                    a = acc[:, t * LANES:(t + 1) * LANES]
                    if use_rope:
                        a = (a * cos + pltpu.roll(a, LANES - 16, 1) * sa + pltpu.roll(a, 16, 1) * sb)
                    if kind == "ropeq":
                        a = a * (W_DIM ** -0.5)
                    tiles.append(a)
                acc = jnp.concatenate(tiles, axis=1) if len(tiles) > 1 else tiles[0]
            o_ref[:, c0:c0 + cw] = acc.astype(dt)
        off += width


def _inproj(x2, mod, g, rope, w_ext, *, tm, rows_per_mod, seq, use_rope):
    N, D = x2.shape
    nt = seq // tm
    kern = functools.partial(_inproj_kernel, d=D, use_rope=use_rope)
    rope_spec = pl.BlockSpec((tm, LANES), lambda i: (i % nt, 0))
    out_shape = [jax.ShapeDtypeStruct((N, w), dt) for _, w, dt, _ in _SEGS]
    out_specs = [pl.BlockSpec((tm, w), lambda i: (i, 0)) for _, w, _, _ in _SEGS]
    return pl.pallas_call(
        kern,
        grid=(N // tm,),
        in_specs=[pl.BlockSpec((tm, D), lambda i: (i, 0)),
                  pl.BlockSpec((1, 1, 6 * D), lambda i: ((i * tm) // rows_per_mod, 0, 0)),
                  pl.BlockSpec((1, D), lambda i: (0, 0)),
                  rope_spec, rope_spec, rope_spec,
                  _resident((D, _EXT_WIDTH), lambda i: (0, 0))],
        out_specs=out_specs,
        out_shape=out_shape,
        compiler_params=_params(("arbitrary",)),
        name="inproj_lat" if use_rope else "inproj_ctx",
    )(x2, mod, g, *rope, w_ext)


def _extend_w_in(w):
    D = w.shape[0]
    o = np.cumsum([0, 512, 512, 512, 512, 16, 512, 128, 128, 512, 512, 512, 3072])
    p = [w[:, o[i]:o[i + 1]] for i in range(12)]
    dup = lambda a: jnp.concatenate([a[:, 0:64], a[:, 0:64], a[:, 64:128], a[:, 64:128]], axis=1)
    mg = jnp.concatenate([p[4], jnp.zeros((D, LANES - 16), w.dtype)], axis=1)
    ext = jnp.concatenate([p[0], p[1], p[2], p[3], mg, p[5], dup(p[6]), dup(p[7]), p[8], p[9], p[10], p[11]], axis=1)
    return ext.astype(BF16)


def _rope_tables(n_tokens):
    rows = n_tokens // GRID_W
    r, col = jnp.meshgrid(jnp.arange(rows), jnp.arange(GRID_W), indexing="ij")
    half = ROPE_DIM // 2
    inv = ROPE_BASE ** (-jnp.arange(0, half, 2, dtype=F32) / half)
    ang_r = r.reshape(-1, 1).astype(F32) * inv
    ang_c = col.reshape(-1, 1).astype(F32) * inv
    ang = jnp.concatenate([ang_r, ang_r, ang_c, ang_c], axis=-1)
    cos, sin = jnp.cos(ang), jnp.sin(ang)
    cos2 = jnp.concatenate([cos, cos], axis=-1)
    sin2 = jnp.concatenate([sin, sin], axis=-1)
    first = (jnp.arange(LANES) % 32) < 16
    sin_a = jnp.where(first, -sin2, 0.0)
    sin_b = jnp.where(first, 0.0, sin2)
    return cos2, sin_a, sin_b


def _scan_lanes(x, op, fill, reverse):
    lane = lax.broadcasted_iota(jnp.int32, x.shape, 1)
    d = 1
    while d < LANES:
        if reverse:
            shifted = jnp.where(lane < LANES - d, pltpu.roll(x, LANES - d, 1), fill)
        else:
            shifted = jnp.where(lane >= d, pltpu.roll(x, d, 1), fill)
        x = op(x, shifted)
        d *= 2
    return x


def _gates_kernel(x_ref, bias_ref, o_ref, *, rows):
    x = x_ref[0] + bias_ref[...]
    lane = lax.broadcasted_iota(jnp.int32, (rows, LANES), 1)
    for d, reverse in enumerate((False, True)):
        i_pre = x[(2 * d) * rows:(2 * d + 1) * rows]
        f_pre = x[(2 * d + 1) * rows:(2 * d + 2) * rows]
        log_f = jax.nn.log_sigmoid(f_pre)
        b = _scan_lanes(log_f, jnp.add, 0.0, reverse)
        a = i_pre - b
        pm = _scan_lanes(a, jnp.maximum, -jnp.inf, reverse)
        last = 0 if reverse else LANES - 1
        b_end = jnp.sum(jnp.where(lane == last, b, 0.0), axis=1, keepdims=True)
        a_max = jnp.max(a, axis=1, keepdims=True)
        o_ref[0, 5 * d + 0] = b
        o_ref[0, 5 * d + 1] = a
        o_ref[0, 5 * d + 2] = pm
        o_ref[0, 5 * d + 3] = jnp.broadcast_to(b_end, (rows, LANES))
        o_ref[0, 5 * d + 4] = jnp.broadcast_to(a_max, (rows, LANES))


def _gates(mg, b_gate, B, T):
    nc = T // CHUNK
    rows = M_HEADS * nc
    g = mg[:, :4 * M_HEADS].reshape(B, T, 4 * M_HEADS)
    gt = jnp.transpose(g, (0, 2, 1)).reshape(B, 4 * rows, LANES)
    bias = jnp.repeat(b_gate.astype(F32), nc).reshape(4 * rows, 1)
    return pl.pallas_call(
        functools.partial(_gates_kernel, rows=rows),
        grid=(B,),
        in_specs=[pl.BlockSpec((1, 4 * rows, LANES), lambda b: (b, 0, 0)),
                  pl.BlockSpec((4 * rows, 1), lambda b: (0, 0))],
        out_specs=pl.BlockSpec((1, 10, rows, LANES), lambda b: (b, 0, 0, 0)),
        out_shape=jax.ShapeDtypeStruct((B, 10, rows, LANES), F32),
        compiler_params=_params(("arbitrary",)),
        name="mlstm_gates",
    )(gt, bias)


_CONV_TILE = 256


def _mlstm_kernel(qc_ref, kc_ref, vc_ref, oc_ref, pc_ref, ql_ref, kl_ref, vl_ref, ol_ref, pl_ref,
                  cwq_ref, cwk_ref, nrm_ref, *rest, tc, tl, need_ctx):
    if need_ctx:
        yl_ref, yc_ref = rest[0], rest[1]
        scr = rest[2:]
    else:
        yl_ref, yc_ref = rest[0], None
        scr = rest[1:]
    xpad, qs, ks, hf, hb, cx, mst = scr
    head = pl.program_id(1)
    tot = tc + tl

    def conv_stream(u_ref, w_ref, dst, dst_off, ts, scale):
        xpad[0:8, :] = jnp.zeros((8, LANES), F32)
        xpad[8:8 + ts, :] = u_ref[0].astype(F32)
        xpad[8 + ts:16 + ts, :] = jnp.zeros((8, LANES), F32)
        w = w_ref[...]
        w0, w1, w2 = w[0:1], w[1:2], w[2:3]
        tile = min(_CONV_TILE, ts)

        def body(i, _):
            r0 = pl.multiple_of(i * tile, 8)
            win = xpad[pl.ds(r0, tile + 16), :]
            prev = pltpu.roll(win, 1, 0)[8:8 + tile]
            nxt = pltpu.roll(win, tile + 15, 0)[8:8 + tile]
            y = prev * w0 + win[8:8 + tile] * w1 + nxt * w2
            y = y * jax.nn.sigmoid(y) * scale
            dst[pl.ds(pl.multiple_of(dst_off + r0, 8), tile), :] = y.astype(BF16)
            return 0

        lax.fori_loop(0, ts // tile, body, 0)

    kscale = M_DIM ** -0.5
    conv_stream(qc_ref, cwq_ref, qs, 0, tc, 1.0)
    conv_stream(kc_ref, cwk_ref, ks, 0, tc, kscale)
    conv_stream(ql_ref, cwq_ref, qs, tc, tl, 1.0)
    conv_stream(kl_ref, cwk_ref, ks, tc, tl, kscale)

    cx[...] = jnp.zeros(cx.shape, F32)
    mst[...] = jnp.zeros(mst.shape, F32)

    row_i = lax.broadcasted_iota(jnp.int32, (CHUNK, CHUNK), 0)
    col_i = lax.broadcasted_iota(jnp.int32, (CHUNK, CHUNK), 1)
    eye = row_i == col_i
    masks = (col_i <= row_i, col_i >= row_i)
    e0_cols = jnp.where(col_i == 0, 1.0, 0.0).astype(BF16)
    e0_rows = jnp.where(row_i == 0, 1.0, 0.0).astype(F32)

    def colify(row):
        return jnp.sum(jnp.where(eye, row, 0.0), axis=1, keepdims=True)

    def chunk_step(d, c, p_ref, v_ref, nc, base, want_h):
        r = head * nc + c
        plane = lambda j: p_ref[0, 5 * d + j, pl.ds(r, 1), :]
        b_row, a_row, pm_row, bend_row, amax_row = (plane(j) for j in range(5))
        m_prev = mst[d:d + 1, :]
        m_end = jnp.maximum(m_prev, amax_row)
        w_end = jnp.exp(a_row - m_end)
        decay = jnp.exp(m_prev - m_end)
        row0 = pl.multiple_of(base + c * CHUNK, CHUNK)
        q = qs[pl.ds(row0, CHUNK), :]
        k = ks[pl.ds(row0, CHUNK), :]
        v = v_ref[0, pl.ds(pl.multiple_of(c * CHUNK, CHUNK), CHUNK), :]
        c_old = cx[d]
        if want_h:
            m_col = jnp.maximum(m_prev[:, 0:1], colify(pm_row))
            b_col = colify(b_row)
            w_intra = jnp.exp(jnp.where(masks[d], a_row - m_col, -jnp.inf))
            w_inter = jnp.exp(m_prev[:, 0:1] - m_col)
            floor = jnp.exp(-(b_col + m_col))
            s = _dot_nt(q, k) * w_intra
            vx = jnp.concatenate([v, e0_cols], axis=1)
            tot_ = _dot(s.astype(BF16), vx) + w_inter * _dot_nt(q, c_old.astype(BF16))
            num = tot_[:, 0:M_DIM]
            den = tot_[:, M_DIM:M_DIM + 1]
            hval = num / jnp.maximum(jnp.abs(den), floor)
            dst = hf if d == 0 else hb
            dst[pl.ds(row0, CHUNK), :] = hval
        vxt = jnp.concatenate([v.astype(F32).T, e0_rows], axis=0)
        cx[d] = decay * c_old + _dot((vxt * w_end).astype(BF16), k)
        mst[d:d + 1, :] = bend_row + m_end

    ncc, ncl = tc // CHUNK, tl // CHUNK
    for c in range(ncc):
        chunk_step(0, c, pc_ref, vc_ref, ncc, 0, need_ctx)
        chunk_step(1, ncc - 1 - c, pc_ref, vc_ref, ncc, 0, need_ctx)

    def lat_body(c, _):
        chunk_step(0, c, pl_ref, vl_ref, ncl, tc, True)
        chunk_step(1, ncl - 1 - c, pl_ref, vl_ref, ncl, tc, True)
        return 0

    lax.fori_loop(0, ncl, lat_body, 0)

    nrm = nrm_ref[...]

    def finish(o_ref, y_ref, base, ts):
        def body(i, _):
            r0 = pl.multiple_of(i * CHUNK, CHUNK)
            hsum = hf[pl.ds(base + r0, CHUNK), :] + hb[pl.ds(base + r0, CHUNK), :]
            y = hsum * jax.nn.sigmoid(o_ref[0, pl.ds(r0, CHUNK), :].astype(F32))
            ms = jnp.mean(y * y, axis=-1, keepdims=True)
            y_ref[0, pl.ds(r0, CHUNK), :] = (y * lax.rsqrt(ms + EPS) * nrm).astype(BF16)
            return 0

        lax.fori_loop(0, ts // CHUNK, body, 0)

    finish(ol_ref, yl_ref, tc, tl)
    if need_ctx:
        finish(oc_ref, yc_ref, 0, tc)


def _mlstm(ctx_p, lat_p, gates_c, gates_l, conv_w, m_norm, B, tc, tl, need_ctx):
    r3 = lambda a, t: a.reshape(B, t, a.shape[-1])
    qk_c, v_c, o_c = r3(ctx_p["mqk"], tc), r3(ctx_p["mv"], tc), r3(ctx_p["mo"], tc)
    qk_l, v_l, o_l = r3(lat_p["mqk"], tl), r3(lat_p["mv"], tl), r3(lat_p["mo"], tl)
    H = M_HEADS
    head_blk = lambda t, off: pl.BlockSpec((1, t, LANES), lambda b, h: (b, 0, h + off))
    gate_blk = lambda g: pl.BlockSpec((1,) + g.shape[1:], lambda b, h: (b, 0, 0, 0))
    in_specs = [head_blk(tc, 0), head_blk(tc, H), head_blk(tc, 0), head_blk(tc, 0), gate_blk(gates_c),
                head_blk(tl, 0), head_blk(tl, H), head_blk(tl, 0), head_blk(tl, 0), gate_blk(gates_l),
                pl.BlockSpec((3, LANES), lambda b, h: (0, h)),
                pl.BlockSpec((3, LANES), lambda b, h: (0, h + H)),
                pl.BlockSpec((1, LANES), lambda b, h: (0, h))]
    out_shape = [jax.ShapeDtypeStruct((B, tl, BRANCH), BF16)]
    out_specs = [head_blk(tl, 0)]
    if need_ctx:
        out_shape.append(jax.ShapeDtypeStruct((B, tc, BRANCH), BF16))
        out_specs.append(head_blk(tc, 0))
    tot = tc + tl
    scratch = [pltpu.VMEM((max(tc, tl) + 16, LANES), F32),
               pltpu.VMEM((tot, LANES), BF16), pltpu.VMEM((tot, LANES), BF16),
               pltpu.VMEM((tot, LANES), F32), pltpu.VMEM((tot, LANES), F32),
               pltpu.VMEM((2, 2 * LANES, LANES), F32), pltpu.VMEM((8, LANES), F32)]
    outs = pl.pallas_call(
        functools.partial(_mlstm_kernel, tc=tc, tl=tl, need_ctx=need_ctx),
        grid=(B, H),
        in_specs=in_specs,
        out_specs=out_specs,
        out_shape=out_shape,
        scratch_shapes=scratch,
        compiler_params=_params(("arbitrary", "arbitrary")),
        name="mlstm",
    )(qk_c, qk_c, v_c, o_c, gates_c, qk_l, qk_l, v_l, o_l, gates_l,
      conv_w, conv_w, m_norm.reshape(1, -1))
    ya = outs[0].reshape(B * tl, BRANCH)
    ya_c = outs[1].reshape(B * tc, BRANCH) if need_ctx else None
    return ya, ya_c


_WB = 128


def _win_kernel(sink_ref, q_ref, *refs, has_latent, nb):
    if has_latent:
        kp_ref, kn0_ref, kn_ref, vp_ref, vn0_ref, vn_ref, kx_ref, vx_ref, o_ref = refs
    else:
        kx_ref, vx_ref, o_ref = refs
    n = pl.program_id(1)
    rep = W_HEADS // W_KV_HEADS
    rows = rep * _WB
    lane = lax.broadcasted_iota(jnp.int32, (_WB, LANES), 1)
    left = lane < W_DIM
    row_i = lax.broadcasted_iota(jnp.int32, (rows, LANES), 0)
    qq = jnp.bitwise_and(row_i, _WB - 1)
    kk = lax.broadcasted_iota(jnp.int32, (rows, LANES), 1)
    zero = jnp.zeros((_WB, LANES), BF16)
    for g in range(W_KV_HEADS):
        pieces = []
        for t in range(rep // 2):
            qt = q_ref[0, :, (g * rep // 2 + t) * LANES:(g * rep // 2 + t + 1) * LANES]
            pieces += [jnp.where(left, qt, zero), jnp.where(left, zero, qt)]
        lhs = jnp.concatenate(pieces, axis=0)
        sink_col = jnp.zeros((rows, 1), F32)
        hrow = jnp.right_shift(lax.broadcasted_iota(jnp.int32, (rows, 1), 0), 7)
        for r in range(rep):
            sink_col = jnp.where(hrow == r, sink_ref[g * rep + r], sink_col)
        gsl = slice(g * LANES, (g + 1) * LANES)
        logits = [_dot_nt(lhs, kx_ref[0, :, gsl])]
        vals = [vx_ref[0, :, gsl]]
        if has_latent:
            s_prev = _dot_nt(lhs, kp_ref[0, :, gsl])
            s_prev = jnp.where(jnp.logical_and(kk >= qq, n > 0), s_prev, -jnp.inf)
            s_next = _dot_nt(lhs, kn_ref[0, :, gsl])
            s_next = jnp.where(jnp.logical_and(kk <= qq, n < nb - 1), s_next, -jnp.inf)
            logits += [s_prev, _dot_nt(lhs, kn0_ref[0, :, gsl]), s_next]
            vals += [vp_ref[0, :, gsl], vn0_ref[0, :, gsl], vn_ref[0, :, gsl]]
        m = sink_col
        for s in logits:
            m = jnp.maximum(m, jnp.max(s, axis=-1, keepdims=True))
        den = jnp.exp(sink_col - m)
        num = jnp.zeros((rows, LANES), F32)
        for s, v in zip(logits, vals):
            e = jnp.exp(s - m)
            den = den + jnp.sum(e, axis=-1, keepdims=True)
            num = num + _dot(e.astype(BF16), v)
        o = num / den
        for t in range(rep // 2):
            tile = jnp.where(left, o[(2 * t) * _WB:(2 * t + 1) * _WB], o[(2 * t + 1) * _WB:(2 * t + 2) * _WB])
            c0 = (g * rep // 2 + t) * LANES
            o_ref[0, :, c0:c0 + LANES] = tile.astype(BF16)


def _window_attention(q, kx, vx, sink, B, tq, tcx, lat_kv=None):
    q3 = q.reshape(B, tq, BRANCH)
    kx3, vx3 = kx.reshape(B, tcx, 2 * LANES), vx.reshape(B, tcx, 2 * LANES)
    nb = tq // _WB
    has_latent = lat_kv is not None
    in_specs = [pl.BlockSpec(memory_space=pltpu.SMEM),
                pl.BlockSpec((1, _WB, BRANCH), lambda b, n: (b, n, 0))]
    args = [sink.astype(F32), q3]
    if has_latent:
        k3, v3 = (a.reshape(B, tq, 2 * LANES) for a in lat_kv)
        blk = lambda f: pl.BlockSpec((1, _WB, 2 * LANES), f)
        prev = lambda b, n: (b, jnp.maximum(n - 1, 0), 0)
        cur = lambda b, n: (b, n, 0)
        nxt = lambda b, n: (b, jnp.minimum(n + 1, nb - 1), 0)
        in_specs += [blk(prev), blk(cur), blk(nxt), blk(prev), blk(cur), blk(nxt)]
        args += [k3, k3, k3, v3, v3, v3]
    ctx_blk = pl.BlockSpec((1, tcx, 2 * LANES), lambda b, n: (b, 0, 0))
    in_specs += [ctx_blk, ctx_blk]
    args += [kx3, vx3]
    out = pl.pallas_call(
        functools.partial(_win_kernel, has_latent=has_latent, nb=nb),
        grid=(B, nb),
        in_specs=in_specs,
        out_specs=pl.BlockSpec((1, _WB, BRANCH), lambda b, n: (b, n, 0)),
        out_shape=jax.ShapeDtypeStruct((B, tq, BRANCH), BF16),
        compiler_params=_params(("arbitrary", "arbitrary")),
        name="win_lat" if has_latent else "win_ctx",
    )(*args)
    return out.reshape(B * tq, BRANCH)


def _diff_kernel(lam_ref, q_ref, *refs, tq, tk, tlat, lam_init):
    if tlat:
        kx_ref, vx_ref, kl_ref, vl_ref, nrm_ref, o_ref = refs
    else:
        kx_ref, vx_ref, nrm_ref, o_ref = refs
    q = q_ref[0]
    lane = lax.broadcasted_iota(jnp.int32, (tq, LANES), 1)
    left = lane < LANES // 2
    zero = jnp.zeros((tq, LANES), BF16)
    lhs = jnp.concatenate([jnp.where(left, q, zero), jnp.where(left, zero, q)], axis=0)

    def attend(carry, k, v):
        m, l, acc = carry
        s = _dot_nt(lhs, k)
        m_new = jnp.maximum(m, jnp.max(s, axis=-1, keepdims=True))
        p = jnp.exp(s - m_new)
        alpha = jnp.exp(m - m_new)
        l = alpha * l + jnp.sum(p, axis=-1, keepdims=True)
        acc = alpha * acc + _dot(p.astype(BF16), v)
        return m_new, l, acc

    carry = (jnp.full((2 * tq, 1), -jnp.inf, F32), jnp.zeros((2 * tq, 1), F32), jnp.zeros((2 * tq, LANES), F32))
    carry = attend(carry, kx_ref[0], vx_ref[0])
    if tlat:
        def body(j, c):
            r0 = pl.multiple_of(j * tk, tk)
            return attend(c, kl_ref[0, pl.ds(r0, tk), :], vl_ref[0, pl.ds(r0, tk), :])

        carry = lax.fori_loop(0, tlat // tk, body, carry)
    m, l, acc = carry
    o = acc / l
    lv = lam_ref[...]
    s1 = jnp.sum(lv[0:1] * lv[1:2], axis=-1, keepdims=True)
    s2 = jnp.sum(lv[2:3] * lv[3:4], axis=-1, keepdims=True)
    lam = jnp.exp(s1) - jnp.exp(s2) + lam_init
    y = o[0:tq] - lam * o[tq:2 * tq]
    ms = jnp.mean(y * y, axis=-1, keepdims=True)
    o_ref[0] = (y * lax.rsqrt(ms + EPS) * nrm_ref[...] * (1.0 - lam_init)).astype(BF16)


def _diff_attention(q, kx, vx, lamv, d_norm, lam_init, B, tq_total, tcx, lat_kv=None, tq=256, tk=512):
    q3 = q.reshape(B, tq_total, BRANCH)
    kx3, vx3 = kx.reshape(B, tcx, BRANCH), vx.reshape(B, tcx, BRANCH)
    tq = min(tq, tq_total)
    head_blk = lambda t, f: pl.BlockSpec((1, t, LANES), f)
    whole = lambda b, h, i: (b, 0, h)
    in_specs = [pl.BlockSpec((8, LANES), lambda b, h, i: (0, 0)),
                head_blk(tq, lambda b, h, i: (b, i, h)),
                head_blk(tcx, whole), head_blk(tcx, whole)]
    args = [lamv, q3, kx3, vx3]
    tlat = 0
    if lat_kv is not None:
        tlat = tq_total
        k3, v3 = (a.reshape(B, tlat, BRANCH) for a in lat_kv)
        in_specs += [head_blk(tlat, whole), head_blk(tlat, whole)]
        args += [k3, v3]
    in_specs.append(pl.BlockSpec((1, LANES), lambda b, h, i: (0, h)))
    args.append(d_norm.reshape(1, -1))
    out = pl.pallas_call(
        functools.partial(_diff_kernel, tq=tq, tk=min(tk, max(tlat, 1)), tlat=tlat, lam_init=lam_init),
        grid=(B, DF_HEADS, tq_total // tq),
        in_specs=in_specs,
        out_specs=head_blk(tq, lambda b, h, i: (b, i, h)),
        out_shape=jax.ShapeDtypeStruct((B, tq_total, BRANCH), BF16),
        compiler_params=_params(("arbitrary", "arbitrary", "arbitrary")),
        name="diff_lat" if tlat else "diff_ctx",
    )(*args)
    return out.reshape(B * tq_total, BRANCH)


def _merge_kernel(ya_ref, yb_ref, yc_ref, gate_ref, x_ref, mod_ref, wb_ref, wo_ref, o_ref, *, d):
    merged = None
    for i, y_ref in enumerate((ya_ref, yb_ref, yc_ref)):
        gate = jax.nn.sigmoid(gate_ref[:, i * d:(i + 1) * d].astype(F32))
        term = gate * _dot(y_ref[...], wb_ref[i])
        merged = term if merged is None else merged + term
    out = _dot(merged.astype(BF16), wo_ref[...])
    gt = mod_ref[0][:, 2 * d:3 * d]
    o_ref[...] = x_ref[...] + gt * out


def _merge(ya, yb, yc, gate, x2, mod, wb, wo, *, tm, rows_per_mod):
    N, D = x2.shape
    row = lambda w: pl.BlockSpec((tm, w), lambda i: (i, 0))
    return pl.pallas_call(
        functools.partial(_merge_kernel, d=D),
        grid=(N // tm,),
        in_specs=[row(BRANCH), row(BRANCH), row(BRANCH), row(N_BRANCH * D), row(D),
                  pl.BlockSpec((1, 1, 6 * D), lambda i: ((i * tm) // rows_per_mod, 0, 0)),
                  _resident((N_BRANCH, BRANCH, D), lambda i: (0, 0, 0)),
                  _resident((D, D), lambda i: (0, 0))],
        out_specs=row(D),
        out_shape=jax.ShapeDtypeStruct((N, D), F32),
        compiler_params=_params(("arbitrary",)),
        name="merge",
    )(ya, yb, yc, gate, x2, mod, wb, wo)


_FFN_CHUNK = 256


def _ffn_kernel(x_ref, mod_ref, g_ref, wi_ref, wo_ref, gf_ref, o_ref, *, d, hidden, final_norm):
    x = x_ref[...]
    mod = mod_ref[0]
    h = _modnorm(x, g_ref[...], mod[:, 3 * d:4 * d], mod[:, 4 * d:5 * d]).astype(BF16)
    acc = None
    for c0 in range(0, hidden, _FFN_CHUNK):
        gate = _dot(h, wi_ref[:, c0:c0 + _FFN_CHUNK])
        up = _dot(h, wi_ref[:, hidden + c0:hidden + c0 + _FFN_CHUNK])
        a = (gate * jax.nn.sigmoid(gate) * up).astype(BF16)
        part = _dot(a, wo_ref[c0:c0 + _FFN_CHUNK, :])
        acc = part if acc is None else acc + part
    y = x + mod[:, 5 * d:6 * d] * acc
    if final_norm:
        ms = jnp.mean(y * y, axis=-1, keepdims=True)
        y = y * lax.rsqrt(ms + EPS) * gf_ref[...]
    o_ref[...] = y


def _ffn(x2, mod, g, wi, wo, g_final, *, tm, rows_per_mod, final_norm):
    N, D = x2.shape
    hidden = wo.shape[0]
    return pl.pallas_call(
        functools.partial(_ffn_kernel, d=D, hidden=hidden, final_norm=final_norm),
        grid=(N // tm,),
        in_specs=[pl.BlockSpec((tm, D), lambda i: (i, 0)),
                  pl.BlockSpec((1, 1, 6 * D), lambda i: ((i * tm) // rows_per_mod, 0, 0)),
                  pl.BlockSpec((1, D), lambda i: (0, 0)),
                  _resident((D, 2 * hidden), lambda i: (0, 0)),
                  _resident((hidden, D), lambda i: (0, 0)),
                  pl.BlockSpec((1, D), lambda i: (0, 0))],
        out_specs=pl.BlockSpec((tm, D), lambda i: (i, 0)),
        out_shape=jax.ShapeDtypeStruct((N, D), F32),
        compiler_params=_params(("arbitrary",)),
        name="ffn",
    )(x2, mod, g, wi, wo, g_final)


def _row_tile(n, want=512):
    t = want
    while n % t:
        t //= 2
    return t


def kernel(x, c, ctx, c_ctx, w_mod, b_mod, g_mix, g_ffn, w_in, b_gate, conv_w, m_norm, sink,
           lam_q1, lam_k1, lam_q2, lam_k2, d_norm, w_branch, w_out, w_ffn_in, w_ffn_out, g_final):
    B, T, D = x.shape
    Tc = ctx.shape[1]
    depth = w_mod.shape[0]
    assert T % CHUNK == 0 and Tc % CHUNK == 0 and T % GRID_W == 0

    n_rows = -(-(B + 1) // 16) * 16
    cc = jnp.concatenate([c, c_ctx[None, :], jnp.zeros((n_rows - B - 1, D), F32)], axis=0)
    mods = _adaln(cc, w_mod, b_mod)

    rope = _rope_tables(T)
    no_rope = tuple(jnp.zeros((Tc, LANES), F32) for _ in range(3))
    tm = _row_tile(T)
    tmc = _row_tile(Tc)
    names = [s[0] for s in _SEGS]

    xl = x.reshape(B * T, D)
    xc = ctx.reshape(B * Tc, D)
    for l in range(depth):
        need_ctx = l < depth - 1
        lam_init = 0.8 - 0.6 * math.exp(-0.3 * l)
        mod_l = mods[l, :B].reshape(B, 1, 6 * D)
        mod_c = mods[l, B:B + 1].reshape(1, 1, 6 * D)
        w_ext = _extend_w_in(w_in[l])
        g_mix_l = g_mix[l].reshape(1, D)
        g_ffn_l = g_ffn[l].reshape(1, D)
        lamv = jnp.zeros((8, LANES), F32)
        for i, v in enumerate((lam_q1[l], lam_k1[l], lam_q2[l], lam_k2[l])):
            lamv = lamv.at[i, :v.shape[0]].set(v.astype(F32))

        pl_ = dict(zip(names, _inproj(xl, mod_l, g_mix_l, rope, w_ext, tm=tm, rows_per_mod=T, seq=T, use_rope=True)))
        pc_ = dict(zip(names, _inproj(xc, mod_c, g_mix_l, no_rope, w_ext, tm=tmc, rows_per_mod=B * Tc, seq=Tc,
                                      use_rope=False)))

        gates_l = _gates(pl_["mg"], b_gate[l], B, T)
        gates_c = _gates(pc_["mg"], b_gate[l], B, Tc)
        ya, ya_c = _mlstm(pc_, pl_, gates_c, gates_l, conv_w[l], m_norm[l], B, Tc, T, need_ctx)
        yb = _window_attention(pl_["wq"], pc_["wk"], pc_["wv"], sink[l], B, T, Tc, lat_kv=(pl_["wk"], pl_["wv"]))
        yc = _diff_attention(pl_["dq"], pc_["dk"], pc_["dv"], lamv, d_norm[l], lam_init, B, T, Tc,
                             lat_kv=(pl_["dk"], pl_["dv"]))
        wb = w_branch[l].astype(BF16)
        wo = w_out[l].astype(BF16)
        wfi = w_ffn_in[l].astype(BF16)
        wfo = w_ffn_out[l].astype(BF16)
        gfin = g_final.reshape(1, D)
        last = l == depth - 1
        xl = _merge(ya, yb, yc, pl_["gate"], xl, mod_l, wb, wo, tm=tm, rows_per_mod=T)
        xl = _ffn(xl, mod_l, g_ffn_l, wfi, wfo, gfin, tm=tm, rows_per_mod=T, final_norm=last)
        if need_ctx:
            yb_c = _window_attention(pc_["wq"], pc_["wk"], pc_["wv"], sink[l], B, Tc, Tc)
            yc_c = _diff_attention(pc_["dq"], pc_["dk"], pc_["dv"], lamv, d_norm[l], lam_init, B, Tc, Tc)
            xc = _merge(ya_c, yb_c, yc_c, pc_["gate"], xc, mod_c, wb, wo, tm=tmc, rows_per_mod=B * Tc)
            xc = _ffn(xc, mod_c, g_ffn_l, wfi, wfo, gfin, tm=tmc, rows_per_mod=B * Tc, final_norm=False)
    return xl.reshape(B, T, D)
```

```python
import functools
import math

import jax
import jax.numpy as jnp
import numpy as np
from jax import lax
from jax.experimental import pallas as pl
from jax.experimental.pallas import tpu as pltpu

F32 = jnp.float32
BF16 = jnp.bfloat16

GRID_W = 64
BRANCH = 512
N_BRANCH = 3
M_HEADS = 4
M_DIM = 128
W_HEADS = 8
W_KV_HEADS = 2
W_DIM = 64
DF_HEADS = 4
ROPE_DIM = 64
ROPE_BASE = 10000.0
EPS = 1e-6
LANES = 128
CHUNK = 128
VMEM_LIMIT = 56 * 1024 * 1024

_NT = (((1,), (1,)), ((), ()))
LOG2E = math.log2(math.e)
_Q_SCALE = W_DIM ** -0.5 * LOG2E


def _dot(a, b):
    return jnp.dot(a, b, preferred_element_type=F32)


def _dot_nt(a, b):
    return lax.dot_general(a, b, _NT, preferred_element_type=F32)


def _params(sem):
    return pltpu.CompilerParams(dimension_semantics=sem, vmem_limit_bytes=VMEM_LIMIT)


def _resident(shape, index_map):
    return pl.BlockSpec(shape, index_map, pipeline_mode=pl.Buffered(1))


def _adaln_kernel(c_ref, w_ref, b_ref, o_ref):
    c = c_ref[...]
    s = (c * jax.nn.sigmoid(c)).astype(BF16)
    o_ref[0] = _dot(s, w_ref[0].astype(BF16)) + b_ref[0]


def _adaln(cc, w_mod, b_mod):
    L, D, N = w_mod.shape
    R = cc.shape[0]
    tn = 1536
    return pl.pallas_call(
        _adaln_kernel,
        grid=(L, N // tn),
        in_specs=[pl.BlockSpec((R, D), lambda l, j: (0, 0)),
                  pl.BlockSpec((1, D, tn), lambda l, j: (l, 0, j)),
                  pl.BlockSpec((1, 1, tn), lambda l, j: (l, 0, j))],
        out_specs=pl.BlockSpec((1, R, tn), lambda l, j: (l, 0, j)),
        out_shape=jax.ShapeDtypeStruct((L, R, N), F32),
        compiler_params=_params(("arbitrary", "arbitrary")),
        name="adaln",
    )(cc, w_mod, b_mod.reshape(L, 1, N))


def _modnorm(x, g, shift, scale):
    ms = jnp.mean(x * x, axis=-1, keepdims=True)
    return x * lax.rsqrt(ms + EPS) * (g * (1.0 + scale)) + shift


_SEGS = (("mqk", 1024, BF16, "plain"), ("mv", 512, BF16, "plain"), ("mo", 512, BF16, "plain"),
         ("mg", 128, F32, "plain"), ("wq", 512, BF16, "ropeq"), ("wk", 256, BF16, "rope"),
         ("wv", 256, BF16, "plain"), ("dq", 512, BF16, "ropeq"), ("dk", 512, BF16, "rope"),
         ("dv", 512, BF16, "plain"), ("gate", 3072, BF16, "plain"))
_EXT_WIDTH = sum(s[1] for s in _SEGS)
_COL_CHUNK = 512


def _inproj_kernel(x_ref, mod_ref, g_ref, cos_ref, sa_ref, sb_ref, w_ref, *out_refs, d, use_rope):
    x = x_ref[...]
    mod = mod_ref[0]
    h = _modnorm(x, g_ref[...], mod[:, 0:d], mod[:, d:2 * d]).astype(BF16)
    if use_rope:
        cos, sa, sb = cos_ref[...], sa_ref[...], sb_ref[...]
    off = 0
    for (name, width, dt, kind), o_ref in zip(_SEGS, out_refs):
        for c0 in range(0, width, _COL_CHUNK):
            cw = min(_COL_CHUNK, width - c0)
            acc = _dot(h, w_ref[:, off + c0:off + c0 + cw])
            if kind != "plain":
                tiles = []
                for t in range(cw // LANES):
                    a = acc[:, t * LANES:(t + 1) * LANES]
                    if use_rope:
                        a = (a * cos + pltpu.roll(a, LANES - 16, 1) * sa + pltpu.roll(a, 16, 1) * sb)
                    if kind == "ropeq":
                        a = a * _Q_SCALE
                    tiles.append(a)
                acc = jnp.concatenate(tiles, axis=1) if len(tiles) > 1 else tiles[0]
            o_ref[:, c0:c0 + cw] = acc.astype(dt)
        off += width


def _inproj(x2, mod, g, rope, w_ext, *, tm, rows_per_mod, seq, use_rope):
    N, D = x2.shape
    nt = seq // tm
    kern = functools.partial(_inproj_kernel, d=D, use_rope=use_rope)
    rope_spec = pl.BlockSpec((tm, LANES), lambda i: (i % nt, 0))
    out_shape = [jax.ShapeDtypeStruct((N, w), dt) for _, w, dt, _ in _SEGS]
    out_specs = [pl.BlockSpec((tm, w), lambda i: (i, 0)) for _, w, _, _ in _SEGS]
    return pl.pallas_call(
        kern,
        grid=(N // tm,),
        in_specs=[pl.BlockSpec((tm, D), lambda i: (i, 0)),
                  pl.BlockSpec((1, 1, 6 * D), lambda i: ((i * tm) // rows_per_mod, 0, 0)),
                  pl.BlockSpec((1, D), lambda i: (0, 0)),
                  rope_spec, rope_spec, rope_spec,
                  _resident((D, _EXT_WIDTH), lambda i: (0, 0))],
        out_specs=out_specs,
        out_shape=out_shape,
        compiler_params=_params(("arbitrary",)),
        name="inproj_lat" if use_rope else "inproj_ctx",
    )(x2, mod, g, *rope, w_ext)


def _extend_w_in(w):
    D = w.shape[0]
    o = np.cumsum([0, 512, 512, 512, 512, 16, 512, 128, 128, 512, 512, 512, 3072])
    p = [w[:, o[i]:o[i + 1]] for i in range(12)]
    dup = lambda a: jnp.concatenate([a[:, 0:64], a[:, 0:64], a[:, 64:128], a[:, 64:128]], axis=1)
    mg = jnp.concatenate([p[4], jnp.zeros((D, LANES - 16), w.dtype)], axis=1)
    ext = jnp.concatenate([p[0], p[1], p[2], p[3], mg, p[5], dup(p[6]), dup(p[7]), p[8], p[9], p[10], p[11]], axis=1)
    return ext.astype(BF16)


def _rope_tables(n_tokens):
    rows = n_tokens // GRID_W
    r, col = jnp.meshgrid(jnp.arange(rows), jnp.arange(GRID_W), indexing="ij")
    half = ROPE_DIM // 2
    inv = ROPE_BASE ** (-jnp.arange(0, half, 2, dtype=F32) / half)
    ang_r = r.reshape(-1, 1).astype(F32) * inv
    ang_c = col.reshape(-1, 1).astype(F32) * inv
    ang = jnp.concatenate([ang_r, ang_r, ang_c, ang_c], axis=-1)
    cos, sin = jnp.cos(ang), jnp.sin(ang)
    cos2 = jnp.concatenate([cos, cos], axis=-1)
    sin2 = jnp.concatenate([sin, sin], axis=-1)
    first = (jnp.arange(LANES) % 32) < 16
    sin_a = jnp.where(first, -sin2, 0.0)
    sin_b = jnp.where(first, 0.0, sin2)
    return cos2, sin_a, sin_b


def _scan_lanes(x, op, fill, reverse):
    lane = lax.broadcasted_iota(jnp.int32, x.shape, 1)
    d = 1
    while d < LANES:
        if reverse:
            shifted = jnp.where(lane < LANES - d, pltpu.roll(x, LANES - d, 1), fill)
        else:
            shifted = jnp.where(lane >= d, pltpu.roll(x, d, 1), fill)
        x = op(x, shifted)
        d *= 2
    return x


def _gates_kernel(x_ref, bias_ref, o_ref, *, rows):
    x = x_ref[0] + bias_ref[...]
    lane = lax.broadcasted_iota(jnp.int32, (rows, LANES), 1)
    for d, reverse in enumerate((False, True)):
        i_pre = x[(2 * d) * rows:(2 * d + 1) * rows]
        f_pre = x[(2 * d + 1) * rows:(2 * d + 2) * rows]
        log_f = jax.nn.log_sigmoid(f_pre)
        b = _scan_lanes(log_f, jnp.add, 0.0, reverse)
        a = i_pre - b
        pm = _scan_lanes(a, jnp.maximum, -jnp.inf, reverse)
        last = 0 if reverse else LANES - 1
        b_end = jnp.sum(jnp.where(lane == last, b, 0.0), axis=1, keepdims=True)
        a_max = jnp.max(a, axis=1, keepdims=True)
        o_ref[0, 5 * d + 0] = b
        o_ref[0, 5 * d + 1] = a
        o_ref[0, 5 * d + 2] = pm
        o_ref[0, 5 * d + 3] = jnp.broadcast_to(b_end, (rows, LANES))
        o_ref[0, 5 * d + 4] = jnp.broadcast_to(a_max, (rows, LANES))


def _gates(mg, b_gate, B, T):
    nc = T // CHUNK
    rows = M_HEADS * nc
    g = mg[:, :4 * M_HEADS].reshape(B, T, 4 * M_HEADS)
    gt = jnp.transpose(g, (0, 2, 1)).reshape(B, 4 * rows, LANES)
    bias = jnp.repeat(b_gate.astype(F32), nc).reshape(4 * rows, 1)
    return pl.pallas_call(
        functools.partial(_gates_kernel, rows=rows),
        grid=(B,),
        in_specs=[pl.BlockSpec((1, 4 * rows, LANES), lambda b: (b, 0, 0)),
                  pl.BlockSpec((4 * rows, 1), lambda b: (0, 0))],
        out_specs=pl.BlockSpec((1, 10, rows, LANES), lambda b: (b, 0, 0, 0)),
        out_shape=jax.ShapeDtypeStruct((B, 10, rows, LANES), F32),
        compiler_params=_params(("arbitrary",)),
        name="mlstm_gates",
    )(gt, bias)


_CONV_TILE = 256


def _mlstm_kernel(qc_ref, kc_ref, vc_ref, oc_ref, pc_ref, ql_ref, kl_ref, vl_ref, ol_ref, pl_ref,
                  cwq_ref, cwk_ref, nrm_ref, *rest, tc, tl, need_ctx):
    if need_ctx:
        yl_ref, yc_ref = rest[0], rest[1]
        scr = rest[2:]
    else:
        yl_ref, yc_ref = rest[0], None
        scr = rest[1:]
    xpad, qs, ks, hf, hb, cx, mst = scr
    head = pl.program_id(1)
    tot = tc + tl

    def conv_stream(u_ref, w_ref, dst, dst_off, ts, scale):
        xpad[0:8, :] = jnp.zeros((8, LANES), F32)
        xpad[8:8 + ts, :] = u_ref[0].astype(F32)
        xpad[8 + ts:16 + ts, :] = jnp.zeros((8, LANES), F32)
        w = w_ref[...]
        w0, w1, w2 = w[0:1], w[1:2], w[2:3]
        tile = min(_CONV_TILE, ts)

        def body(i, _):
            r0 = pl.multiple_of(i * tile, 8)
            win = xpad[pl.ds(r0, tile + 16), :]
            prev = pltpu.roll(win, 1, 0)[8:8 + tile]
            nxt = pltpu.roll(win, tile + 15, 0)[8:8 + tile]
            y = prev * w0 + win[8:8 + tile] * w1 + nxt * w2
            y = y * jax.nn.sigmoid(y) * scale
            dst[pl.ds(pl.multiple_of(dst_off + r0, 8), tile), :] = y.astype(BF16)
            return 0

        lax.fori_loop(0, ts // tile, body, 0)

    kscale = M_DIM ** -0.5
    conv_stream(qc_ref, cwq_ref, qs, 0, tc, 1.0)
    conv_stream(kc_ref, cwk_ref, ks, 0, tc, kscale)
    conv_stream(ql_ref, cwq_ref, qs, tc, tl, 1.0)
    conv_stream(kl_ref, cwk_ref, ks, tc, tl, kscale)

    cx[...] = jnp.zeros(cx.shape, F32)
    mst[...] = jnp.zeros(mst.shape, F32)

    row_i = lax.broadcasted_iota(jnp.int32, (CHUNK, CHUNK), 0)
    col_i = lax.broadcasted_iota(jnp.int32, (CHUNK, CHUNK), 1)
    eye = row_i == col_i
    masks = (col_i <= row_i, col_i >= row_i)
    e0_cols = jnp.where(col_i == 0, 1.0, 0.0).astype(BF16)
    e0_rows = jnp.where(row_i == 0, 1.0, 0.0).astype(F32)

    def colify(row):
        return jnp.sum(jnp.where(eye, row, 0.0), axis=1, keepdims=True)

    def chunk_step(d, c, p_ref, v_ref, nc, base, want_h):
        r = head * nc + c
        plane = lambda j: p_ref[0, 5 * d + j, pl.ds(r, 1), :]
        b_row, a_row, pm_row, bend_row, amax_row = (plane(j) for j in range(5))
        m_prev = mst[d:d + 1, :]
        m_end = jnp.maximum(m_prev, amax_row)
        w_end = jnp.exp(a_row - m_end)
        decay = jnp.exp(m_prev - m_end)
        row0 = pl.multiple_of(base + c * CHUNK, CHUNK)
        q = qs[pl.ds(row0, CHUNK), :]
        k = ks[pl.ds(row0, CHUNK), :]
        v = v_ref[0, pl.ds(pl.multiple_of(c * CHUNK, CHUNK), CHUNK), :]
        c_old = cx[d]
        if want_h:
            m_col = jnp.maximum(m_prev[:, 0:1], colify(pm_row))
            b_col = colify(b_row)
            w_intra = jnp.exp(jnp.where(masks[d], a_row - m_col, -jnp.inf))
            w_inter = jnp.exp(m_prev[:, 0:1] - m_col)
            floor = jnp.exp(-(b_col + m_col))
            s = _dot_nt(q, k) * w_intra
            vx = jnp.concatenate([v, e0_cols], axis=1)
            tot_ = _dot(s.astype(BF16), vx) + w_inter * _dot_nt(q, c_old.astype(BF16))
            num = tot_[:, 0:M_DIM]
            den = tot_[:, M_DIM:M_DIM + 1]
            hval = num / jnp.maximum(jnp.abs(den), floor)
            dst = hf if d == 0 else hb
            dst[pl.ds(row0, CHUNK), :] = hval
        vxt = jnp.concatenate([v.astype(F32).T, e0_rows], axis=0)
        cx[d] = decay * c_old + _dot((vxt * w_end).astype(BF16), k)
        mst[d:d + 1, :] = bend_row + m_end

    ncc, ncl = tc // CHUNK, tl // CHUNK
    for c in range(ncc):
        chunk_step(0, c, pc_ref, vc_ref, ncc, 0, need_ctx)
        chunk_step(1, ncc - 1 - c, pc_ref, vc_ref, ncc, 0, need_ctx)

    def lat_body(c, _):
        chunk_step(0, c, pl_ref, vl_ref, ncl, tc, True)
        chunk_step(1, ncl - 1 - c, pl_ref, vl_ref, ncl, tc, True)
        return 0

    lax.fori_loop(0, ncl, lat_body, 0, unroll=2)

    nrm = nrm_ref[...]

    def finish(o_ref, y_ref, base, ts):
        def body(i, _):
            r0 = pl.multiple_of(i * CHUNK, CHUNK)
            hsum = hf[pl.ds(base + r0, CHUNK), :] + hb[pl.ds(base + r0, CHUNK), :]
            y = hsum * jax.nn.sigmoid(o_ref[0, pl.ds(r0, CHUNK), :].astype(F32))
            ms = jnp.mean(y * y, axis=-1, keepdims=True)
            y_ref[0, pl.ds(r0, CHUNK), :] = (y * lax.rsqrt(ms + EPS) * nrm).astype(BF16)
            return 0

        lax.fori_loop(0, ts // CHUNK, body, 0, unroll=min(4, ts // CHUNK))

    finish(ol_ref, yl_ref, tc, tl)
    if need_ctx:
        finish(oc_ref, yc_ref, 0, tc)


def _mlstm(ctx_p, lat_p, gates_c, gates_l, conv_w, m_norm, B, tc, tl, need_ctx):
    r3 = lambda a, t: a.reshape(B, t, a.shape[-1])
    qk_c, v_c, o_c = r3(ctx_p["mqk"], tc), r3(ctx_p["mv"], tc), r3(ctx_p["mo"], tc)
    qk_l, v_l, o_l = r3(lat_p["mqk"], tl), r3(lat_p["mv"], tl), r3(lat_p["mo"], tl)
    H = M_HEADS
    head_blk = lambda t, off: pl.BlockSpec((1, t, LANES), lambda b, h: (b, 0, h + off))
    gate_blk = lambda g: pl.BlockSpec((1,) + g.shape[1:], lambda b, h: (b, 0, 0, 0))
    in_specs = [head_blk(tc, 0), head_blk(tc, H), head_blk(tc, 0), head_blk(tc, 0), gate_blk(gates_c),
                head_blk(tl, 0), head_blk(tl, H), head_blk(tl, 0), head_blk(tl, 0), gate_blk(gates_l),
                pl.BlockSpec((3, LANES), lambda b, h: (0, h)),
                pl.BlockSpec((3, LANES), lambda b, h: (0, h + H)),
                pl.BlockSpec((1, LANES), lambda b, h: (0, h))]
    out_shape = [jax.ShapeDtypeStruct((B, tl, BRANCH), BF16)]
    out_specs = [head_blk(tl, 0)]
    if need_ctx:
        out_shape.append(jax.ShapeDtypeStruct((B, tc, BRANCH), BF16))
        out_specs.append(head_blk(tc, 0))
    tot = tc + tl
    scratch = [pltpu.VMEM((max(tc, tl) + 16, LANES), F32),
               pltpu.VMEM((tot, LANES), BF16), pltpu.VMEM((tot, LANES), BF16),
               pltpu.VMEM((tot, LANES), F32), pltpu.VMEM((tot, LANES), F32),
               pltpu.VMEM((2, 2 * LANES, LANES), F32), pltpu.VMEM((8, LANES), F32)]
    outs = pl.pallas_call(
        functools.partial(_mlstm_kernel, tc=tc, tl=tl, need_ctx=need_ctx),
        grid=(B, H),
        in_specs=in_specs,
        out_specs=out_specs,
        out_shape=out_shape,
        scratch_shapes=scratch,
        compiler_params=_params(("arbitrary", "arbitrary")),
        name="mlstm",
    )(qk_c, qk_c, v_c, o_c, gates_c, qk_l, qk_l, v_l, o_l, gates_l,
      conv_w, conv_w, m_norm.reshape(1, -1))
    ya = outs[0].reshape(B * tl, BRANCH)
    ya_c = outs[1].reshape(B * tc, BRANCH) if need_ctx else None
    return ya, ya_c


_WB = 128


def _win_kernel(sink_ref, q_ref, *refs, has_latent, nb):
    if has_latent:
        kp_ref, kn0_ref, kn_ref, vp_ref, vn0_ref, vn_ref, kx_ref, vx_ref, o_ref = refs
    else:
        kx_ref, vx_ref, o_ref = refs
    n = pl.program_id(1)
    rep = W_HEADS // W_KV_HEADS
    rows = rep * _WB
    lane = lax.broadcasted_iota(jnp.int32, (_WB, LANES), 1)
    left = lane < W_DIM
    row_i = lax.broadcasted_iota(jnp.int32, (rows, LANES), 0)
    qq = jnp.bitwise_and(row_i, _WB - 1)
    kk = lax.broadcasted_iota(jnp.int32, (rows, LANES), 1)
    zero = jnp.zeros((_WB, LANES), BF16)
    for g in range(W_KV_HEADS):
        pieces = []
        for t in range(rep // 2):
            qt = q_ref[0, :, (g * rep // 2 + t) * LANES:(g * rep // 2 + t + 1) * LANES]
            pieces += [jnp.where(left, qt, zero), jnp.where(left, zero, qt)]
        lhs = jnp.concatenate(pieces, axis=0)
        sink_col = jnp.concatenate([jnp.full((_WB, 1), sink_ref[g * rep + r] * LOG2E, F32) for r in range(rep)], axis=0)
        gsl = slice(g * LANES, (g + 1) * LANES)
        logits = [_dot_nt(lhs, kx_ref[0, :, gsl])]
        vals = [vx_ref[0, :, gsl]]
        if has_latent:
            s_prev = _dot_nt(lhs, kp_ref[0, :, gsl])
            s_prev = jnp.where(jnp.logical_and(kk >= qq, n > 0), s_prev, -jnp.inf)
            s_next = _dot_nt(lhs, kn_ref[0, :, gsl])
            s_next = jnp.where(jnp.logical_and(kk <= qq, n < nb - 1), s_next, -jnp.inf)
            logits += [s_prev, _dot_nt(lhs, kn0_ref[0, :, gsl]), s_next]
            vals += [vp_ref[0, :, gsl], vn0_ref[0, :, gsl], vn_ref[0, :, gsl]]
        s_all = jnp.concatenate(logits, axis=1)
        m = jnp.maximum(sink_col, jnp.max(s_all, axis=-1, keepdims=True))
        e_all = jnp.exp2(s_all - m)
        den = jnp.exp2(sink_col - m) + jnp.sum(e_all, axis=-1, keepdims=True)
        o = _dot(e_all.astype(BF16), jnp.concatenate(vals, axis=0)) / den
        for t in range(rep // 2):
            tile = jnp.where(left, o[(2 * t) * _WB:(2 * t + 1) * _WB], o[(2 * t + 1) * _WB:(2 * t + 2) * _WB])
            c0 = (g * rep // 2 + t) * LANES
            o_ref[0, :, c0:c0 + LANES] = tile.astype(BF16)


def _window_attention(q, kx, vx, sink, B, tq, tcx, lat_kv=None):
    q3 = q.reshape(B, tq, BRANCH)
    kx3, vx3 = kx.reshape(B, tcx, 2 * LANES), vx.reshape(B, tcx, 2 * LANES)
    nb = tq // _WB
    has_latent = lat_kv is not None
    in_specs = [pl.BlockSpec(memory_space=pltpu.SMEM),
                pl.BlockSpec((1, _WB, BRANCH), lambda b, n: (b, n, 0))]
    args = [sink.astype(F32), q3]
    if has_latent:
        k3, v3 = (a.reshape(B, tq, 2 * LANES) for a in lat_kv)
        blk = lambda f: pl.BlockSpec((1, _WB, 2 * LANES), f)
        prev = lambda b, n: (b, jnp.maximum(n - 1, 0), 0)
        cur = lambda b, n: (b, n, 0)
        nxt = lambda b, n: (b, jnp.minimum(n + 1, nb - 1), 0)
        in_specs += [blk(prev), blk(cur), blk(nxt), blk(prev), blk(cur), blk(nxt)]
        args += [k3, k3, k3, v3, v3, v3]
    ctx_blk = pl.BlockSpec((1, tcx, 2 * LANES), lambda b, n: (b, 0, 0))
    in_specs += [ctx_blk, ctx_blk]
    args += [kx3, vx3]
    out = pl.pallas_call(
        functools.partial(_win_kernel, has_latent=has_latent, nb=nb),
        grid=(B, nb),
        in_specs=in_specs,
        out_specs=pl.BlockSpec((1, _WB, BRANCH), lambda b, n: (b, n, 0)),
        out_shape=jax.ShapeDtypeStruct((B, tq, BRANCH), BF16),
        compiler_params=_params(("arbitrary", "arbitrary")),
        name="win_lat" if has_latent else "win_ctx",
    )(*args)
    return out.reshape(B * tq, BRANCH)


def _diff_kernel(lam_ref, q_ref, *refs, tq, tk, tlat, lam_init):
    if tlat:
        kx_ref, vx_ref, kl_ref, vl_ref, nrm_ref, o_ref = refs
    else:
        kx_ref, vx_ref, nrm_ref, o_ref = refs
    q = q_ref[0]
    lane = lax.broadcasted_iota(jnp.int32, (tq, LANES), 1)
    left = lane < LANES // 2
    zero = jnp.zeros((tq, LANES), BF16)
    lhs = jnp.concatenate([jnp.where(left, q, zero), jnp.where(left, zero, q)], axis=0)

    def attend(carry, k, v):
        m, l, acc = carry
        s = _dot_nt(lhs, k)
        m_new = jnp.maximum(m, jnp.max(s, axis=-1, keepdims=True))
        p = jnp.exp2(s - m_new)
        alpha = jnp.exp2(m - m_new)
        l = alpha * l + jnp.sum(p, axis=-1, keepdims=True)
        acc = alpha * acc + _dot(p.astype(BF16), v)
        return m_new, l, acc

    carry = (jnp.full((2 * tq, 1), -jnp.inf, F32), jnp.zeros((2 * tq, 1), F32), jnp.zeros((2 * tq, LANES), F32))
    carry = attend(carry, kx_ref[0], vx_ref[0])
    for j in range(tlat // tk):
        carry = attend(carry, kl_ref[0, j * tk:(j + 1) * tk, :], vl_ref[0, j * tk:(j + 1) * tk, :])
    m, l, acc = carry
    o = acc / l
    lv = lam_ref[...]
    s1 = jnp.sum(lv[0:1] * lv[1:2], axis=-1, keepdims=True)
    s2 = jnp.sum(lv[2:3] * lv[3:4], axis=-1, keepdims=True)
    lam = jnp.exp(s1) - jnp.exp(s2) + lam_init
    y = o[0:tq] - lam * o[tq:2 * tq]
    ms = jnp.mean(y * y, axis=-1, keepdims=True)
    o_ref[0] = (y * lax.rsqrt(ms + EPS) * nrm_ref[...] * (1.0 - lam_init)).astype(BF16)


def _diff_attention(q, kx, vx, lamv, d_norm, lam_init, B, tq_total, tcx, lat_kv=None, tq=256, tk=512):
    q3 = q.reshape(B, tq_total, BRANCH)
    kx3, vx3 = kx.reshape(B, tcx, BRANCH), vx.reshape(B, tcx, BRANCH)
    tq = min(tq, tq_total)
    head_blk = lambda t, f: pl.BlockSpec((1, t, LANES), f)
    whole = lambda b, h, i: (b, 0, h)
    in_specs = [pl.BlockSpec((8, LANES), lambda b, h, i: (0, 0)),
                head_blk(tq, lambda b, h, i: (b, i, h)),
                head_blk(tcx, whole), head_blk(tcx, whole)]
    args = [lamv, q3, kx3, vx3]
    tlat = 0
    if lat_kv is not None:
        tlat = tq_total
        k3, v3 = (a.reshape(B, tlat, BRANCH) for a in lat_kv)
        in_specs += [head_blk(tlat, whole), head_blk(tlat, whole)]
        args += [k3, v3]
    in_specs.append(pl.BlockSpec((1, LANES), lambda b, h, i: (0, h)))
    args.append(d_norm.reshape(1, -1))
    out = pl.pallas_call(
        functools.partial(_diff_kernel, tq=tq, tk=min(tk, max(tlat, 1)), tlat=tlat, lam_init=lam_init),
        grid=(B, DF_HEADS, tq_total // tq),
        in_specs=in_specs,
        out_specs=head_blk(tq, lambda b, h, i: (b, i, h)),
        out_shape=jax.ShapeDtypeStruct((B, tq_total, BRANCH), BF16),
        compiler_params=_params(("arbitrary", "arbitrary", "arbitrary")),
        name="diff_lat" if tlat else "diff_ctx",
    )(*args)
    return out.reshape(B * tq_total, BRANCH)


def _merge_kernel(ya_ref, yb_ref, yc_ref, gate_ref, x_ref, mod_ref, wb_ref, wo_ref, o_ref, *, d):
    merged = None
    for i, y_ref in enumerate((ya_ref, yb_ref, yc_ref)):
        gate = jax.nn.sigmoid(gate_ref[:, i * d:(i + 1) * d].astype(F32))
        term = gate * _dot(y_ref[...], wb_ref[i])
        merged = term if merged is None else merged + term
    out = _dot(merged.astype(BF16), wo_ref[...])
    gt = mod_ref[0][:, 2 * d:3 * d]
    o_ref[...] = x_ref[...] + gt * out


def _merge(ya, yb, yc, gate, x2, mod, wb, wo, *, tm, rows_per_mod):
    N, D = x2.shape
    row = lambda w: pl.BlockSpec((tm, w), lambda i: (i, 0))
    return pl.pallas_call(
        functools.partial(_merge_kernel, d=D),
        grid=(N // tm,),
        in_specs=[row(BRANCH), row(BRANCH), row(BRANCH), row(N_BRANCH * D), row(D),
                  pl.BlockSpec((1, 1, 6 * D), lambda i: ((i * tm) // rows_per_mod, 0, 0)),
                  _resident((N_BRANCH, BRANCH, D), lambda i: (0, 0, 0)),
                  _resident((D, D), lambda i: (0, 0))],
        out_specs=row(D),
        out_shape=jax.ShapeDtypeStruct((N, D), F32),
        compiler_params=_params(("arbitrary",)),
        name="merge",
    )(ya, yb, yc, gate, x2, mod, wb, wo)


_FFN_CHUNK = 256


def _ffn_kernel(x_ref, mod_ref, g_ref, wi_ref, wo_ref, gf_ref, o_ref, *, d, hidden, final_norm):
    x = x_ref[...]
    mod = mod_ref[0]
    h = _modnorm(x, g_ref[...], mod[:, 3 * d:4 * d], mod[:, 4 * d:5 * d]).astype(BF16)
    acc = None
    for c0 in range(0, hidden, _FFN_CHUNK):
        gate = _dot(h, wi_ref[:, c0:c0 + _FFN_CHUNK])
        up = _dot(h, wi_ref[:, hidden + c0:hidden + c0 + _FFN_CHUNK])
        a = (gate * jax.nn.sigmoid(gate) * up).astype(BF16)
        part = _dot(a, wo_ref[c0:c0 + _FFN_CHUNK, :])
        acc = part if acc is None else acc + part
    y = x + mod[:, 5 * d:6 * d] * acc
    if final_norm:
        ms = jnp.mean(y * y, axis=-1, keepdims=True)
        y = y * lax.rsqrt(ms + EPS) * gf_ref[...]
    o_ref[...] = y


def _ffn(x2, mod, g, wi, wo, g_final, *, tm, rows_per_mod, final_norm):
    N, D = x2.shape
    hidden = wo.shape[0]
    return pl.pallas_call(
        functools.partial(_ffn_kernel, d=D, hidden=hidden, final_norm=final_norm),
        grid=(N // tm,),
        in_specs=[pl.BlockSpec((tm, D), lambda i: (i, 0)),
                  pl.BlockSpec((1, 1, 6 * D), lambda i: ((i * tm) // rows_per_mod, 0, 0)),
                  pl.BlockSpec((1, D), lambda i: (0, 0)),
                  _resident((D, 2 * hidden), lambda i: (0, 0)),
                  _resident((hidden, D), lambda i: (0, 0)),
                  pl.BlockSpec((1, D), lambda i: (0, 0))],
        out_specs=pl.BlockSpec((tm, D), lambda i: (i, 0)),
        out_shape=jax.ShapeDtypeStruct((N, D), F32),
        compiler_params=_params(("arbitrary",)),
        name="ffn",
    )(x2, mod, g, wi, wo, g_final)


def _row_tile(n, want=512):
    t = want
    while n % t:
        t //= 2
    return t


def kernel(x, c, ctx, c_ctx, w_mod, b_mod, g_mix, g_ffn, w_in, b_gate, conv_w, m_norm, sink,
           lam_q1, lam_k1, lam_q2, lam_k2, d_norm, w_branch, w_out, w_ffn_in, w_ffn_out, g_final):
    B, T, D = x.shape
    Tc = ctx.shape[1]
    depth = w_mod.shape[0]
    assert T % CHUNK == 0 and Tc % CHUNK == 0 and T % GRID_W == 0

    n_rows = -(-(B + 1) // 16) * 16
    cc = jnp.concatenate([c, c_ctx[None, :], jnp.zeros((n_rows - B - 1, D), F32)], axis=0)
    mods = _adaln(cc, w_mod, b_mod)

    rope = _rope_tables(T)
    no_rope = tuple(jnp.zeros((Tc, LANES), F32) for _ in range(3))
    tm = _row_tile(T)
    tmc = _row_tile(Tc)
    names = [s[0] for s in _SEGS]

    xl = x.reshape(B * T, D)
    xc = ctx.reshape(B * Tc, D)
    for l in range(depth):
        need_ctx = l < depth - 1
        lam_init = 0.8 - 0.6 * math.exp(-0.3 * l)
        mod_l = mods[l, :B].reshape(B, 1, 6 * D)
        mod_c = mods[l, B:B + 1].reshape(1, 1, 6 * D)
        w_ext = _extend_w_in(w_in[l])
        g_mix_l = g_mix[l].reshape(1, D)
        g_ffn_l = g_ffn[l].reshape(1, D)
        lamv = jnp.zeros((8, LANES), F32)
        for i, v in enumerate((lam_q1[l], lam_k1[l], lam_q2[l], lam_k2[l])):
            lamv = lamv.at[i, :v.shape[0]].set(v.astype(F32))

        pl_ = dict(zip(names, _inproj(xl, mod_l, g_mix_l, rope, w_ext, tm=tm, rows_per_mod=T, seq=T, use_rope=True)))
        pc_ = dict(zip(names, _inproj(xc, mod_c, g_mix_l, no_rope, w_ext, tm=tmc, rows_per_mod=B * Tc, seq=Tc,
                                      use_rope=False)))

        gates_l = _gates(pl_["mg"], b_gate[l], B, T)
        gates_c = _gates(pc_["mg"], b_gate[l], B, Tc)
        ya, ya_c = _mlstm(pc_, pl_, gates_c, gates_l, conv_w[l], m_norm[l], B, Tc, T, need_ctx)
        yb = _window_attention(pl_["wq"], pc_["wk"], pc_["wv"], sink[l], B, T, Tc, lat_kv=(pl_["wk"], pl_["wv"]))
        yc = _diff_attention(pl_["dq"], pc_["dk"], pc_["dv"], lamv, d_norm[l], lam_init, B, T, Tc,
                             lat_kv=(pl_["dk"], pl_["dv"]))
        wb = w_branch[l].astype(BF16)
        wo = w_out[l].astype(BF16)
        wfi = w_ffn_in[l].astype(BF16)
        wfo = w_ffn_out[l].astype(BF16)
        gfin = g_final.reshape(1, D)
        last = l == depth - 1
        xl = _merge(ya, yb, yc, pl_["gate"], xl, mod_l, wb, wo, tm=tm, rows_per_mod=T)
        xl = _ffn(xl, mod_l, g_ffn_l, wfi, wfo, gfin, tm=tm, rows_per_mod=T, final_norm=last)
        if need_ctx:
            yb_c = _window_attention(pc_["wq"], pc_["wk"], pc_["wv"], sink[l], B, Tc, Tc)
            yc_c = _diff_attention(pc_["dq"], pc_["dk"], pc_["dv"], lamv, d_norm[l], lam_init, B, Tc, Tc)
            xc = _merge(ya_c, yb_c, yc_c, pc_["gate"], xc, mod_c, wb, wo, tm=tmc, rows_per_mod=B * Tc)
            xc = _ffn(xc, mod_c, g_ffn_l, wfi, wfo, gfin, tm=tmc, rows_per_mod=B * Tc, final_norm=False)
    return xl.reshape(B, T, D)
```

```python
import functools
import math

import jax
import jax.numpy as jnp
import numpy as np
from jax import lax
from jax.experimental import pallas as pl
from jax.experimental.pallas import tpu as pltpu

F32 = jnp.float32
BF16 = jnp.bfloat16

GRID_W = 64
BRANCH = 512
N_BRANCH = 3
M_HEADS = 4
M_DIM = 128
W_HEADS = 8
W_KV_HEADS = 2
W_DIM = 64
DF_HEADS = 4
ROPE_DIM = 64
ROPE_BASE = 10000.0
EPS = 1e-6
LANES = 128
CHUNK = 128
VMEM_LIMIT = 56 * 1024 * 1024

_NT = (((1,), (1,)), ((), ()))
LOG2E = math.log2(math.e)
_Q_SCALE = W_DIM ** -0.5 * LOG2E


def _dot(a, b):
    return jnp.dot(a, b, preferred_element_type=F32)


def _dot_nt(a, b):
    return lax.dot_general(a, b, _NT, preferred_element_type=F32)


def _params(sem):
    return pltpu.CompilerParams(dimension_semantics=sem, vmem_limit_bytes=VMEM_LIMIT)


def _resident(shape, index_map):
    return pl.BlockSpec(shape, index_map, pipeline_mode=pl.Buffered(1))


def _adaln_kernel(c_ref, w_ref, b_ref, o_ref):
    c = c_ref[...]
    s = (c * jax.nn.sigmoid(c)).astype(BF16)
    o_ref[0] = _dot(s, w_ref[0].astype(BF16)) + b_ref[0]


def _adaln(cc, w_mod, b_mod):
    L, D, N = w_mod.shape
    R = cc.shape[0]
    tn = 1536
    return pl.pallas_call(
        _adaln_kernel,
        grid=(L, N // tn),
        in_specs=[pl.BlockSpec((R, D), lambda l, j: (0, 0)),
                  pl.BlockSpec((1, D, tn), lambda l, j: (l, 0, j)),
                  pl.BlockSpec((1, 1, tn), lambda l, j: (l, 0, j))],
        out_specs=pl.BlockSpec((1, R, tn), lambda l, j: (l, 0, j)),
        out_shape=jax.ShapeDtypeStruct((L, R, N), F32),
        compiler_params=_params(("arbitrary", "arbitrary")),
        name="adaln",
    )(cc, w_mod, b_mod.reshape(L, 1, N))


def _modnorm(x, g, shift, scale):
    ms = jnp.mean(x * x, axis=-1, keepdims=True)
    return x * lax.rsqrt(ms + EPS) * (g * (1.0 + scale)) + shift


_SEGS = (("mqk", 1024, BF16, "plain"), ("mv", 512, BF16, "plain"), ("mo", 512, BF16, "plain"),
         ("mg", 128, F32, "plain"), ("wq", 512, BF16, "ropeq"), ("wk", 256, BF16, "rope"),
         ("wv", 256, BF16, "plain"), ("dq", 512, BF16, "ropeq"), ("dk", 512, BF16, "rope"),
         ("dv", 512, BF16, "plain"), ("gate", 3072, BF16, "plain"))
_EXT_WIDTH = sum(s[1] for s in _SEGS)
_COL_CHUNK = 512


def _inproj_kernel(x_ref, mod_ref, g_ref, cos_ref, sa_ref, sb_ref, w_ref, *out_refs, d, use_rope):
    x = x_ref[...]
    mod = mod_ref[0]
    h = _modnorm(x, g_ref[...], mod[:, 0:d], mod[:, d:2 * d]).astype(BF16)
    if use_rope:
        cos, sa, sb = cos_ref[...], sa_ref[...], sb_ref[...]
    off = 0
    for (name, width, dt, kind), o_ref in zip(_SEGS, out_refs):
        for c0 in range(0, width, _COL_CHUNK):
            cw = min(_COL_CHUNK, width - c0)
            acc = _dot(h, w_ref[:, off + c0:off + c0 + cw])
            if kind != "plain":
                tiles = []
                for t in range(cw // LANES):
                    a = acc[:, t * LANES:(t + 1) * LANES]
                    if use_rope:
                        a = (a * cos + pltpu.roll(a, LANES - 16, 1) * sa + pltpu.roll(a, 16, 1) * sb)
                    if kind == "ropeq":
                        a = a * _Q_SCALE
                    tiles.append(a)
                acc = jnp.concatenate(tiles, axis=1) if len(tiles) > 1 else tiles[0]
            o_ref[:, c0:c0 + cw] = acc.astype(dt)
        off += width


def _inproj(x2, mod, g, rope, w_ext, *, tm, rows_per_mod, seq, use_rope):
    N, D = x2.shape
    nt = seq // tm
    kern = functools.partial(_inproj_kernel, d=D, use_rope=use_rope)
    rope_spec = pl.BlockSpec((tm, LANES), lambda i: (i % nt, 0))
    out_shape = [jax.ShapeDtypeStruct((N, w), dt) for _, w, dt, _ in _SEGS]
    out_specs = [pl.BlockSpec((tm, w), lambda i: (i, 0)) for _, w, _, _ in _SEGS]
    return pl.pallas_call(
        kern,
        grid=(N // tm,),
        in_specs=[pl.BlockSpec((tm, D), lambda i: (i, 0)),
                  pl.BlockSpec((1, 1, 6 * D), lambda i: ((i * tm) // rows_per_mod, 0, 0)),
                  pl.BlockSpec((1, D), lambda i: (0, 0)),
                  rope_spec, rope_spec, rope_spec,
                  _resident((D, _EXT_WIDTH), lambda i: (0, 0))],
        out_specs=out_specs,
        out_shape=out_shape,
        compiler_params=_params(("arbitrary",)),
        name="inproj_lat" if use_rope else "inproj_ctx",
    )(x2, mod, g, *rope, w_ext)


def _extend_w_in(w):
    D = w.shape[0]
    o = np.cumsum([0, 512, 512, 512, 512, 16, 512, 128, 128, 512, 512, 512, 3072])
    p = [w[:, o[i]:o[i + 1]] for i in range(12)]
    dup = lambda a: jnp.concatenate([a[:, 0:64], a[:, 0:64], a[:, 64:128], a[:, 64:128]], axis=1)
    mg = jnp.concatenate([p[4], jnp.zeros((D, LANES - 16), w.dtype)], axis=1)
    ext = jnp.concatenate([p[0], p[1], p[2], p[3], mg, p[5], dup(p[6]), dup(p[7]), p[8], p[9], p[10], p[11]], axis=1)
    return ext.astype(BF16)


def _rope_tables(n_tokens):
    rows = n_tokens // GRID_W
    r, col = jnp.meshgrid(jnp.arange(rows), jnp.arange(GRID_W), indexing="ij")
    half = ROPE_DIM // 2
    inv = ROPE_BASE ** (-jnp.arange(0, half, 2, dtype=F32) / half)
    ang_r = r.reshape(-1, 1).astype(F32) * inv
    ang_c = col.reshape(-1, 1).astype(F32) * inv
    ang = jnp.concatenate([ang_r, ang_r, ang_c, ang_c], axis=-1)
    cos, sin = jnp.cos(ang), jnp.sin(ang)
    cos2 = jnp.concatenate([cos, cos], axis=-1)
    sin2 = jnp.concatenate([sin, sin], axis=-1)
    first = (jnp.arange(LANES) % 32) < 16
    sin_a = jnp.where(first, -sin2, 0.0)
    sin_b = jnp.where(first, 0.0, sin2)
    return cos2, sin_a, sin_b


def _scan_lanes(x, op, fill, reverse):
    lane = lax.broadcasted_iota(jnp.int32, x.shape, 1)
    d = 1
    while d < LANES:
        if reverse:
            shifted = jnp.where(lane < LANES - d, pltpu.roll(x, LANES - d, 1), fill)
        else:
            shifted = jnp.where(lane >= d, pltpu.roll(x, d, 1), fill)
        x = op(x, shifted)
        d *= 2
    return x


def _gates_kernel(x_ref, bias_ref, o_ref, *, rows):
    x = x_ref[0] + bias_ref[...]
    lane = lax.broadcasted_iota(jnp.int32, (rows, LANES), 1)
    for d, reverse in enumerate((False, True)):
        i_pre = x[(2 * d) * rows:(2 * d + 1) * rows]
        f_pre = x[(2 * d + 1) * rows:(2 * d + 2) * rows]
        log_f = jax.nn.log_sigmoid(f_pre)
        b = _scan_lanes(log_f, jnp.add, 0.0, reverse)
        a = i_pre - b
        pm = _scan_lanes(a, jnp.maximum, -jnp.inf, reverse)
        last = 0 if reverse else LANES - 1
        b_end = jnp.sum(jnp.where(lane == last, b, 0.0), axis=1, keepdims=True)
        a_max = jnp.max(a, axis=1, keepdims=True)
        o_ref[0, 5 * d + 0] = b
        o_ref[0, 5 * d + 1] = a
        o_ref[0, 5 * d + 2] = pm
        o_ref[0, 5 * d + 3] = jnp.broadcast_to(b_end, (rows, LANES))
        o_ref[0, 5 * d + 4] = jnp.broadcast_to(a_max, (rows, LANES))


def _gates(mg, b_gate, B, T):
    nc = T // CHUNK
    rows = M_HEADS * nc
    g = mg[:, :4 * M_HEADS].reshape(B, T, 4 * M_HEADS)
    gt = jnp.transpose(g, (0, 2, 1)).reshape(B, 4 * rows, LANES)
    bias = jnp.repeat(b_gate.astype(F32), nc).reshape(4 * rows, 1)
    return pl.pallas_call(
        functools.partial(_gates_kernel, rows=rows),
        grid=(B,),
        in_specs=[pl.BlockSpec((1, 4 * rows, LANES), lambda b: (b, 0, 0)),
                  pl.BlockSpec((4 * rows, 1), lambda b: (0, 0))],
        out_specs=pl.BlockSpec((1, 10, rows, LANES), lambda b: (b, 0, 0, 0)),
        out_shape=jax.ShapeDtypeStruct((B, 10, rows, LANES), F32),
        compiler_params=_params(("arbitrary",)),
        name="mlstm_gates",
    )(gt, bias)


_CONV_TILE = 256


def _mlstm_kernel(qc_ref, kc_ref, vc_ref, oc_ref, pc_ref, ql_ref, kl_ref, vl_ref, ol_ref, pl_ref,
                  cwq_ref, cwk_ref, nrm_ref, *rest, tc, tl, need_ctx):
    if need_ctx:
        yl_ref, yc_ref = rest[0], rest[1]
        scr = rest[2:]
    else:
        yl_ref, yc_ref = rest[0], None
        scr = rest[1:]
    xpad, qs, ks, hf, hb, cx, mst = scr
    head = pl.program_id(1)
    tot = tc + tl

    def conv_stream(u_ref, w_ref, dst, dst_off, ts, scale):
        xpad[0:8, :] = jnp.zeros((8, LANES), F32)
        xpad[8:8 + ts, :] = u_ref[0].astype(F32)
        xpad[8 + ts:16 + ts, :] = jnp.zeros((8, LANES), F32)
        w = w_ref[...]
        w0, w1, w2 = w[0:1], w[1:2], w[2:3]
        tile = min(_CONV_TILE, ts)

        def body(i, _):
            r0 = pl.multiple_of(i * tile, 8)
            win = xpad[pl.ds(r0, tile + 16), :]
            prev = pltpu.roll(win, 1, 0)[8:8 + tile]
            nxt = pltpu.roll(win, tile + 15, 0)[8:8 + tile]
            y = prev * w0 + win[8:8 + tile] * w1 + nxt * w2
            y = y * jax.nn.sigmoid(y) * scale
            dst[pl.ds(pl.multiple_of(dst_off + r0, 8), tile), :] = y.astype(BF16)
            return 0

        lax.fori_loop(0, ts // tile, body, 0)

    kscale = M_DIM ** -0.5
    conv_stream(qc_ref, cwq_ref, qs, 0, tc, 1.0)
    conv_stream(kc_ref, cwk_ref, ks, 0, tc, kscale)
    conv_stream(ql_ref, cwq_ref, qs, tc, tl, 1.0)
    conv_stream(kl_ref, cwk_ref, ks, tc, tl, kscale)

    cx[...] = jnp.zeros(cx.shape, F32)
    mst[...] = jnp.zeros(mst.shape, F32)

    row_i = lax.broadcasted_iota(jnp.int32, (CHUNK, CHUNK), 0)
    col_i = lax.broadcasted_iota(jnp.int32, (CHUNK, CHUNK), 1)
    eye = row_i == col_i
    masks = (col_i <= row_i, col_i >= row_i)
    e0_cols = jnp.where(col_i == 0, 1.0, 0.0).astype(BF16)
    e0_rows = jnp.where(row_i == 0, 1.0, 0.0).astype(F32)

    def colify(row):
        return jnp.sum(jnp.where(eye, row, 0.0), axis=1, keepdims=True)

    def chunk_step(d, c, p_ref, v_ref, nc, base, want_h):
        r = head * nc + c
        plane = lambda j: p_ref[0, 5 * d + j, pl.ds(r, 1), :]
        b_row, a_row, pm_row, bend_row, amax_row = (plane(j) for j in range(5))
        m_prev = mst[d:d + 1, :]
        m_end = jnp.maximum(m_prev, amax_row)
        w_end = jnp.exp(a_row - m_end)
        decay = jnp.exp(m_prev - m_end)
        row0 = pl.multiple_of(base + c * CHUNK, CHUNK)
        q = qs[pl.ds(row0, CHUNK), :]
        k = ks[pl.ds(row0, CHUNK), :]
        v = v_ref[0, pl.ds(pl.multiple_of(c * CHUNK, CHUNK), CHUNK), :]
        c_old = cx[d]
        if want_h:
            m_col = jnp.maximum(m_prev[:, 0:1], colify(pm_row))
            b_col = colify(b_row)
            w_intra = jnp.exp(jnp.where(masks[d], a_row - m_col, -jnp.inf))
            w_inter = jnp.exp(m_prev[:, 0:1] - m_col)
            floor = jnp.exp(-(b_col + m_col))
            s = _dot_nt(q, k) * w_intra
            vx = jnp.concatenate([v, e0_cols], axis=1)
            tot_ = _dot(s.astype(BF16), vx) + w_inter * _dot_nt(q, c_old.astype(BF16))
            num = tot_[:, 0:M_DIM]
            den = tot_[:, M_DIM:M_DIM + 1]
            hval = num / jnp.maximum(jnp.abs(den), floor)
            dst = hf if d == 0 else hb
            dst[pl.ds(row0, CHUNK), :] = hval
        vxt = jnp.concatenate([v.astype(F32).T, e0_rows], axis=0)
        cx[d] = decay * c_old + _dot((vxt * w_end).astype(BF16), k)
        mst[d:d + 1, :] = bend_row + m_end

    ncc, ncl = tc // CHUNK, tl // CHUNK
    for c in range(ncc):
        chunk_step(0, c, pc_ref, vc_ref, ncc, 0, need_ctx)
        chunk_step(1, ncc - 1 - c, pc_ref, vc_ref, ncc, 0, need_ctx)

    def lat_body(c, _):
        chunk_step(0, c, pl_ref, vl_ref, ncl, tc, True)
        chunk_step(1, ncl - 1 - c, pl_ref, vl_ref, ncl, tc, True)
        return 0

    lax.fori_loop(0, ncl, lat_body, 0, unroll=2)

    nrm = nrm_ref[...]

    def finish(o_ref, y_ref, base, ts):
        def body(i, _):
            r0 = pl.multiple_of(i * CHUNK, CHUNK)
            hsum = hf[pl.ds(base + r0, CHUNK), :] + hb[pl.ds(base + r0, CHUNK), :]
            y = hsum * jax.nn.sigmoid(o_ref[0, pl.ds(r0, CHUNK), :].astype(F32))
            ms = jnp.mean(y * y, axis=-1, keepdims=True)
            y_ref[0, pl.ds(r0, CHUNK), :] = (y * lax.rsqrt(ms + EPS) * nrm).astype(BF16)
            return 0

        lax.fori_loop(0, ts // CHUNK, body, 0, unroll=min(4, ts // CHUNK))

    finish(ol_ref, yl_ref, tc, tl)
    if need_ctx:
        finish(oc_ref, yc_ref, 0, tc)


def _mlstm(ctx_p, lat_p, gates_c, gates_l, conv_w, m_norm, B, tc, tl, need_ctx):
    r3 = lambda a, t: a.reshape(B, t, a.shape[-1])
    qk_c, v_c, o_c = r3(ctx_p["mqk"], tc), r3(ctx_p["mv"], tc), r3(ctx_p["mo"], tc)
    qk_l, v_l, o_l = r3(lat_p["mqk"], tl), r3(lat_p["mv"], tl), r3(lat_p["mo"], tl)
    H = M_HEADS
    head_blk = lambda t, off: pl.BlockSpec((1, t, LANES), lambda b, h: (b, 0, h + off))
    gate_blk = lambda g: pl.BlockSpec((1,) + g.shape[1:], lambda b, h: (b, 0, 0, 0))
    in_specs = [head_blk(tc, 0), head_blk(tc, H), head_blk(tc, 0), head_blk(tc, 0), gate_blk(gates_c),
                head_blk(tl, 0), head_blk(tl, H), head_blk(tl, 0), head_blk(tl, 0), gate_blk(gates_l),
                pl.BlockSpec((3, LANES), lambda b, h: (0, h)),
                pl.BlockSpec((3, LANES), lambda b, h: (0, h + H)),
                pl.BlockSpec((1, LANES), lambda b, h: (0, h))]
    out_shape = [jax.ShapeDtypeStruct((B, tl, BRANCH), BF16)]
    out_specs = [head_blk(tl, 0)]
    if need_ctx:
        out_shape.append(jax.ShapeDtypeStruct((B, tc, BRANCH), BF16))
        out_specs.append(head_blk(tc, 0))
    tot = tc + tl
    scratch = [pltpu.VMEM((max(tc, tl) + 16, LANES), F32),
               pltpu.VMEM((tot, LANES), BF16), pltpu.VMEM((tot, LANES), BF16),
               pltpu.VMEM((tot, LANES), F32), pltpu.VMEM((tot, LANES), F32),
               pltpu.VMEM((2, 2 * LANES, LANES), F32), pltpu.VMEM((8, LANES), F32)]
    outs = pl.pallas_call(
        functools.partial(_mlstm_kernel, tc=tc, tl=tl, need_ctx=need_ctx),
        grid=(B, H),
        in_specs=in_specs,
        out_specs=out_specs,
        out_shape=out_shape,
        scratch_shapes=scratch,
        compiler_params=_params(("arbitrary", "arbitrary")),
        name="mlstm",
    )(qk_c, qk_c, v_c, o_c, gates_c, qk_l, qk_l, v_l, o_l, gates_l,
      conv_w, conv_w, m_norm.reshape(1, -1))
    ya = outs[0].reshape(B * tl, BRANCH)
    ya_c = outs[1].reshape(B * tc, BRANCH) if need_ctx else None
    return ya, ya_c


_WB = 128


def _win_kernel(sink_ref, q_ref, *refs, has_latent, nb):
    if has_latent:
        kp_ref, kn0_ref, kn_ref, vp_ref, vn0_ref, vn_ref, kx_ref, vx_ref, o_ref = refs
    else:
        kx_ref, vx_ref, o_ref = refs
    n = pl.program_id(1)
    rep = W_HEADS // W_KV_HEADS
    rows = rep * _WB
    lane = lax.broadcasted_iota(jnp.int32, (_WB, LANES), 1)
    left = lane < W_DIM
    row_i = lax.broadcasted_iota(jnp.int32, (rows, LANES), 0)
    qq = jnp.bitwise_and(row_i, _WB - 1)
    kk = lax.broadcasted_iota(jnp.int32, (rows, LANES), 1)
    zero = jnp.zeros((_WB, LANES), BF16)
    for g in range(W_KV_HEADS):
        pieces = []
        for t in range(rep // 2):
            qt = q_ref[0, :, (g * rep // 2 + t) * LANES:(g * rep // 2 + t + 1) * LANES]
            pieces += [jnp.where(left, qt, zero), jnp.where(left, zero, qt)]
        lhs = jnp.concatenate(pieces, axis=0)
        sink_col = jnp.concatenate([jnp.full((_WB, 1), sink_ref[g * rep + r] * LOG2E, F32) for r in range(rep)], axis=0)
        gsl = slice(g * LANES, (g + 1) * LANES)
        logits = [_dot_nt(lhs, kx_ref[0, :, gsl])]
        vals = [vx_ref[0, :, gsl]]
        if has_latent:
            s_prev = _dot_nt(lhs, kp_ref[0, :, gsl])
            s_prev = jnp.where(jnp.logical_and(kk >= qq, n > 0), s_prev, -jnp.inf)
            s_next = _dot_nt(lhs, kn_ref[0, :, gsl])
            s_next = jnp.where(jnp.logical_and(kk <= qq, n < nb - 1), s_next, -jnp.inf)
            logits += [s_prev, _dot_nt(lhs, kn0_ref[0, :, gsl]), s_next]
            vals += [vp_ref[0, :, gsl], vn0_ref[0, :, gsl], vn_ref[0, :, gsl]]
        m_part = None
        for s in logits:
            for t in range(s.shape[1] // LANES):
                st = s[:, t * LANES:(t + 1) * LANES]
                m_part = st if m_part is None else jnp.maximum(m_part, st)
        m = jnp.maximum(sink_col, jnp.max(m_part, axis=-1, keepdims=True))
        acc = jnp.zeros((rows, 2 * LANES), F32)
        for s, v in zip(logits, vals):
            vx = jnp.concatenate([v, jnp.ones(v.shape, BF16)], axis=1)
            acc = acc + _dot(jnp.exp2(s - m).astype(BF16), vx)
        o = acc[:, 0:LANES] / (acc[:, LANES:2 * LANES] + jnp.exp2(sink_col - m))
        for t in range(rep // 2):
            tile = jnp.where(left, o[(2 * t) * _WB:(2 * t + 1) * _WB], o[(2 * t + 1) * _WB:(2 * t + 2) * _WB])
            c0 = (g * rep // 2 + t) * LANES
            o_ref[0, :, c0:c0 + LANES] = tile.astype(BF16)


def _window_attention(q, kx, vx, sink, B, tq, tcx, lat_kv=None):
    q3 = q.reshape(B, tq, BRANCH)
    kx3, vx3 = kx.reshape(B, tcx, 2 * LANES), vx.reshape(B, tcx, 2 * LANES)
    nb = tq // _WB
    has_latent = lat_kv is not None
    in_specs = [pl.BlockSpec(memory_space=pltpu.SMEM),
                pl.BlockSpec((1, _WB, BRANCH), lambda b, n: (b, n, 0))]
    args = [sink.astype(F32), q3]
    if has_latent:
        k3, v3 = (a.reshape(B, tq, 2 * LANES) for a in lat_kv)
        blk = lambda f: pl.BlockSpec((1, _WB, 2 * LANES), f)
        prev = lambda b, n: (b, jnp.maximum(n - 1, 0), 0)
        cur = lambda b, n: (b, n, 0)
        nxt = lambda b, n: (b, jnp.minimum(n + 1, nb - 1), 0)
        in_specs += [blk(prev), blk(cur), blk(nxt), blk(prev), blk(cur), blk(nxt)]
        args += [k3, k3, k3, v3, v3, v3]
    ctx_blk = pl.BlockSpec((1, tcx, 2 * LANES), lambda b, n: (b, 0, 0))
    in_specs += [ctx_blk, ctx_blk]
    args += [kx3, vx3]
    out = pl.pallas_call(
        functools.partial(_win_kernel, has_latent=has_latent, nb=nb),
        grid=(B, nb),
        in_specs=in_specs,
        out_specs=pl.BlockSpec((1, _WB, BRANCH), lambda b, n: (b, n, 0)),
        out_shape=jax.ShapeDtypeStruct((B, tq, BRANCH), BF16),
        compiler_params=_params(("arbitrary", "arbitrary")),
        name="win_lat" if has_latent else "win_ctx",
    )(*args)
    return out.reshape(B * tq, BRANCH)


def _diff_kernel(lam_ref, q_ref, *refs, tq, tk, tcx, tlat, lam_init):
    if tlat:
        kx_ref, vx_ref, kl_ref, vl_ref, nrm_ref, o_ref = refs
    else:
        kx_ref, vx_ref, nrm_ref, o_ref = refs
        kl_ref = vl_ref = None
    q = q_ref[0]
    lane = lax.broadcasted_iota(jnp.int32, (tq, LANES), 1)
    left = lane < LANES // 2
    zero = jnp.zeros((tq, LANES), BF16)
    tiles = [(kx_ref, vx_ref, 0, tcx)] + [(kl_ref, vl_ref, j * tk, tk) for j in range(tlat // tk)]
    ones = jnp.ones((max(tk, tcx), LANES), BF16)

    lhs = (jnp.where(left, q, zero), jnp.where(left, zero, q))
    state = [None, None]
    for k_ref, v_ref, r0, n in tiles:
        vx = jnp.concatenate([v_ref[0, r0:r0 + n, :], ones[0:n]], axis=1)
        k = k_ref[0, r0:r0 + n, :]
        for comp in range(2):
            s = _dot_nt(lhs[comp], k)
            m_part = s[:, 0:LANES]
            for t in range(1, n // LANES):
                m_part = jnp.maximum(m_part, s[:, t * LANES:(t + 1) * LANES])
            m_j = jnp.max(m_part, axis=-1, keepdims=True)
            part = _dot(jnp.exp2(s - m_j).astype(BF16), vx)
            if state[comp] is None:
                state[comp] = (m_j, part)
            else:
                m_run, acc = state[comp]
                m_new = jnp.maximum(m_run, m_j)
                state[comp] = (m_new, acc * jnp.exp2(m_run - m_new) + part * jnp.exp2(m_j - m_new))
    outs = [acc[:, 0:LANES] / acc[:, LANES:2 * LANES] for _, acc in state]
    lv = lam_ref[...]
    s1 = jnp.sum(lv[0:1] * lv[1:2], axis=-1, keepdims=True)
    s2 = jnp.sum(lv[2:3] * lv[3:4], axis=-1, keepdims=True)
    lam = jnp.exp(s1) - jnp.exp(s2) + lam_init
    y = outs[0] - lam * outs[1]
    ms = jnp.mean(y * y, axis=-1, keepdims=True)
    o_ref[0] = (y * lax.rsqrt(ms + EPS) * nrm_ref[...] * (1.0 - lam_init)).astype(BF16)


def _diff_attention(q, kx, vx, lamv, d_norm, lam_init, B, tq_total, tcx, lat_kv=None, tq=512, tk=1024):
    q3 = q.reshape(B, tq_total, BRANCH)
    kx3, vx3 = kx.reshape(B, tcx, BRANCH), vx.reshape(B, tcx, BRANCH)
    tq = min(tq, tq_total)
    head_blk = lambda t, f: pl.BlockSpec((1, t, LANES), f)
    whole = lambda b, h, i: (b, 0, h)
    in_specs = [pl.BlockSpec((8, LANES), lambda b, h, i: (0, 0)),
                head_blk(tq, lambda b, h, i: (b, i, h)),
                head_blk(tcx, whole), head_blk(tcx, whole)]
    args = [lamv, q3, kx3, vx3]
    tlat = 0
    if lat_kv is not None:
        tlat = tq_total
        k3, v3 = (a.reshape(B, tlat, BRANCH) for a in lat_kv)
        in_specs += [head_blk(tlat, whole), head_blk(tlat, whole)]
        args += [k3, v3]
    in_specs.append(pl.BlockSpec((1, LANES), lambda b, h, i: (0, h)))
    args.append(d_norm.reshape(1, -1))
    tk = min(tk, max(tlat, LANES))
    out = pl.pallas_call(
        functools.partial(_diff_kernel, tq=tq, tk=tk, tcx=tcx, tlat=tlat, lam_init=lam_init),
        grid=(B, DF_HEADS, tq_total // tq),
        in_specs=in_specs,
        out_specs=head_blk(tq, lambda b, h, i: (b, i, h)),
        out_shape=jax.ShapeDtypeStruct((B, tq_total, BRANCH), BF16),
        compiler_params=_params(("arbitrary", "arbitrary", "arbitrary")),
        name="diff_lat" if tlat else "diff_ctx",
    )(*args)
    return out.reshape(B * tq_total, BRANCH)


def _merge_kernel(ya_ref, yb_ref, yc_ref, gate_ref, x_ref, mod_ref, wb_ref, wo_ref, o_ref, *, d):
    merged = None
    for i, y_ref in enumerate((ya_ref, yb_ref, yc_ref)):
        gate = jax.nn.sigmoid(gate_ref[:, i * d:(i + 1) * d].astype(F32))
        term = gate * _dot(y_ref[...], wb_ref[i])
        merged = term if merged is None else merged + term
    out = _dot(merged.astype(BF16), wo_ref[...])
    gt = mod_ref[0][:, 2 * d:3 * d]
    o_ref[...] = x_ref[...] + gt * out


def _merge(ya, yb, yc, gate, x2, mod, wb, wo, *, tm, rows_per_mod):
    N, D = x2.shape
    row = lambda w: pl.BlockSpec((tm, w), lambda i: (i, 0))
    return pl.pallas_call(
        functools.partial(_merge_kernel, d=D),
        grid=(N // tm,),
        in_specs=[row(BRANCH), row(BRANCH), row(BRANCH), row(N_BRANCH * D), row(D),
                  pl.BlockSpec((1, 1, 6 * D), lambda i: ((i * tm) // rows_per_mod, 0, 0)),
                  _resident((N_BRANCH, BRANCH, D), lambda i: (0, 0, 0)),
                  _resident((D, D), lambda i: (0, 0))],
        out_specs=row(D),
        out_shape=jax.ShapeDtypeStruct((N, D), F32),
        compiler_params=_params(("arbitrary",)),
        name="merge",
    )(ya, yb, yc, gate, x2, mod, wb, wo)


_FFN_CHUNK = 256


def _ffn_kernel(x_ref, mod_ref, g_ref, wi_ref, wo_ref, gf_ref, o_ref, *, d, hidden, final_norm):
    x = x_ref[...]
    mod = mod_ref[0]
    h = _modnorm(x, g_ref[...], mod[:, 3 * d:4 * d], mod[:, 4 * d:5 * d]).astype(BF16)
    acc = None
    for c0 in range(0, hidden, _FFN_CHUNK):
        gate = _dot(h, wi_ref[:, c0:c0 + _FFN_CHUNK])
        up = _dot(h, wi_ref[:, hidden + c0:hidden + c0 + _FFN_CHUNK])
        a = (gate * jax.nn.sigmoid(gate) * up).astype(BF16)
        part = _dot(a, wo_ref[c0:c0 + _FFN_CHUNK, :])
        acc = part if acc is None else acc + part
    y = x + mod[:, 5 * d:6 * d] * acc
    if final_norm:
        ms = jnp.mean(y * y, axis=-1, keepdims=True)
        y = y * lax.rsqrt(ms + EPS) * gf_ref[...]
    o_ref[...] = y


def _ffn(x2, mod, g, wi, wo, g_final, *, tm, rows_per_mod, final_norm):
    N, D = x2.shape
    hidden = wo.shape[0]
    return pl.pallas_call(
        functools.partial(_ffn_kernel, d=D, hidden=hidden, final_norm=final_norm),
        grid=(N // tm,),
        in_specs=[pl.BlockSpec((tm, D), lambda i: (i, 0)),
                  pl.BlockSpec((1, 1, 6 * D), lambda i: ((i * tm) // rows_per_mod, 0, 0)),
                  pl.BlockSpec((1, D), lambda i: (0, 0)),
                  _resident((D, 2 * hidden), lambda i: (0, 0)),
                  _resident((hidden, D), lambda i: (0, 0)),
                  pl.BlockSpec((1, D), lambda i: (0, 0))],
        out_specs=pl.BlockSpec((tm, D), lambda i: (i, 0)),
        out_shape=jax.ShapeDtypeStruct((N, D), F32),
        compiler_params=_params(("arbitrary",)),
        name="ffn",
    )(x2, mod, g, wi, wo, g_final)


def _row_tile(n, want=512):
    t = want
    while n % t:
        t //= 2
    return t


def kernel(x, c, ctx, c_ctx, w_mod, b_mod, g_mix, g_ffn, w_in, b_gate, conv_w, m_norm, sink,
           lam_q1, lam_k1, lam_q2, lam_k2, d_norm, w_branch, w_out, w_ffn_in, w_ffn_out, g_final):
    B, T, D = x.shape
    Tc = ctx.shape[1]
    depth = w_mod.shape[0]
    assert T % CHUNK == 0 and Tc % CHUNK == 0 and T % GRID_W == 0

    n_rows = -(-(B + 1) // 16) * 16
    cc = jnp.concatenate([c, c_ctx[None, :], jnp.zeros((n_rows - B - 1, D), F32)], axis=0)
    mods = _adaln(cc, w_mod, b_mod)

    rope = _rope_tables(T)
    no_rope = tuple(jnp.zeros((Tc, LANES), F32) for _ in range(3))
    tm = _row_tile(T)
    tmc = _row_tile(Tc)
    names = [s[0] for s in _SEGS]

    xl = x.reshape(B * T, D)
    xc = ctx.reshape(B * Tc, D)
    for l in range(depth):
        need_ctx = l < depth - 1
        lam_init = 0.8 - 0.6 * math.exp(-0.3 * l)
        mod_l = mods[l, :B].reshape(B, 1, 6 * D)
        mod_c = mods[l, B:B + 1].reshape(1, 1, 6 * D)
        w_ext = _extend_w_in(w_in[l])
        g_mix_l = g_mix[l].reshape(1, D)
        g_ffn_l = g_ffn[l].reshape(1, D)
        lamv = jnp.zeros((8, LANES), F32)
        for i, v in enumerate((lam_q1[l], lam_k1[l], lam_q2[l], lam_k2[l])):
            lamv = lamv.at[i, :v.shape[0]].set(v.astype(F32))

        pl_ = dict(zip(names, _inproj(xl, mod_l, g_mix_l, rope, w_ext, tm=tm, rows_per_mod=T, seq=T, use_rope=True)))
        pc_ = dict(zip(names, _inproj(xc, mod_c, g_mix_l, no_rope, w_ext, tm=tmc, rows_per_mod=B * Tc, seq=Tc,
                                      use_rope=False)))

        gates_l = _gates(pl_["mg"], b_gate[l], B, T)
        gates_c = _gates(pc_["mg"], b_gate[l], B, Tc)
        ya, ya_c = _mlstm(pc_, pl_, gates_c, gates_l, conv_w[l], m_norm[l], B, Tc, T, need_ctx)
        yb = _window_attention(pl_["wq"], pc_["wk"], pc_["wv"], sink[l], B, T, Tc, lat_kv=(pl_["wk"], pl_["wv"]))
        yc = _diff_attention(pl_["dq"], pc_["dk"], pc_["dv"], lamv, d_norm[l], lam_init, B, T, Tc,
                             lat_kv=(pl_["dk"], pl_["dv"]))
        wb = w_branch[l].astype(BF16)
        wo = w_out[l].astype(BF16)
        wfi = w_ffn_in[l].astype(BF16)
        wfo = w_ffn_out[l].astype(BF16)
        gfin = g_final.reshape(1, D)
        last = l == depth - 1
        xl = _merge(ya, yb, yc, pl_["gate"], xl, mod_l, wb, wo, tm=tm, rows_per_mod=T)
        xl = _ffn(xl, mod_l, g_ffn_l, wfi, wfo, gfin, tm=tm, rows_per_mod=T, final_norm=last)
        if need_ctx:
            yb_c = _window_attention(pc_["wq"], pc_["wk"], pc_["wv"], sink[l], B, Tc, Tc)
            yc_c = _diff_attention(pc_["dq"], pc_["dk"], pc_["dv"], lamv, d_norm[l], lam_init, B, Tc, Tc)
            xc = _merge(ya_c, yb_c, yc_c, pc_["gate"], xc, mod_c, wb, wo, tm=tmc, rows_per_mod=B * Tc)
            xc = _ffn(xc, mod_c, g_ffn_l, wfi, wfo, gfin, tm=tmc, rows_per_mod=B * Tc, final_norm=False)
    return xl.reshape(B, T, D)
```

```python
import functools
import math

import jax
import jax.numpy as jnp
import numpy as np
from jax import lax
from jax.experimental import pallas as pl
from jax.experimental.pallas import tpu as pltpu

F32 = jnp.float32
BF16 = jnp.bfloat16

GRID_W = 64
BRANCH = 512
N_BRANCH = 3
M_HEADS = 4
M_DIM = 128
W_HEADS = 8
W_KV_HEADS = 2
W_DIM = 64
DF_HEADS = 4
ROPE_DIM = 64
ROPE_BASE = 10000.0
EPS = 1e-6
LANES = 128
CHUNK = 128
VMEM_LIMIT = 56 * 1024 * 1024

_NT = (((1,), (1,)), ((), ()))
LOG2E = math.log2(math.e)
_Q_SCALE = W_DIM ** -0.5 * LOG2E


def _dot(a, b):
    return jnp.dot(a, b, preferred_element_type=F32)


def _dot_nt(a, b):
    return lax.dot_general(a, b, _NT, preferred_element_type=F32)


def _params(sem):
    return pltpu.CompilerParams(dimension_semantics=sem, vmem_limit_bytes=VMEM_LIMIT)


def _resident(shape, index_map):
    return pl.BlockSpec(shape, index_map, pipeline_mode=pl.Buffered(1))


def _adaln_kernel(c_ref, w_ref, b_ref, o_ref):
    c = c_ref[...]
    s = (c * jax.nn.sigmoid(c)).astype(BF16)
    o_ref[0] = _dot(s, w_ref[0].astype(BF16)) + b_ref[0]


def _adaln(cc, w_mod, b_mod):
    L, D, N = w_mod.shape
    R = cc.shape[0]
    tn = 1536
    return pl.pallas_call(
        _adaln_kernel,
        grid=(L, N // tn),
        in_specs=[pl.BlockSpec((R, D), lambda l, j: (0, 0)),
                  pl.BlockSpec((1, D, tn), lambda l, j: (l, 0, j)),
                  pl.BlockSpec((1, 1, tn), lambda l, j: (l, 0, j))],
        out_specs=pl.BlockSpec((1, R, tn), lambda l, j: (l, 0, j)),
        out_shape=jax.ShapeDtypeStruct((L, R, N), F32),
        compiler_params=_params(("arbitrary", "arbitrary")),
        name="adaln",
    )(cc, w_mod, b_mod.reshape(L, 1, N))


def _modnorm(x, g, shift, scale):
    ms = jnp.mean(x * x, axis=-1, keepdims=True)
    return x * lax.rsqrt(ms + EPS) * (g * (1.0 + scale)) + shift


_SEGS = (("mqk", 1024, BF16, "plain"), ("mv", 512, BF16, "plain"), ("mo", 512, BF16, "plain"),
         ("mg", 128, F32, "plain"), ("wq", 512, BF16, "ropeq"), ("wk", 256, BF16, "rope"),
         ("wv", 256, BF16, "plain"), ("dq", 512, BF16, "ropeq"), ("dk", 512, BF16, "rope"),
         ("dv", 512, BF16, "plain"), ("gate", 3072, BF16, "plain"))
_EXT_WIDTH = sum(s[1] for s in _SEGS)
_COL_CHUNK = 512


def _inproj_kernel(x_ref, mod_ref, g_ref, cos_ref, sa_ref, sb_ref, w_ref, *out_refs, d, use_rope):
    x = x_ref[...]
    mod = mod_ref[0]
    h = _modnorm(x, g_ref[...], mod[:, 0:d], mod[:, d:2 * d]).astype(BF16)
    if use_rope:
        cos, sa, sb = cos_ref[...], sa_ref[...], sb_ref[...]
    off = 0
    for (name, width, dt, kind), o_ref in zip(_SEGS, out_refs):
        for c0 in range(0, width, _COL_CHUNK):
            cw = min(_COL_CHUNK, width - c0)
            acc = _dot(h, w_ref[:, off + c0:off + c0 + cw])
            if kind != "plain":
                tiles = []
                for t in range(cw // LANES):
                    a = acc[:, t * LANES:(t + 1) * LANES]
                    if use_rope:
                        a = (a * cos + pltpu.roll(a, LANES - 16, 1) * sa + pltpu.roll(a, 16, 1) * sb)
                    if kind == "ropeq":
                        a = a * _Q_SCALE
                    tiles.append(a)
                acc = jnp.concatenate(tiles, axis=1) if len(tiles) > 1 else tiles[0]
            o_ref[:, c0:c0 + cw] = acc.astype(dt)
        off += width


def _inproj(x2, mod, g, rope, w_ext, *, tm, rows_per_mod, seq, use_rope):
    N, D = x2.shape
    nt = seq // tm
    kern = functools.partial(_inproj_kernel, d=D, use_rope=use_rope)
    rope_spec = pl.BlockSpec((tm, LANES), lambda i: (i % nt, 0))
    out_shape = [jax.ShapeDtypeStruct((N, w), dt) for _, w, dt, _ in _SEGS]
    out_specs = [pl.BlockSpec((tm, w), lambda i: (i, 0)) for _, w, _, _ in _SEGS]
    return pl.pallas_call(
        kern,
        grid=(N // tm,),
        in_specs=[pl.BlockSpec((tm, D), lambda i: (i, 0)),
                  pl.BlockSpec((1, 1, 6 * D), lambda i: ((i * tm) // rows_per_mod, 0, 0)),
                  pl.BlockSpec((1, D), lambda i: (0, 0)),
                  rope_spec, rope_spec, rope_spec,
                  _resident((D, _EXT_WIDTH), lambda i: (0, 0))],
        out_specs=out_specs,
        out_shape=out_shape,
        compiler_params=_params(("arbitrary",)),
        name="inproj_lat" if use_rope else "inproj_ctx",
    )(x2, mod, g, *rope, w_ext)


def _extend_w_in(w):
    D = w.shape[0]
    o = np.cumsum([0, 512, 512, 512, 512, 16, 512, 128, 128, 512, 512, 512, 3072])
    p = [w[:, o[i]:o[i + 1]] for i in range(12)]
    dup = lambda a: jnp.concatenate([a[:, 0:64], a[:, 0:64], a[:, 64:128], a[:, 64:128]], axis=1)
    mg = jnp.concatenate([p[4], jnp.zeros((D, LANES - 16), w.dtype)], axis=1)
    ext = jnp.concatenate([p[0], p[1], p[2], p[3], mg, p[5], dup(p[6]), dup(p[7]), p[8], p[9], p[10], p[11]], axis=1)
    return ext.astype(BF16)


def _rope_tables(n_tokens):
    rows = n_tokens // GRID_W
    r, col = jnp.meshgrid(jnp.arange(rows), jnp.arange(GRID_W), indexing="ij")
    half = ROPE_DIM // 2
    inv = ROPE_BASE ** (-jnp.arange(0, half, 2, dtype=F32) / half)
    ang_r = r.reshape(-1, 1).astype(F32) * inv
    ang_c = col.reshape(-1, 1).astype(F32) * inv
    ang = jnp.concatenate([ang_r, ang_r, ang_c, ang_c], axis=-1)
    cos, sin = jnp.cos(ang), jnp.sin(ang)
    cos2 = jnp.concatenate([cos, cos], axis=-1)
    sin2 = jnp.concatenate([sin, sin], axis=-1)
    first = (jnp.arange(LANES) % 32) < 16
    sin_a = jnp.where(first, -sin2, 0.0)
    sin_b = jnp.where(first, 0.0, sin2)
    return cos2, sin_a, sin_b


def _scan_lanes(x, op, fill, reverse):
    lane = lax.broadcasted_iota(jnp.int32, x.shape, 1)
    d = 1
    while d < LANES:
        if reverse:
            shifted = jnp.where(lane < LANES - d, pltpu.roll(x, LANES - d, 1), fill)
        else:
            shifted = jnp.where(lane >= d, pltpu.roll(x, d, 1), fill)
        x = op(x, shifted)
        d *= 2
    return x


def _gates_kernel(x_ref, bias_ref, o_ref, *, rows):
    x = x_ref[0] + bias_ref[...]
    lane = lax.broadcasted_iota(jnp.int32, (rows, LANES), 1)
    for d, reverse in enumerate((False, True)):
        i_pre = x[(2 * d) * rows:(2 * d + 1) * rows]
        f_pre = x[(2 * d + 1) * rows:(2 * d + 2) * rows]
        log_f = jax.nn.log_sigmoid(f_pre)
        b = _scan_lanes(log_f, jnp.add, 0.0, reverse)
        a = i_pre - b
        pm = _scan_lanes(a, jnp.maximum, -jnp.inf, reverse)
        last = 0 if reverse else LANES - 1
        b_end = jnp.sum(jnp.where(lane == last, b, 0.0), axis=1, keepdims=True)
        a_max = jnp.max(a, axis=1, keepdims=True)
        o_ref[0, 5 * d + 0] = b
        o_ref[0, 5 * d + 1] = a
        o_ref[0, 5 * d + 2] = pm
        o_ref[0, 5 * d + 3] = jnp.broadcast_to(b_end, (rows, LANES))
        o_ref[0, 5 * d + 4] = jnp.broadcast_to(a_max, (rows, LANES))


def _gates(mg, b_gate, B, T):
    nc = T // CHUNK
    rows = M_HEADS * nc
    g = mg[:, :4 * M_HEADS].reshape(B, T, 4 * M_HEADS)
    gt = jnp.transpose(g, (0, 2, 1)).reshape(B, 4 * rows, LANES)
    bias = jnp.repeat(b_gate.astype(F32), nc).reshape(4 * rows, 1)
    return pl.pallas_call(
        functools.partial(_gates_kernel, rows=rows),
        grid=(B,),
        in_specs=[pl.BlockSpec((1, 4 * rows, LANES), lambda b: (b, 0, 0)),
                  pl.BlockSpec((4 * rows, 1), lambda b: (0, 0))],
        out_specs=pl.BlockSpec((1, 10, rows, LANES), lambda b: (b, 0, 0, 0)),
        out_shape=jax.ShapeDtypeStruct((B, 10, rows, LANES), F32),
        compiler_params=_params(("arbitrary",)),
        name="mlstm_gates",
    )(gt, bias)


_CONV_TILE = 256


def _mlstm_kernel(qc_ref, kc_ref, vc_ref, oc_ref, pc_ref, ql_ref, kl_ref, vl_ref, ol_ref, pl_ref,
                  cwq_ref, cwk_ref, nrm_ref, *rest, tc, tl, need_ctx):
    if need_ctx:
        yl_ref, yc_ref = rest[0], rest[1]
        scr = rest[2:]
    else:
        yl_ref, yc_ref = rest[0], None
        scr = rest[1:]
    xpad, qs, ks, hf, hb, cx, mst = scr
    head = pl.program_id(1)
    tot = tc + tl

    def conv_stream(u_ref, w_ref, dst, dst_off, ts, scale):
        xpad[0:8, :] = jnp.zeros((8, LANES), F32)
        xpad[8:8 + ts, :] = u_ref[0].astype(F32)
        xpad[8 + ts:16 + ts, :] = jnp.zeros((8, LANES), F32)
        w = w_ref[...]
        w0, w1, w2 = w[0:1], w[1:2], w[2:3]
        tile = min(_CONV_TILE, ts)

        def body(i, _):
            r0 = pl.multiple_of(i * tile, 8)
            win = xpad[pl.ds(r0, tile + 16), :]
            prev = pltpu.roll(win, 1, 0)[8:8 + tile]
            nxt = pltpu.roll(win, tile + 15, 0)[8:8 + tile]
            y = prev * w0 + win[8:8 + tile] * w1 + nxt * w2
            y = y * jax.nn.sigmoid(y) * scale
            dst[pl.ds(pl.multiple_of(dst_off + r0, 8), tile), :] = y.astype(BF16)
            return 0

        lax.fori_loop(0, ts // tile, body, 0)

    kscale = M_DIM ** -0.5
    conv_stream(qc_ref, cwq_ref, qs, 0, tc, 1.0)
    conv_stream(kc_ref, cwk_ref, ks, 0, tc, kscale)
    conv_stream(ql_ref, cwq_ref, qs, tc, tl, 1.0)
    conv_stream(kl_ref, cwk_ref, ks, tc, tl, kscale)

    cx[...] = jnp.zeros(cx.shape, F32)
    mst[...] = jnp.zeros(mst.shape, F32)

    row_i = lax.broadcasted_iota(jnp.int32, (CHUNK, CHUNK), 0)
    col_i = lax.broadcasted_iota(jnp.int32, (CHUNK, CHUNK), 1)
    masks =(col_i <= row_i, col_i >= row_i)
    ones_b = jnp.ones((CHUNK, LANES), BF16)
    ones_f = jnp.ones((LANES, CHUNK), F32)

    def colify(row):
        return jnp.broadcast_to(row, (CHUNK, CHUNK)).T

    def chunk_step(d, c, p_ref, v_ref, nc, base, want_h):
        r = head * nc + c
        plane = lambda j: p_ref[0, 5 * d + j, pl.ds(r, 1), :]
        b_row, a_row, pm_row, bend_row, amax_row = (plane(j) for j in range(5))
        m_prev = mst[d:d + 1, :]
        m_end = jnp.maximum(m_prev, amax_row)
        w_end = jnp.exp(a_row - m_end)
        decay = jnp.exp(m_prev - m_end)
        row0 = pl.multiple_of(base + c * CHUNK, CHUNK)
        q = qs[pl.ds(row0, CHUNK), :]
        k = ks[pl.ds(row0, CHUNK), :]
        v = v_ref[0, pl.ds(pl.multiple_of(c * CHUNK, CHUNK), CHUNK), :]
        c_old = cx[d]
        if want_h:
            m_col = jnp.maximum(m_prev, colify(pm_row))
            w_intra = jnp.exp(jnp.where(masks[d], a_row - m_col, -jnp.inf))
            w_inter = jnp.exp(m_prev - m_col)
            floor = jnp.exp(-(colify(b_row) + m_col))
            s = _dot_nt(q, k) * w_intra
            intra = _dot(s.astype(BF16), jnp.concatenate([v, ones_b], axis=1))
            inter = _dot_nt(q, c_old.astype(BF16))
            num = intra[:, 0:M_DIM] + w_inter * inter[:, 0:M_DIM]
            den = intra[:, M_DIM:2 * M_DIM] + w_inter * inter[:, M_DIM:2 * M_DIM]
            hval = num / jnp.maximum(jnp.abs(den), floor)
            dst = hf if d == 0 else hb
            dst[pl.ds(row0, CHUNK), :] = hval
        vxt = jnp.concatenate([v.astype(F32).T, ones_f], axis=0)
        cx[d] = decay * c_old + _dot((vxt * w_end).astype(BF16), k)
        mst[d:d + 1, :] = bend_row + m_end

    ncc, ncl = tc // CHUNK, tl // CHUNK
    for c in range(ncc):
        chunk_step(0, c, pc_ref, vc_ref, ncc, 0, need_ctx)
        chunk_step(1, ncc - 1 - c, pc_ref, vc_ref, ncc, 0, need_ctx)

    def lat_body(c, _):
        chunk_step(0, c, pl_ref, vl_ref, ncl, tc, True)
        chunk_step(1, ncl - 1 - c, pl_ref, vl_ref, ncl, tc, True)
        return 0

    lax.fori_loop(0, ncl, lat_body, 0, unroll=4)

    nrm = nrm_ref[...]

    def finish(o_ref, y_ref, base, ts):
        def body(i, _):
            r0 = pl.multiple_of(i * CHUNK, CHUNK)
            hsum = hf[pl.ds(base + r0, CHUNK), :] + hb[pl.ds(base + r0, CHUNK), :]
            y = hsum * jax.nn.sigmoid(o_ref[0, pl.ds(r0, CHUNK), :].astype(F32))
            ms = jnp.mean(y * y, axis=-1, keepdims=True)
            y_ref[0, pl.ds(r0, CHUNK), :] = (y * lax.rsqrt(ms + EPS) * nrm).astype(BF16)
            return 0

        lax.fori_loop(0, ts // CHUNK, body, 0, unroll=min(4, ts // CHUNK))

    finish(ol_ref, yl_ref, tc, tl)
    if need_ctx:
        finish(oc_ref, yc_ref, 0, tc)


def _mlstm(ctx_p, lat_p, gates_c, gates_l, conv_w, m_norm, B, tc, tl, need_ctx):
    r3 = lambda a, t: a.reshape(B, t, a.shape[-1])
    qk_c, v_c, o_c = r3(ctx_p["mqk"], tc), r3(ctx_p["mv"], tc), r3(ctx_p["mo"], tc)
    qk_l, v_l, o_l = r3(lat_p["mqk"], tl), r3(lat_p["mv"], tl), r3(lat_p["mo"], tl)
    H = M_HEADS
    head_blk = lambda t, off: pl.BlockSpec((1, t, LANES), lambda b, h: (b, 0, h + off))
    gate_blk = lambda g: pl.BlockSpec((1,) + g.shape[1:], lambda b, h: (b, 0, 0, 0))
    in_specs = [head_blk(tc, 0), head_blk(tc, H), head_blk(tc, 0), head_blk(tc, 0), gate_blk(gates_c),
                head_blk(tl, 0), head_blk(tl, H), head_blk(tl, 0), head_blk(tl, 0), gate_blk(gates_l),
                pl.BlockSpec((3, LANES), lambda b, h: (0, h)),
                pl.BlockSpec((3, LANES), lambda b, h: (0, h + H)),
                pl.BlockSpec((1, LANES), lambda b, h: (0, h))]
    out_shape = [jax.ShapeDtypeStruct((B, tl, BRANCH), BF16)]
    out_specs = [head_blk(tl, 0)]
    if need_ctx:
        out_shape.append(jax.ShapeDtypeStruct((B, tc, BRANCH), BF16))
        out_specs.append(head_blk(tc, 0))
    tot = tc + tl
    scratch = [pltpu.VMEM((max(tc, tl) + 16, LANES), F32),
               pltpu.VMEM((tot, LANES), BF16), pltpu.VMEM((tot, LANES), BF16),
               pltpu.VMEM((tot, LANES), F32), pltpu.VMEM((tot, LANES), F32),
               pltpu.VMEM((2, 2 * LANES, LANES), F32), pltpu.VMEM((8, LANES), F32)]
    outs = pl.pallas_call(
        functools.partial(_mlstm_kernel, tc=tc, tl=tl, need_ctx=need_ctx),
        grid=(B, H),
        in_specs=in_specs,
        out_specs=out_specs,
        out_shape=out_shape,
        scratch_shapes=scratch,
        compiler_params=_params(("arbitrary", "arbitrary")),
        name="mlstm",
    )(qk_c, qk_c, v_c, o_c, gates_c, qk_l, qk_l, v_l, o_l, gates_l,
      conv_w, conv_w, m_norm.reshape(1, -1))
    ya = outs[0].reshape(B * tl, BRANCH)
    ya_c = outs[1].reshape(B * tc, BRANCH) if need_ctx else None
    return ya, ya_c


_WB = 128


def _win_kernel(sink_ref, q_ref, *refs, has_latent, qb, seq):
    if has_latent:
        kl_ref, vl_ref, kx_ref, vx_ref, o_ref = refs
    else:
        kx_ref, vx_ref, o_ref = refs
    n = pl.program_id(1)
    rep = W_HEADS // W_KV_HEADS
    rows = rep * _WB
    span = 3 * _WB
    lane = lax.broadcasted_iota(jnp.int32, (_WB, LANES), 1)
    left = lane < W_DIM
    zero = jnp.zeros((_WB, LANES), BF16)
    if has_latent:
        qq = jnp.bitwise_and(lax.broadcasted_iota(jnp.int32, (rows, span), 0), _WB - 1)
        dist = lax.broadcasted_iota(jnp.int32, (rows, span), 1) - qq
    for j in range(qb):
        r0 = j * _WB
        if has_latent:
            q_start = (n * qb + j) * _WB
            k_start = pl.multiple_of(jnp.clip(q_start - _WB, 0, seq - span), _WB)
            shift = k_start - q_start
            near = jnp.logical_and(dist + shift >= -_WB, dist + shift <= _WB)
        for g in range(W_KV_HEADS):
            pieces = []
            for t in range(rep // 2):
                qt = q_ref[0, r0:r0 + _WB, (g * rep // 2 + t) * LANES:(g * rep // 2 + t + 1) * LANES]
                pieces += [jnp.where(left, qt, zero), jnp.where(left, zero, qt)]
            lhs = jnp.concatenate(pieces, axis=0)
            sink_col = jnp.concatenate(
                [jnp.full((_WB, 1), sink_ref[g * rep + r] * LOG2E, F32) for r in range(rep)], axis=0)
            gsl = slice(g * LANES, (g + 1) * LANES)
            logits = [_dot_nt(lhs, kx_ref[0, :, gsl])]
            vals = [vx_ref[0, :, gsl]]
            if has_latent:
                s_win = _dot_nt(lhs, kl_ref[0, pl.ds(k_start, span), gsl])
                logits.append(jnp.where(near, s_win, -jnp.inf))
                vals.append(vl_ref[0, pl.ds(k_start, span), gsl])
            m_part = None
            for s in logits:
                for t in range(s.shape[1] // LANES):
                    st = s[:, t * LANES:(t + 1) * LANES]
                    m_part = st if m_part is None else jnp.maximum(m_part, st)
            m = jnp.maximum(sink_col, jnp.max(m_part, axis=-1, keepdims=True))
            acc = None
            for s, v in zip(logits, vals):
                vx = jnp.concatenate([v, jnp.ones(v.shape, BF16)], axis=1)
                part = _dot(jnp.exp2(s - m).astype(BF16), vx)
                acc = part if acc is None else acc + part
            o = acc[:, 0:LANES] / (acc[:, LANES:2 * LANES] + jnp.exp2(sink_col - m))
            for t in range(rep // 2):
                tile = jnp.where(left, o[(2 * t) * _WB:(2 * t + 1) * _WB], o[(2 * t + 1) * _WB:(2 * t + 2) * _WB])
                c0 = (g * rep // 2 + t) * LANES
                o_ref[0, r0:r0 + _WB, c0:c0 + LANES] = tile.astype(BF16)


def _window_attention(q, kx, vx, sink, B, tq, tcx, lat_kv=None, qb=4):
    q3 = q.reshape(B, tq, BRANCH)
    kx3, vx3 = kx.reshape(B, tcx, 2 * LANES), vx.reshape(B, tcx, 2 * LANES)
    nb = tq // _WB
    qb = math.gcd(qb, nb)
    has_latent = lat_kv is not None
    whole = lambda t: pl.BlockSpec((1, t, 2 * LANES), lambda b, n: (b, 0, 0))
    in_specs = [pl.BlockSpec(memory_space=pltpu.SMEM),
                pl.BlockSpec((1, qb * _WB, BRANCH), lambda b, n: (b, n, 0))]
    args = [sink.astype(F32), q3]
    if has_latent:
        assert tq >= 3 * _WB
        in_specs += [whole(tq), whole(tq)]
        args += [a.reshape(B, tq, 2 * LANES) for a in lat_kv]
    in_specs += [whole(tcx), whole(tcx)]
    args += [kx3, vx3]
    out = pl.pallas_call(
        functools.partial(_win_kernel, has_latent=has_latent, qb=qb, seq=tq),
        grid=(B, nb // qb),
        in_specs=in_specs,
        out_specs=pl.BlockSpec((1, qb * _WB, BRANCH), lambda b, n: (b, n, 0)),
        out_shape=jax.ShapeDtypeStruct((B, tq, BRANCH), BF16),
        compiler_params=_params(("arbitrary", "arbitrary")),
        name="win_lat" if has_latent else "win_ctx",
    )(*args)
    return out.reshape(B * tq, BRANCH)


def _diff_kernel(lam_ref, q_ref, *refs, tq, tk, tcx, tlat, lam_init):
    if tlat:
        kx_ref, vx_ref, kl_ref, vl_ref, nrm_ref, o_ref = refs
    else:
        kx_ref, vx_ref, nrm_ref, o_ref = refs
        kl_ref = vl_ref = None
    q = q_ref[0]
    lane = lax.broadcasted_iota(jnp.int32, (tq, LANES), 1)
    left = lane < LANES // 2
    zero = jnp.zeros((tq, LANES), BF16)
    tiles = [(kx_ref, vx_ref, 0, tcx)] + [(kl_ref, vl_ref, j * tk, tk) for j in range(tlat // tk)]
    ones = jnp.ones((max(tk, tcx), LANES), BF16)

    lhs = (jnp.where(left, q, zero), jnp.where(left, zero, q))
    state = [None, None]
    for k_ref, v_ref, r0, n in tiles:
        vx = jnp.concatenate([v_ref[0, r0:r0 + n, :], ones[0:n]], axis=1)
        k = k_ref[0, r0:r0 + n, :]
        for comp in range(2):
            s = _dot_nt(lhs[comp], k)
            m_part = s[:, 0:LANES]
            for t in range(1, n // LANES):
                m_part = jnp.maximum(m_part, s[:, t * LANES:(t + 1) * LANES])
            m_j = jnp.max(m_part, axis=-1, keepdims=True)
            part = _dot(jnp.exp2(s - m_j).astype(BF16), vx)
            if state[comp] is None:
                state[comp] = (m_j, part)
            else:
                m_run, acc = state[comp]
                m_new = jnp.maximum(m_run, m_j)
                state[comp] = (m_new, acc * jnp.exp2(m_run - m_new) + part * jnp.exp2(m_j - m_new))
    outs = [acc[:, 0:LANES] / acc[:, LANES:2 * LANES] for _, acc in state]
    lv = lam_ref[...]
    s1 = jnp.sum(lv[0:1] * lv[1:2], axis=-1, keepdims=True)
    s2 = jnp.sum(lv[2:3] * lv[3:4], axis=-1, keepdims=True)
    lam = jnp.exp(s1) - jnp.exp(s2) + lam_init
    y = outs[0] - lam * outs[1]
    ms = jnp.mean(y * y, axis=-1, keepdims=True)
    o_ref[0] = (y * lax.rsqrt(ms + EPS) * nrm_ref[...] * (1.0 - lam_init)).astype(BF16)


def _diff_attention(q, kx, vx, lamv, d_norm, lam_init, B, tq_total, tcx, lat_kv=None, tq=1024, tk=1024):
    q3 = q.reshape(B, tq_total, BRANCH)
    kx3, vx3 = kx.reshape(B, tcx, BRANCH), vx.reshape(B, tcx, BRANCH)
    tq = min(tq, tq_total)
    head_blk = lambda t, f: pl.BlockSpec((1, t, LANES), f)
    whole = lambda b, h, i: (b, 0, h)
    in_specs = [pl.BlockSpec((8, LANES), lambda b, h, i: (0, 0)),
                head_blk(tq, lambda b, h, i: (b, i, h)),
                head_blk(tcx, whole), head_blk(tcx, whole)]
    args = [lamv, q3, kx3, vx3]
    tlat = 0
    if lat_kv is not None:
        tlat = tq_total
        k3, v3 = (a.reshape(B, tlat, BRANCH) for a in lat_kv)
        in_specs += [head_blk(tlat, whole), head_blk(tlat, whole)]
        args += [k3, v3]
    in_specs.append(pl.BlockSpec((1, LANES), lambda b, h, i: (0, h)))
    args.append(d_norm.reshape(1, -1))
    tk = min(tk, max(tlat, LANES))
    out = pl.pallas_call(
        functools.partial(_diff_kernel, tq=tq, tk=tk, tcx=tcx, tlat=tlat, lam_init=lam_init),
        grid=(B, DF_HEADS, tq_total // tq),
        in_specs=in_specs,
        out_specs=head_blk(tq, lambda b, h, i: (b, i, h)),
        out_shape=jax.ShapeDtypeStruct((B, tq_total, BRANCH), BF16),
        compiler_params=_params(("arbitrary", "arbitrary", "arbitrary")),
        name="diff_lat" if tlat else "diff_ctx",
    )(*args)
    return out.reshape(B * tq_total, BRANCH)


def _merge_kernel(ya_ref, yb_ref, yc_ref, gate_ref, x_ref, mod_ref, wb_ref, wo_ref, o_ref, *, d):
    merged = None
    for i, y_ref in enumerate((ya_ref, yb_ref, yc_ref)):
        gate = jax.nn.sigmoid(gate_ref[:, i * d:(i + 1) * d].astype(F32))
        term = gate * _dot(y_ref[...], wb_ref[i])
        merged = term if merged is None else merged + term
    out = _dot(merged.astype(BF16), wo_ref[...])
    gt = mod_ref[0][:, 2 * d:3 * d]
    o_ref[...] = x_ref[...] + gt * out


def _merge(ya, yb, yc, gate, x2, mod, wb, wo, *, tm, rows_per_mod):
    N, D = x2.shape
    row = lambda w: pl.BlockSpec((tm, w), lambda i: (i, 0))
    return pl.pallas_call(
        functools.partial(_merge_kernel, d=D),
        grid=(N // tm,),
        in_specs=[row(BRANCH), row(BRANCH), row(BRANCH), row(N_BRANCH * D), row(D),
                  pl.BlockSpec((1, 1, 6 * D), lambda i: ((i * tm) // rows_per_mod, 0, 0)),
                  _resident((N_BRANCH, BRANCH, D), lambda i: (0, 0, 0)),
                  _resident((D, D), lambda i: (0, 0))],
        out_specs=row(D),
        out_shape=jax.ShapeDtypeStruct((N, D), F32),
        compiler_params=_params(("arbitrary",)),
        name="merge",
    )(ya, yb, yc, gate, x2, mod, wb, wo)


_FFN_CHUNK = 256


def _ffn_kernel(x_ref, mod_ref, g_ref, wi_ref, wo_ref, gf_ref, o_ref, *, d, hidden, final_norm):
    x = x_ref[...]
    mod = mod_ref[0]
    h = _modnorm(x, g_ref[...], mod[:, 3 * d:4 * d], mod[:, 4 * d:5 * d]).astype(BF16)
    acc = None
    for c0 in range(0, hidden, _FFN_CHUNK):
        gate = _dot(h, wi_ref[:, c0:c0 + _FFN_CHUNK])
        up = _dot(h, wi_ref[:, hidden + c0:hidden + c0 + _FFN_CHUNK])
        a = (gate * jax.nn.sigmoid(gate) * up).astype(BF16)
        part = _dot(a, wo_ref[c0:c0 + _FFN_CHUNK, :])
        acc = part if acc is None else acc + part
    y = x + mod[:, 5 * d:6 * d] * acc
    if final_norm:
        ms = jnp.mean(y * y, axis=-1, keepdims=True)
        y = y * lax.rsqrt(ms + EPS) * gf_ref[...]
    o_ref[...] = y


def _ffn(x2, mod, g, wi, wo, g_final, *, tm, rows_per_mod, final_norm):
    N, D = x2.shape
    hidden = wo.shape[0]
    return pl.pallas_call(
        functools.partial(_ffn_kernel, d=D, hidden=hidden, final_norm=final_norm),
        grid=(N // tm,),
        in_specs=[pl.BlockSpec((tm, D), lambda i: (i, 0)),
                  pl.BlockSpec((1, 1, 6 * D), lambda i: ((i * tm) // rows_per_mod, 0, 0)),
                  pl.BlockSpec((1, D), lambda i: (0, 0)),
                  _resident((D, 2 * hidden), lambda i: (0, 0)),
                  _resident((hidden, D), lambda i: (0, 0)),
                  pl.BlockSpec((1, D), lambda i: (0, 0))],
        out_specs=pl.BlockSpec((tm, D), lambda i: (i, 0)),
        out_shape=jax.ShapeDtypeStruct((N, D), F32),
        compiler_params=_params(("arbitrary",)),
        name="ffn",
    )(x2, mod, g, wi, wo, g_final)


def _row_tile(n, want=512):
    t = want
    while n % t:
        t //= 2
    return t


def kernel(x, c, ctx, c_ctx, w_mod, b_mod, g_mix, g_ffn, w_in, b_gate, conv_w, m_norm, sink,
           lam_q1, lam_k1, lam_q2, lam_k2, d_norm, w_branch, w_out, w_ffn_in, w_ffn_out, g_final):
    B, T, D = x.shape
    Tc = ctx.shape[1]
    depth = w_mod.shape[0]
    assert T % CHUNK == 0 and Tc % CHUNK == 0 and T % GRID_W == 0

    n_rows = -(-(B + 1) // 16) * 16
    cc = jnp.concatenate([c, c_ctx[None, :], jnp.zeros((n_rows - B - 1, D), F32)], axis=0)
    mods = _adaln(cc, w_mod, b_mod)

    rope = _rope_tables(T)
    no_rope = tuple(jnp.zeros((Tc, LANES), F32) for _ in range(3))
    tm = _row_tile(T)
    tmc = _row_tile(Tc)
    names = [s[0] for s in _SEGS]

    xl = x.reshape(B * T, D)
    xc = ctx.reshape(B * Tc, D)
    for l in range(depth):
        need_ctx = l < depth - 1
        lam_init = 0.8 - 0.6 * math.exp(-0.3 * l)
        mod_l = mods[l, :B].reshape(B, 1, 6 * D)
        mod_c = mods[l, B:B + 1].reshape(1, 1, 6 * D)
        w_ext = _extend_w_in(w_in[l])
        g_mix_l = g_mix[l].reshape(1, D)
        g_ffn_l = g_ffn[l].reshape(1, D)
        lamv = jnp.zeros((8, LANES), F32)
        for i, v in enumerate((lam_q1[l], lam_k1[l], lam_q2[l], lam_k2[l])):
            lamv = lamv.at[i, :v.shape[0]].set(v.astype(F32))

        pl_ = dict(zip(names, _inproj(xl, mod_l, g_mix_l, rope, w_ext, tm=tm, rows_per_mod=T, seq=T, use_rope=True)))
        pc_ = dict(zip(names, _inproj(xc, mod_c, g_mix_l, no_rope, w_ext, tm=tmc, rows_per_mod=B * Tc, seq=Tc,
                                      use_rope=False)))

        gates_l = _gates(pl_["mg"], b_gate[l], B, T)
        gates_c = _gates(pc_["mg"], b_gate[l], B, Tc)
        ya, ya_c = _mlstm(pc_, pl_, gates_c, gates_l, conv_w[l], m_norm[l], B, Tc, T, need_ctx)
        yb = _window_attention(pl_["wq"], pc_["wk"], pc_["wv"], sink[l], B, T, Tc, lat_kv=(pl_["wk"], pl_["wv"]))
        yc = _diff_attention(pl_["dq"], pc_["dk"], pc_["dv"], lamv, d_norm[l], lam_init, B, T, Tc,
                             lat_kv=(pl_["dk"], pl_["dv"]))
        wb = w_branch[l].astype(BF16)
        wo = w_out[l].astype(BF16)
        wfi = w_ffn_in[l].astype(BF16)
        wfo = w_ffn_out[l].astype(BF16)
        gfin = g_final.reshape(1, D)
        last = l == depth - 1
        xl = _merge(ya, yb, yc, pl_["gate"], xl, mod_l, wb, wo, tm=tm, rows_per_mod=T)
        xl = _ffn(xl, mod_l, g_ffn_l, wfi, wfo, gfin, tm=tm, rows_per_mod=T, final_norm=last)
        if need_ctx:
            yb_c = _window_attention(pc_["wq"], pc_["wk"], pc_["wv"], sink[l], B, Tc, Tc)
            yc_c = _diff_attention(pc_["dq"], pc_["dk"], pc_["dv"], lamv, d_norm[l], lam_init, B, Tc, Tc)
            xc = _merge(ya_c, yb_c, yc_c, pc_["gate"], xc, mod_c, wb, wo, tm=tmc, rows_per_mod=B * Tc)
            xc = _ffn(xc, mod_c, g_ffn_l, wfi, wfo, gfin, tm=tmc, rows_per_mod=B * Tc, final_norm=False)
    return xl.reshape(B, T, D)
```

```python
import functools
import math

import jax
import jax.numpy as jnp
import numpy as np
from jax import lax
from jax.experimental import pallas as pl
from jax.experimental.pallas import tpu as pltpu

F32 = jnp.float32
BF16 = jnp.bfloat16

GRID_W = 64
BRANCH = 512
N_BRANCH = 3
M_HEADS = 4
M_DIM = 128
W_HEADS = 8
W_KV_HEADS = 2
W_DIM = 64
DF_HEADS = 4
ROPE_DIM = 64
ROPE_BASE = 10000.0
EPS = 1e-6
LANES = 128
CHUNK = 128
VMEM_LIMIT = 56 * 1024 * 1024

_NT = (((1,), (1,)), ((), ()))
LOG2E = math.log2(math.e)
_Q_SCALE = W_DIM ** -0.5 * LOG2E


def _dot(a, b):
    return jnp.dot(a, b, preferred_element_type=F32)


def _dot_nt(a, b):
    return lax.dot_general(a, b, _NT, preferred_element_type=F32)


def _params(sem):
    return pltpu.CompilerParams(dimension_semantics=sem, vmem_limit_bytes=VMEM_LIMIT)


def _resident(shape, index_map):
    return pl.BlockSpec(shape, index_map, pipeline_mode=pl.Buffered(1))


def _adaln_kernel(c_ref, w_ref, b_ref, o_ref):
    c = c_ref[...]
    s = (c * jax.nn.sigmoid(c)).astype(BF16)
    o_ref[0] = _dot(s, w_ref[0].astype(BF16)) + b_ref[0]


def _adaln(cc, w_mod, b_mod):
    L, D, N = w_mod.shape
    R = cc.shape[0]
    tn = 1536
    return pl.pallas_call(
        _adaln_kernel,
        grid=(L, N // tn),
        in_specs=[pl.BlockSpec((R, D), lambda l, j: (0, 0)),
                  pl.BlockSpec((1, D, tn), lambda l, j: (l, 0, j)),
                  pl.BlockSpec((1, 1, tn), lambda l, j: (l, 0, j))],
        out_specs=pl.BlockSpec((1, R, tn), lambda l, j: (l, 0, j)),
        out_shape=jax.ShapeDtypeStruct((L, R, N), F32),
        compiler_params=_params(("arbitrary", "arbitrary")),
        name="adaln",
    )(cc, w_mod, b_mod.reshape(L, 1, N))


def _modnorm(x, g, shift, scale):
    ms = jnp.mean(x * x, axis=-1, keepdims=True)
    return x * lax.rsqrt(ms + EPS) * (g * (1.0 + scale)) + shift


_SEGS = (("mqk", 1024, BF16, "plain"), ("mv", 512, BF16, "plain"), ("mo", 512, BF16, "plain"),
         ("mg", 128, F32, "plain"), ("wq", 512, BF16, "ropeq"), ("wk", 256, BF16, "rope"),
         ("wv", 256, BF16, "plain"), ("dq", 512, BF16, "ropeq"), ("dk", 512, BF16, "rope"),
         ("dv", 512, BF16, "plain"), ("gate", 3072, BF16, "plain"))
_EXT_WIDTH = sum(s[1] for s in _SEGS)
_COL_CHUNK = 512


def _inproj_kernel(x_ref, mod_ref, g_ref, cos_ref, sa_ref, sb_ref, w_ref, *out_refs, d, use_rope):
    x = x_ref[...]
    mod = mod_ref[0]
    h = _modnorm(x, g_ref[...], mod[:, 0:d], mod[:, d:2 * d]).astype(BF16)
    if use_rope:
        cos, sa, sb = cos_ref[...], sa_ref[...], sb_ref[...]
    off = 0
    for (name, width, dt, kind), o_ref in zip(_SEGS, out_refs):
        for c0 in range(0, width, _COL_CHUNK):
            cw = min(_COL_CHUNK, width - c0)
            acc = _dot(h, w_ref[:, off + c0:off + c0 + cw])
            if kind != "plain":
                tiles = []
                for t in range(cw // LANES):
                    a = acc[:, t * LANES:(t + 1) * LANES]
                    if use_rope:
                        a = (a * cos + pltpu.roll(a, LANES - 16, 1) * sa + pltpu.roll(a, 16, 1) * sb)
                    if kind == "ropeq":
                        a = a * _Q_SCALE
                    tiles.append(a)
                acc = jnp.concatenate(tiles, axis=1) if len(tiles) > 1 else tiles[0]
            o_ref[:, c0:c0 + cw] = acc.astype(dt)
        off += width


def _inproj(x2, mod, g, rope, w_ext, *, tm, rows_per_mod, seq, use_rope):
    N, D = x2.shape
    nt = seq // tm
    kern = functools.partial(_inproj_kernel, d=D, use_rope=use_rope)
    rope_spec = pl.BlockSpec((tm, LANES), lambda i: (i % nt, 0))
    out_shape = [jax.ShapeDtypeStruct((N, w), dt) for _, w, dt, _ in _SEGS]
    out_specs = [pl.BlockSpec((tm, w), lambda i: (i, 0)) for _, w, _, _ in _SEGS]
    return pl.pallas_call(
        kern,
        grid=(N // tm,),
        in_specs=[pl.BlockSpec((tm, D), lambda i: (i, 0)),
                  pl.BlockSpec((1, 1, 6 * D), lambda i: ((i * tm) // rows_per_mod, 0, 0)),
                  pl.BlockSpec((1, D), lambda i: (0, 0)),
                  rope_spec, rope_spec, rope_spec,
                  _resident((D, _EXT_WIDTH), lambda i: (0, 0))],
        out_specs=out_specs,
        out_shape=out_shape,
        compiler_params=_params(("arbitrary",)),
        name="inproj_lat" if use_rope else "inproj_ctx",
    )(x2, mod, g, *rope, w_ext)


def _extend_w_in(w):
    D = w.shape[0]
    o = np.cumsum([0, 512, 512, 512, 512, 16, 512, 128, 128, 512, 512, 512, 3072])
    p = [w[:, o[i]:o[i + 1]] for i in range(12)]
    dup = lambda a: jnp.concatenate([a[:, 0:64], a[:, 0:64], a[:, 64:128], a[:, 64:128]], axis=1)
    mg = jnp.concatenate([p[4], jnp.zeros((D, LANES - 16), w.dtype)], axis=1)
    ext = jnp.concatenate([p[0], p[1], p[2], p[3], mg, p[5], dup(p[6]), dup(p[7]), p[8], p[9], p[10], p[11]], axis=1)
    return ext.astype(BF16)


def _rope_tables(n_tokens):
    rows = n_tokens // GRID_W
    r, col = jnp.meshgrid(jnp.arange(rows), jnp.arange(GRID_W), indexing="ij")
    half = ROPE_DIM // 2
    inv = ROPE_BASE ** (-jnp.arange(0, half, 2, dtype=F32) / half)
    ang_r = r.reshape(-1, 1).astype(F32) * inv
    ang_c = col.reshape(-1, 1).astype(F32) * inv
    ang = jnp.concatenate([ang_r, ang_r, ang_c, ang_c], axis=-1)
    cos, sin = jnp.cos(ang), jnp.sin(ang)
    cos2 = jnp.concatenate([cos, cos], axis=-1)
    sin2 = jnp.concatenate([sin, sin], axis=-1)
    first = (jnp.arange(LANES) % 32) < 16
    sin_a = jnp.where(first, -sin2, 0.0)
    sin_b = jnp.where(first, 0.0, sin2)
    return cos2, sin_a, sin_b


def _scan_lanes(x, op, fill, reverse):
    lane = lax.broadcasted_iota(jnp.int32, x.shape, 1)
    d = 1
    while d < LANES:
        if reverse:
            shifted = jnp.where(lane < LANES - d, pltpu.roll(x, LANES - d, 1), fill)
        else:
            shifted = jnp.where(lane >= d, pltpu.roll(x, d, 1), fill)
        x = op(x, shifted)
        d *= 2
    return x


def _gates_kernel(x_ref, bias_ref, o_ref, *, rows):
    x = x_ref[0] + bias_ref[...]
    lane = lax.broadcasted_iota(jnp.int32, (rows, LANES), 1)
    for d, reverse in enumerate((False, True)):
        i_pre = x[(2 * d) * rows:(2 * d + 1) * rows]
        f_pre = x[(2 * d + 1) * rows:(2 * d + 2) * rows]
        log_f = jax.nn.log_sigmoid(f_pre)
        b = _scan_lanes(log_f, jnp.add, 0.0, reverse)
        a = i_pre - b
        pm = _scan_lanes(a, jnp.maximum, -jnp.inf, reverse)
        last = 0 if reverse else LANES - 1
        b_end = jnp.sum(jnp.where(lane == last, b, 0.0), axis=1, keepdims=True)
        a_max = jnp.max(a, axis=1, keepdims=True)
        o_ref[0, 5 * d + 0] = b
        o_ref[0, 5 * d + 1] = a
        o_ref[0, 5 * d + 2] = pm
        o_ref[0, 5 * d + 3] = jnp.broadcast_to(b_end, (rows, LANES))
        o_ref[0, 5 * d + 4] = jnp.broadcast_to(a_max, (rows, LANES))


def _gates(mg, b_gate, B, T):
    nc = T // CHUNK
    rows = M_HEADS * nc
    g = mg[:, :4 * M_HEADS].reshape(B, T, 4 * M_HEADS)
    gt = jnp.transpose(g, (0, 2, 1)).reshape(B, 4 * rows, LANES)
    bias = jnp.repeat(b_gate.astype(F32), nc).reshape(4 * rows, 1)
    return pl.pallas_call(
        functools.partial(_gates_kernel, rows=rows),
        grid=(B,),
        in_specs=[pl.BlockSpec((1, 4 * rows, LANES), lambda b: (b, 0, 0)),
                  pl.BlockSpec((4 * rows, 1), lambda b: (0, 0))],
        out_specs=pl.BlockSpec((1, 10, rows, LANES), lambda b: (b, 0, 0, 0)),
        out_shape=jax.ShapeDtypeStruct((B, 10, rows, LANES), F32),
        compiler_params=_params(("arbitrary",)),
        name="mlstm_gates",
    )(gt, bias)


_CONV_TILE = 256


def _mlstm_kernel(qc_ref, kc_ref, vc_ref, oc_ref, pc_ref, ql_ref, kl_ref, vl_ref, ol_ref, pl_ref,
                  cwq_ref, cwk_ref, nrm_ref, *rest, tc, tl, need_ctx):
    if need_ctx:
        yl_ref, yc_ref = rest[0], rest[1]
        scr = rest[2:]
    else:
        yl_ref, yc_ref = rest[0], None
        scr = rest[1:]
    xpad, qs, ks, hf, hb, cx, mst = scr
    head = pl.program_id(1)
    tot = tc + tl

    def conv_stream(u_ref, w_ref, dst, dst_off, ts, scale):
        xpad[0:8, :] = jnp.zeros((8, LANES), F32)
        xpad[8:8 + ts, :] = u_ref[0].astype(F32)
        xpad[8 + ts:16 + ts, :] = jnp.zeros((8, LANES), F32)
        w = w_ref[...]
        w0, w1, w2 = w[0:1], w[1:2], w[2:3]
        tile = min(_CONV_TILE, ts)

        def body(i, _):
            r0 = pl.multiple_of(i * tile, 8)
            win = xpad[pl.ds(r0, tile + 16), :]
            prev = pltpu.roll(win, 1, 0)[8:8 + tile]
            nxt = pltpu.roll(win, tile + 15, 0)[8:8 + tile]
            y = prev * w0 + win[8:8 + tile] * w1 + nxt * w2
            y = y * jax.nn.sigmoid(y) * scale
            dst[pl.ds(pl.multiple_of(dst_off + r0, 8), tile), :] = y.astype(BF16)
            return 0

        lax.fori_loop(0, ts // tile, body, 0)

    kscale = M_DIM ** -0.5
    conv_stream(qc_ref, cwq_ref, qs, 0, tc, 1.0)
    conv_stream(kc_ref, cwk_ref, ks, 0, tc, kscale)
    conv_stream(ql_ref, cwq_ref, qs, tc, tl, 1.0)
    conv_stream(kl_ref, cwk_ref, ks, tc, tl, kscale)

    cx[...] = jnp.zeros(cx.shape, F32)
    mst[...] = jnp.zeros(mst.shape, F32)

    row_i = lax.broadcasted_iota(jnp.int32, (CHUNK, CHUNK), 0)
    col_i = lax.broadcasted_iota(jnp.int32, (CHUNK, CHUNK), 1)
    masks =(col_i <= row_i, col_i >= row_i)
    ones_b = jnp.ones((CHUNK, LANES), BF16)
    ones_f = jnp.ones((LANES, CHUNK), F32)

    def colify(row):
        return jnp.broadcast_to(row, (CHUNK, CHUNK)).T

    def chunk_step(d, c, p_ref, v_ref, nc, base, want_h):
        r = head * nc + c
        plane = lambda j: p_ref[0, 5 * d + j, pl.ds(r, 1), :]
        b_row, a_row, pm_row, bend_row, amax_row = (plane(j) for j in range(5))
        m_prev = mst[d:d + 1, :]
        m_end = jnp.maximum(m_prev, amax_row)
        w_end = jnp.exp(a_row - m_end)
        decay = jnp.exp(m_prev - m_end)
        row0 = pl.multiple_of(base + c * CHUNK, CHUNK)
        q = qs[pl.ds(row0, CHUNK), :]
        k = ks[pl.ds(row0, CHUNK), :]
        v = v_ref[0, pl.ds(pl.multiple_of(c * CHUNK, CHUNK), CHUNK), :]
        c_old = cx[d]
        if want_h:
            m_col = jnp.maximum(m_prev, colify(pm_row))
            w_intra = jnp.exp(jnp.where(masks[d], a_row - m_col, -jnp.inf))
            w_inter = jnp.exp(m_prev - m_col)
            floor = jnp.exp(-(colify(b_row) + m_col))
            s = _dot_nt(q, k) * w_intra
            intra = _dot(s.astype(BF16), jnp.concatenate([v, ones_b], axis=1))
            inter = _dot_nt(q, c_old.astype(BF16))
            num = intra[:, 0:M_DIM] + w_inter * inter[:, 0:M_DIM]
            den = intra[:, M_DIM:2 * M_DIM] + w_inter * inter[:, M_DIM:2 * M_DIM]
            hval = num / jnp.maximum(jnp.abs(den), floor)
            dst = hf if d == 0 else hb
            dst[pl.ds(row0, CHUNK), :] = hval
        vxt = jnp.concatenate([v.astype(F32).T, ones_f], axis=0)
        cx[d] = decay * c_old + _dot((vxt * w_end).astype(BF16), k)
        mst[d:d + 1, :] = bend_row + m_end

    ncc, ncl = tc // CHUNK, tl // CHUNK
    for c in range(ncc):
        chunk_step(0, c, pc_ref, vc_ref, ncc, 0, need_ctx)
        chunk_step(1, ncc - 1 - c, pc_ref, vc_ref, ncc, 0, need_ctx)

    def lat_body(c, _):
        chunk_step(0, c, pl_ref, vl_ref, ncl, tc, True)
        chunk_step(1, ncl - 1 - c, pl_ref, vl_ref, ncl, tc, True)
        return 0

    lax.fori_loop(0, ncl, lat_body, 0, unroll=8)

    nrm = nrm_ref[...]

    def finish(o_ref, y_ref, base, ts):
        def body(i, _):
            r0 = pl.multiple_of(i * CHUNK, CHUNK)
            hsum = hf[pl.ds(base + r0, CHUNK), :] + hb[pl.ds(base + r0, CHUNK), :]
            y = hsum * jax.nn.sigmoid(o_ref[0, pl.ds(r0, CHUNK), :].astype(F32))
            ms = jnp.mean(y * y, axis=-1, keepdims=True)
            y_ref[0, pl.ds(r0, CHUNK), :] = (y * lax.rsqrt(ms + EPS) * nrm).astype(BF16)
            return 0

        lax.fori_loop(0, ts // CHUNK, body, 0, unroll=min(4, ts // CHUNK))

    finish(ol_ref, yl_ref, tc, tl)
    if need_ctx:
        finish(oc_ref, yc_ref, 0, tc)


def _mlstm(ctx_p, lat_p, gates_c, gates_l, conv_w, m_norm, B, tc, tl, need_ctx):
    r3 = lambda a, t: a.reshape(B, t, a.shape[-1])
    qk_c, v_c, o_c = r3(ctx_p["mqk"], tc), r3(ctx_p["mv"], tc), r3(ctx_p["mo"], tc)
    qk_l, v_l, o_l = r3(lat_p["mqk"], tl), r3(lat_p["mv"], tl), r3(lat_p["mo"], tl)
    H = M_HEADS
    head_blk = lambda t, off: pl.BlockSpec((1, t, LANES), lambda b, h: (b, 0, h + off))
    gate_blk = lambda g: pl.BlockSpec((1,) + g.shape[1:], lambda b, h: (b, 0, 0, 0))
    in_specs = [head_blk(tc, 0), head_blk(tc, H), head_blk(tc, 0), head_blk(tc, 0), gate_blk(gates_c),
                head_blk(tl, 0), head_blk(tl, H), head_blk(tl, 0), head_blk(tl, 0), gate_blk(gates_l),
                pl.BlockSpec((3, LANES), lambda b, h: (0, h)),
                pl.BlockSpec((3, LANES), lambda b, h: (0, h + H)),
                pl.BlockSpec((1, LANES), lambda b, h: (0, h))]
    out_shape = [jax.ShapeDtypeStruct((B, tl, BRANCH), BF16)]
    out_specs = [head_blk(tl, 0)]
    if need_ctx:
        out_shape.append(jax.ShapeDtypeStruct((B, tc, BRANCH), BF16))
        out_specs.append(head_blk(tc, 0))
    tot = tc + tl
    scratch = [pltpu.VMEM((max(tc, tl) + 16, LANES), F32),
               pltpu.VMEM((tot, LANES), BF16), pltpu.VMEM((tot, LANES), BF16),
               pltpu.VMEM((tot, LANES), F32), pltpu.VMEM((tot, LANES), F32),
               pltpu.VMEM((2, 2 * LANES, LANES), F32), pltpu.VMEM((8, LANES), F32)]
    outs = pl.pallas_call(
        functools.partial(_mlstm_kernel, tc=tc, tl=tl, need_ctx=need_ctx),
        grid=(B, H),
        in_specs=in_specs,
        out_specs=out_specs,
        out_shape=out_shape,
        scratch_shapes=scratch,
        compiler_params=_params(("arbitrary", "arbitrary")),
        name="mlstm",
    )(qk_c, qk_c, v_c, o_c, gates_c, qk_l, qk_l, v_l, o_l, gates_l,
      conv_w, conv_w, m_norm.reshape(1, -1))
    ya = outs[0].reshape(B * tl, BRANCH)
    ya_c = outs[1].reshape(B * tc, BRANCH) if need_ctx else None
    return ya, ya_c


_WB = 128


def _win_kernel(sink_ref, q_ref, *refs, has_latent, qb, seq):
    if has_latent:
        kl_ref, vl_ref, kx_ref, vx_ref, o_ref = refs
    else:
        kx_ref, vx_ref, o_ref = refs
    n = pl.program_id(1)
    rep = W_HEADS // W_KV_HEADS
    rows = rep * _WB
    span = 3 * _WB
    lane = lax.broadcasted_iota(jnp.int32, (_WB, LANES), 1)
    left = lane < W_DIM
    zero = jnp.zeros((_WB, LANES), BF16)
    if has_latent:
        qq = jnp.bitwise_and(lax.broadcasted_iota(jnp.int32, (rows, span), 0), _WB - 1)
        dist = lax.broadcasted_iota(jnp.int32, (rows, span), 1) - qq
    def logits_of(j, g):
        r0 = j * _WB
        pieces = []
        for t in range(rep // 2):
            qt = q_ref[0, r0:r0 + _WB, (g * rep // 2 + t) * LANES:(g * rep // 2 + t + 1) * LANES]
            pieces += [jnp.where(left, qt, zero), jnp.where(left, zero, qt)]
        lhs = jnp.concatenate(pieces, axis=0)
        gsl = slice(g * LANES, (g + 1) * LANES)
        logits = [_dot_nt(lhs, kx_ref[0, :, gsl])]
        vals = [vx_ref[0, :, gsl]]
        if has_latent:
            q_start = (n * qb + j) * _WB
            k_start = pl.multiple_of(jnp.clip(q_start - _WB, 0, seq - span), _WB)
            shift = k_start - q_start
            near = jnp.logical_and(dist + shift >= -_WB, dist + shift <= _WB)
            s_win = _dot_nt(lhs, kl_ref[0, pl.ds(k_start, span), gsl])
            logits.append(jnp.where(near, s_win, -jnp.inf))
            vals.append(vl_ref[0, pl.ds(k_start, span), gsl])
        return logits, vals

    def finish(j, g, logits, vals):
        r0 = j * _WB
        sink_rep = jnp.concatenate(
            [jnp.full((_WB, LANES), sink_ref[g * rep + r] * LOG2E, F32) for r in range(rep)], axis=0)
        m_part = None
        for s in logits:
            for t in range(s.shape[1] // LANES):
                st = s[:, t * LANES:(t + 1) * LANES]
                m_part = st if m_part is None else jnp.maximum(m_part, st)
        m = jnp.maximum(sink_rep, jnp.max(m_part, axis=-1, keepdims=True))
        acc = None
        for s, v in zip(logits, vals):
            p = jnp.concatenate([jnp.exp2(s[:, t * LANES:(t + 1) * LANES] - m)
                                 for t in range(s.shape[1] // LANES)], axis=1)
            part = _dot(p.astype(BF16), jnp.concatenate([v, jnp.ones(v.shape, BF16)], axis=1))
            acc = part if acc is None else acc + part
        o = acc[:, 0:LANES] / (acc[:, LANES:2 * LANES] + jnp.exp2(sink_rep - m))
        for t in range(rep // 2):
            tile = jnp.where(left, o[(2 * t) * _WB:(2 * t + 1) * _WB], o[(2 * t + 1) * _WB:(2 * t + 2) * _WB])
            c0 = (g * rep // 2 + t) * LANES
            o_ref[0, r0:r0 + _WB, c0:c0 + LANES] = tile.astype(BF16)

    streams = [(j, g) for j in range(qb) for g in range(W_KV_HEADS)]
    ahead = 3
    pending = []
    for i, (j, g) in enumerate(streams):
        pending.append((j, g) + logits_of(j, g))
        if i >= ahead:
            finish(*pending.pop(0))
    for item in pending:
        finish(*item)


def _window_attention(q, kx, vx, sink, B, tq, tcx, lat_kv=None, qb=4):
    q3 = q.reshape(B, tq, BRANCH)
    kx3, vx3 = kx.reshape(B, tcx, 2 * LANES), vx.reshape(B, tcx, 2 * LANES)
    nb = tq // _WB
    qb = math.gcd(qb, nb)
    has_latent = lat_kv is not None
    whole = lambda t: pl.BlockSpec((1, t, 2 * LANES), lambda b, n: (b, 0, 0))
    in_specs = [pl.BlockSpec(memory_space=pltpu.SMEM),
                pl.BlockSpec((1, qb * _WB, BRANCH), lambda b, n: (b, n, 0))]
    args = [sink.astype(F32), q3]
    if has_latent:
        assert tq >= 3 * _WB
        in_specs += [whole(tq), whole(tq)]
        args += [a.reshape(B, tq, 2 * LANES) for a in lat_kv]
    in_specs += [whole(tcx), whole(tcx)]
    args += [kx3, vx3]
    out = pl.pallas_call(
        functools.partial(_win_kernel, has_latent=has_latent, qb=qb, seq=tq),
        grid=(B, nb // qb),
        in_specs=in_specs,
        out_specs=pl.BlockSpec((1, qb * _WB, BRANCH), lambda b, n: (b, n, 0)),
        out_shape=jax.ShapeDtypeStruct((B, tq, BRANCH), BF16),
        compiler_params=_params(("arbitrary", "arbitrary")),
        name="win_lat" if has_latent else "win_ctx",
    )(*args)
    return out.reshape(B * tq, BRANCH)


def _diff_kernel(lam_ref, q_ref, *refs, tq, tk, tcx, tlat, lam_init):
    if tlat:
        kx_ref, vx_ref, kl_ref, vl_ref, nrm_ref, o_ref = refs
    else:
        kx_ref, vx_ref, nrm_ref, o_ref = refs
        kl_ref = vl_ref = None
    q = q_ref[0]
    lane = lax.broadcasted_iota(jnp.int32, (tq, LANES), 1)
    left = lane < LANES // 2
    zero = jnp.zeros((tq, LANES), BF16)
    tiles = [(kx_ref, vx_ref, 0, tcx)] + [(kl_ref, vl_ref, j * tk, tk) for j in range(tlat // tk)]
    ones = jnp.ones((max(tk, tcx), LANES), BF16)

    lhs = (jnp.where(left, q, zero), jnp.where(left, zero, q))
    state = [None, None]
    for k_ref, v_ref, r0, n in tiles:
        vx = jnp.concatenate([v_ref[0, r0:r0 + n, :], ones[0:n]], axis=1)
        k = k_ref[0, r0:r0 + n, :]
        for comp in range(2):
            s = _dot_nt(lhs[comp], k)
            m_part = s[:, 0:LANES]
            for t in range(1, n // LANES):
                m_part = jnp.maximum(m_part, s[:, t * LANES:(t + 1) * LANES])
            m_j = jnp.max(m_part, axis=-1, keepdims=True)
            part = _dot(jnp.exp2(s - m_j).astype(BF16), vx)
            if state[comp] is None:
                state[comp] = (m_j, part)
            else:
                m_run, acc = state[comp]
                m_new = jnp.maximum(m_run, m_j)
                state[comp] = (m_new, acc * jnp.exp2(m_run - m_new) + part * jnp.exp2(m_j - m_new))
    outs = [acc[:, 0:LANES] / acc[:, LANES:2 * LANES] for _, acc in state]
    lv = lam_ref[...]
    s1 = jnp.sum(lv[0:1] * lv[1:2], axis=-1, keepdims=True)
    s2 = jnp.sum(lv[2:3] * lv[3:4], axis=-1, keepdims=True)
    lam = jnp.exp(s1) - jnp.exp(s2) + lam_init
    y = outs[0] - lam * outs[1]
    ms = jnp.mean(y * y, axis=-1, keepdims=True)
    o_ref[0] = (y * lax.rsqrt(ms + EPS) * nrm_ref[...] * (1.0 - lam_init)).astype(BF16)


def _diff_attention(q, kx, vx, lamv, d_norm, lam_init, B, tq_total, tcx, lat_kv=None, tq=1024, tk=1024):
    q3 = q.reshape(B, tq_total, BRANCH)
    kx3, vx3 = kx.reshape(B, tcx, BRANCH), vx.reshape(B, tcx, BRANCH)
    tq = min(tq, tq_total)
    head_blk = lambda t, f: pl.BlockSpec((1, t, LANES), f)
    whole = lambda b, h, i: (b, 0, h)
    in_specs = [pl.BlockSpec((8, LANES), lambda b, h, i: (0, 0)),
                head_blk(tq, lambda b, h, i: (b, i, h)),
                head_blk(tcx, whole), head_blk(tcx, whole)]
    args = [lamv, q3, kx3, vx3]
    tlat = 0
    if lat_kv is not None:
        tlat = tq_total
        k3, v3 = (a.reshape(B, tlat, BRANCH) for a in lat_kv)
        in_specs += [head_blk(tlat, whole), head_blk(tlat, whole)]
        args += [k3, v3]
    in_specs.append(pl.BlockSpec((1, LANES), lambda b, h, i: (0, h)))
    args.append(d_norm.reshape(1, -1))
    tk = min(tk, max(tlat, LANES))
    out = pl.pallas_call(
        functools.partial(_diff_kernel, tq=tq, tk=tk, tcx=tcx, tlat=tlat, lam_init=lam_init),
        grid=(B, DF_HEADS, tq_total // tq),
        in_specs=in_specs,
        out_specs=head_blk(tq, lambda b, h, i: (b, i, h)),
        out_shape=jax.ShapeDtypeStruct((B, tq_total, BRANCH), BF16),
        compiler_params=_params(("arbitrary", "arbitrary", "arbitrary")),
        name="diff_lat" if tlat else "diff_ctx",
    )(*args)
    return out.reshape(B * tq_total, BRANCH)


def _merge_kernel(ya_ref, yb_ref, yc_ref, gate_ref, x_ref, mod_ref, wb_ref, wo_ref, o_ref, *, d):
    merged = None
    for i, y_ref in enumerate((ya_ref, yb_ref, yc_ref)):
        gate = jax.nn.sigmoid(gate_ref[:, i * d:(i + 1) * d].astype(F32))
        term = gate * _dot(y_ref[...], wb_ref[i])
        merged = term if merged is None else merged + term
    out = _dot(merged.astype(BF16), wo_ref[...])
    gt = mod_ref[0][:, 2 * d:3 * d]
    o_ref[...] = x_ref[...] + gt * out


def _merge(ya, yb, yc, gate, x2, mod, wb, wo, *, tm, rows_per_mod):
    N, D = x2.shape
    row = lambda w: pl.BlockSpec((tm, w), lambda i: (i, 0))
    return pl.pallas_call(
        functools.partial(_merge_kernel, d=D),
        grid=(N // tm,),
        in_specs=[row(BRANCH), row(BRANCH), row(BRANCH), row(N_BRANCH * D), row(D),
                  pl.BlockSpec((1, 1, 6 * D), lambda i: ((i * tm) // rows_per_mod, 0, 0)),
                  _resident((N_BRANCH, BRANCH, D), lambda i: (0, 0, 0)),
                  _resident((D, D), lambda i: (0, 0))],
        out_specs=row(D),
        out_shape=jax.ShapeDtypeStruct((N, D), F32),
        compiler_params=_params(("arbitrary",)),
        name="merge",
    )(ya, yb, yc, gate, x2, mod, wb, wo)


_FFN_CHUNK = 256


def _ffn_kernel(x_ref, mod_ref, g_ref, wi_ref, wo_ref, gf_ref, o_ref, *, d, hidden, final_norm):
    x = x_ref[...]
    mod = mod_ref[0]
    h = _modnorm(x, g_ref[...], mod[:, 3 * d:4 * d], mod[:, 4 * d:5 * d]).astype(BF16)
    acc = None
    for c0 in range(0, hidden, _FFN_CHUNK):
        gate = _dot(h, wi_ref[:, c0:c0 + _FFN_CHUNK])
        up = _dot(h, wi_ref[:, hidden + c0:hidden + c0 + _FFN_CHUNK])
        a = (gate * jax.nn.sigmoid(gate) * up).astype(BF16)
        part = _dot(a, wo_ref[c0:c0 + _FFN_CHUNK, :])
        acc = part if acc is None else acc + part
    y = x + mod[:, 5 * d:6 * d] * acc
    if final_norm:
        ms = jnp.mean(y * y, axis=-1, keepdims=True)
        y = y * lax.rsqrt(ms + EPS) * gf_ref[...]
    o_ref[...] = y


def _ffn(x2, mod, g, wi, wo, g_final, *, tm, rows_per_mod, final_norm):
    N, D = x2.shape
    hidden = wo.shape[0]
    return pl.pallas_call(
        functools.partial(_ffn_kernel, d=D, hidden=hidden, final_norm=final_norm),
        grid=(N // tm,),
        in_specs=[pl.BlockSpec((tm, D), lambda i: (i, 0)),
                  pl.BlockSpec((1, 1, 6 * D), lambda i: ((i * tm) // rows_per_mod, 0, 0)),
                  pl.BlockSpec((1, D), lambda i: (0, 0)),
                  _resident((D, 2 * hidden), lambda i: (0, 0)),
                  _resident((hidden, D), lambda i: (0, 0)),
                  pl.BlockSpec((1, D), lambda i: (0, 0))],
        out_specs=pl.BlockSpec((tm, D), lambda i: (i, 0)),
        out_shape=jax.ShapeDtypeStruct((N, D), F32),
        compiler_params=_params(("arbitrary",)),
        name="ffn",
    )(x2, mod, g, wi, wo, g_final)


def _row_tile(n, want=512):
    t = want
    while n % t:
        t //= 2
    return t


def kernel(x, c, ctx, c_ctx, w_mod, b_mod, g_mix, g_ffn, w_in, b_gate, conv_w, m_norm, sink,
           lam_q1, lam_k1, lam_q2, lam_k2, d_norm, w_branch, w_out, w_ffn_in, w_ffn_out, g_final):
    B, T, D = x.shape
    Tc = ctx.shape[1]
    depth = w_mod.shape[0]
    assert T % CHUNK == 0 and Tc % CHUNK == 0 and T % GRID_W == 0

    n_rows = -(-(B + 1) // 16) * 16
    cc = jnp.concatenate([c, c_ctx[None, :], jnp.zeros((n_rows - B - 1, D), F32)], axis=0)
    mods = _adaln(cc, w_mod, b_mod)

    rope = _rope_tables(T)
    no_rope = tuple(jnp.zeros((Tc, LANES), F32) for _ in range(3))
    tm = _row_tile(T)
    tmc = _row_tile(Tc)
    names = [s[0] for s in _SEGS]

    xl = x.reshape(B * T, D)
    xc = ctx.reshape(B * Tc, D)
    for l in range(depth):
        need_ctx = l < depth - 1
        lam_init = 0.8 - 0.6 * math.exp(-0.3 * l)
        mod_l = mods[l, :B].reshape(B, 1, 6 * D)
        mod_c = mods[l, B:B + 1].reshape(1, 1, 6 * D)
        w_ext = _extend_w_in(w_in[l])
        g_mix_l = g_mix[l].reshape(1, D)
        g_ffn_l = g_ffn[l].reshape(1, D)
        lamv = jnp.zeros((8, LANES), F32)
        for i, v in enumerate((lam_q1[l], lam_k1[l], lam_q2[l], lam_k2[l])):
            lamv = lamv.at[i, :v.shape[0]].set(v.astype(F32))

        pl_ = dict(zip(names, _inproj(xl, mod_l, g_mix_l, rope, w_ext, tm=tm, rows_per_mod=T, seq=T, use_rope=True)))
        pc_ = dict(zip(names, _inproj(xc, mod_c, g_mix_l, no_rope, w_ext, tm=tmc, rows_per_mod=B * Tc, seq=Tc,
                                      use_rope=False)))

        gates_l = _gates(pl_["mg"], b_gate[l], B, T)
        gates_c = _gates(pc_["mg"], b_gate[l], B, Tc)
        ya, ya_c = _mlstm(pc_, pl_, gates_c, gates_l, conv_w[l], m_norm[l], B, Tc, T, need_ctx)
        yb = _window_attention(pl_["wq"], pc_["wk"], pc_["wv"], sink[l], B, T, Tc, lat_kv=(pl_["wk"], pl_["wv"]))
        yc = _diff_attention(pl_["dq"], pc_["dk"], pc_["dv"], lamv, d_norm[l], lam_init, B, T, Tc,
                             lat_kv=(pl_["dk"], pl_["dv"]))
        wb = w_branch[l].astype(BF16)
        wo = w_out[l].astype(BF16)
        wfi = w_ffn_in[l].astype(BF16)
        wfo = w_ffn_out[l].astype(BF16)
        gfin = g_final.reshape(1, D)
        last = l == depth - 1
        xl = _merge(ya, yb, yc, pl_["gate"], xl, mod_l, wb, wo, tm=tm, rows_per_mod=T)
        xl = _ffn(xl, mod_l, g_ffn_l, wfi, wfo, gfin, tm=tm, rows_per_mod=T, final_norm=last)
        if need_ctx:
            yb_c = _window_attention(pc_["wq"], pc_["wk"], pc_["wv"], sink[l], B, Tc, Tc)
            yc_c = _diff_attention(pc_["dq"], pc_["dk"], pc_["dv"], lamv, d_norm[l], lam_init, B, Tc, Tc)
            xc = _merge(ya_c, yb_c, yc_c, pc_["gate"], xc, mod_c, wb, wo, tm=tmc, rows_per_mod=B * Tc)
            xc = _ffn(xc, mod_c, g_ffn_l, wfi, wfo, gfin, tm=tmc, rows_per_mod=B * Tc, final_norm=False)
    return xl.reshape(B, T, D)
```

```python
import functools
import math

import jax
import jax.numpy as jnp
import numpy as np
from jax import lax
from jax.experimental import pallas as pl
from jax.experimental.pallas import tpu as pltpu

F32 = jnp.float32
BF16 = jnp.bfloat16

GRID_W = 64
BRANCH = 512
N_BRANCH = 3
M_HEADS = 4
M_DIM = 128
W_HEADS = 8
W_KV_HEADS = 2
W_DIM = 64
DF_HEADS = 4
ROPE_DIM = 64
ROPE_BASE = 10000.0
EPS = 1e-6
LANES = 128
CHUNK = 128
VMEM_LIMIT = 56 * 1024 * 1024

_NT = (((1,), (1,)), ((), ()))
LOG2E = math.log2(math.e)
_Q_SCALE = W_DIM ** -0.5 * LOG2E


def _dot(a, b):
    return jnp.dot(a, b, preferred_element_type=F32)


def _dot_nt(a, b):
    return lax.dot_general(a, b, _NT, preferred_element_type=F32)


def _params(sem):
    return pltpu.CompilerParams(dimension_semantics=sem, vmem_limit_bytes=VMEM_LIMIT)


def _resident(shape, index_map):
    return pl.BlockSpec(shape, index_map, pipeline_mode=pl.Buffered(1))


def _adaln_kernel(c_ref, w_ref, b_ref, o_ref):
    c = c_ref[...]
    s = (c * jax.nn.sigmoid(c)).astype(BF16)
    o_ref[0] = _dot(s, w_ref[0].astype(BF16)) + b_ref[0]


def _adaln(cc, w_mod, b_mod):
    L, D, N = w_mod.shape
    R = cc.shape[0]
    tn = 1536
    return pl.pallas_call(
        _adaln_kernel,
        grid=(L, N // tn),
        in_specs=[pl.BlockSpec((R, D), lambda l, j: (0, 0)),
                  pl.BlockSpec((1, D, tn), lambda l, j: (l, 0, j)),
                  pl.BlockSpec((1, 1, tn), lambda l, j: (l, 0, j))],
        out_specs=pl.BlockSpec((1, R, tn), lambda l, j: (l, 0, j)),
        out_shape=jax.ShapeDtypeStruct((L, R, N), F32),
        compiler_params=_params(("arbitrary", "arbitrary")),
        name="adaln",
    )(cc, w_mod, b_mod.reshape(L, 1, N))


def _modnorm(x, g, shift, scale):
    ms = jnp.mean(x * x, axis=-1, keepdims=True)
    return x * lax.rsqrt(ms + EPS) * (g * (1.0 + scale)) + shift


_SEGS = (("mqk", 1024, BF16, "plain"), ("mv", 512, BF16, "plain"), ("mo", 512, BF16, "plain"),
         ("mg", 128, F32, "plain"), ("wq", 512, BF16, "ropeq"), ("wk", 256, BF16, "rope"),
         ("wv", 256, BF16, "plain"), ("dq", 512, BF16, "ropeq"), ("dk", 512, BF16, "rope"),
         ("dv", 512, BF16, "plain"), ("gate", 3072, BF16, "plain"))
_EXT_WIDTH = sum(s[1] for s in _SEGS)
_COL_CHUNK = 512


def _inproj_kernel(x_ref, mod_ref, g_ref, cos_ref, sa_ref, sb_ref, w_ref, *out_refs, d, use_rope):
    x = x_ref[...]
    mod = mod_ref[0]
    h = _modnorm(x, g_ref[...], mod[:, 0:d], mod[:, d:2 * d]).astype(BF16)
    if use_rope:
        cos, sa, sb = cos_ref[...], sa_ref[...], sb_ref[...]
    off = 0
    for (name, width, dt, kind), o_ref in zip(_SEGS, out_refs):
        for c0 in range(0, width, _COL_CHUNK):
            cw = min(_COL_CHUNK, width - c0)
            acc = _dot(h, w_ref[:, off + c0:off + c0 + cw])
            if kind != "plain":
                tiles = []
                for t in range(cw // LANES):
                    a = acc[:, t * LANES:(t + 1) * LANES]
                    if use_rope:
                        a = (a * cos + pltpu.roll(a, LANES - 16, 1) * sa + pltpu.roll(a, 16, 1) * sb)
                    if kind == "ropeq":
                        a = a * _Q_SCALE
                    tiles.append(a)
                acc = jnp.concatenate(tiles, axis=1) if len(tiles) > 1 else tiles[0]
            o_ref[:, c0:c0 + cw] = acc.astype(dt)
        off += width


def _inproj(x2, mod, g, rope, w_ext, *, tm, rows_per_mod, seq, use_rope):
    N, D = x2.shape
    nt = seq // tm
    kern = functools.partial(_inproj_kernel, d=D, use_rope=use_rope)
    rope_spec = pl.BlockSpec((tm, LANES), lambda i: (i % nt, 0))
    out_shape = [jax.ShapeDtypeStruct((N, w), dt) for _, w, dt, _ in _SEGS]
    out_specs = [pl.BlockSpec((tm, w), lambda i: (i, 0)) for _, w, _, _ in _SEGS]
    return pl.pallas_call(
        kern,
        grid=(N // tm,),
        in_specs=[pl.BlockSpec((tm, D), lambda i: (i, 0)),
                  pl.BlockSpec((1, 1, 6 * D), lambda i: ((i * tm) // rows_per_mod, 0, 0)),
                  pl.BlockSpec((1, D), lambda i: (0, 0)),
                  rope_spec, rope_spec, rope_spec,
                  _resident((D, _EXT_WIDTH), lambda i: (0, 0))],
        out_specs=out_specs,
        out_shape=out_shape,
        compiler_params=_params(("arbitrary",)),
        name="inproj_lat" if use_rope else "inproj_ctx",
    )(x2, mod, g, *rope, w_ext)


def _extend_w_in(w):
    D = w.shape[0]
    o = np.cumsum([0, 512, 512, 512, 512, 16, 512, 128, 128, 512, 512, 512, 3072])
    p = [w[:, o[i]:o[i + 1]] for i in range(12)]
    dup = lambda a: jnp.concatenate([a[:, 0:64], a[:, 0:64], a[:, 64:128], a[:, 64:128]], axis=1)
    mg = jnp.concatenate([p[4], jnp.zeros((D, LANES - 16), w.dtype)], axis=1)
    ext = jnp.concatenate([p[0], p[1], p[2], p[3], mg, p[5], dup(p[6]), dup(p[7]), p[8], p[9], p[10], p[11]], axis=1)
    return ext.astype(BF16)


def _rope_tables(n_tokens):
    rows = n_tokens // GRID_W
    r, col = jnp.meshgrid(jnp.arange(rows), jnp.arange(GRID_W), indexing="ij")
    half = ROPE_DIM // 2
    inv = ROPE_BASE ** (-jnp.arange(0, half, 2, dtype=F32) / half)
    ang_r = r.reshape(-1, 1).astype(F32) * inv
    ang_c = col.reshape(-1, 1).astype(F32) * inv
    ang = jnp.concatenate([ang_r, ang_r, ang_c, ang_c], axis=-1)
    cos, sin = jnp.cos(ang), jnp.sin(ang)
    cos2 = jnp.concatenate([cos, cos], axis=-1)
    sin2 = jnp.concatenate([sin, sin], axis=-1)
    first = (jnp.arange(LANES) % 32) < 16
    sin_a = jnp.where(first, -sin2, 0.0)
    sin_b = jnp.where(first, 0.0, sin2)
    return cos2, sin_a, sin_b


def _scan_lanes(x, op, fill, reverse):
    lane = lax.broadcasted_iota(jnp.int32, x.shape, 1)
    d = 1
    while d < LANES:
        if reverse:
            shifted = jnp.where(lane < LANES - d, pltpu.roll(x, LANES - d, 1), fill)
        else:
            shifted = jnp.where(lane >= d, pltpu.roll(x, d, 1), fill)
        x = op(x, shifted)
        d *= 2
    return x


def _gates_kernel(x_ref, bias_ref, o_ref, *, rows):
    x = x_ref[0] + bias_ref[...]
    lane = lax.broadcasted_iota(jnp.int32, (rows, LANES), 1)
    for d, reverse in enumerate((False, True)):
        i_pre = x[(2 * d) * rows:(2 * d + 1) * rows]
        f_pre = x[(2 * d + 1) * rows:(2 * d + 2) * rows]
        log_f = jax.nn.log_sigmoid(f_pre)
        b = _scan_lanes(log_f, jnp.add, 0.0, reverse)
        a = i_pre - b
        pm = _scan_lanes(a, jnp.maximum, -jnp.inf, reverse)
        last = 0 if reverse else LANES - 1
        b_end = jnp.sum(jnp.where(lane == last, b, 0.0), axis=1, keepdims=True)
        a_max = jnp.max(a, axis=1, keepdims=True)
        o_ref[0, 5 * d + 0] = b
        o_ref[0, 5 * d + 1] = a
        o_ref[0, 5 * d + 2] = pm
        o_ref[0, 5 * d + 3] = jnp.broadcast_to(b_end, (rows, LANES))
        o_ref[0, 5 * d + 4] = jnp.broadcast_to(a_max, (rows, LANES))


def _gates(mg, b_gate, B, T):
    nc = T // CHUNK
    rows = M_HEADS * nc
    g = mg[:, :4 * M_HEADS].reshape(B, T, 4 * M_HEADS)
    gt = jnp.transpose(g, (0, 2, 1)).reshape(B, 4 * rows, LANES)
    bias = jnp.repeat(b_gate.astype(F32), nc).reshape(4 * rows, 1)
    return pl.pallas_call(
        functools.partial(_gates_kernel, rows=rows),
        grid=(B,),
        in_specs=[pl.BlockSpec((1, 4 * rows, LANES), lambda b: (b, 0, 0)),
                  pl.BlockSpec((4 * rows, 1), lambda b: (0, 0))],
        out_specs=pl.BlockSpec((1, 10, rows, LANES), lambda b: (b, 0, 0, 0)),
        out_shape=jax.ShapeDtypeStruct((B, 10, rows, LANES), F32),
        compiler_params=_params(("arbitrary",)),
        name="mlstm_gates",
    )(gt, bias)


_CONV_TILE = 256


def _mlstm_kernel(qc_ref, kc_ref, vc_ref, oc_ref, pc_ref, ql_ref, kl_ref, vl_ref, ol_ref, pl_ref,
                  cwq_ref, cwk_ref, nrm_ref, *rest, tc, tl, need_ctx):
    if need_ctx:
        yl_ref, yc_ref = rest[0], rest[1]
        scr = rest[2:]
    else:
        yl_ref, yc_ref = rest[0], None
        scr = rest[1:]
    xpad, qs, ks, hf, hb, cx, mst = scr
    head = pl.program_id(1)
    tot = tc + tl

    def conv_stream(u_ref, w_ref, dst, dst_off, ts, scale):
        xpad[0:8, :] = jnp.zeros((8, LANES), F32)
        xpad[8:8 + ts, :] = u_ref[0].astype(F32)
        xpad[8 + ts:16 + ts, :] = jnp.zeros((8, LANES), F32)
        w = w_ref[...]
        w0, w1, w2 = w[0:1], w[1:2], w[2:3]
        tile = min(_CONV_TILE, ts)

        def body(i, _):
            r0 = pl.multiple_of(i * tile, 8)
            prev = xpad[pl.ds(r0 + 7, tile), :]
            cur = xpad[pl.ds(r0 + 8, tile), :]
            nxt = xpad[pl.ds(r0 + 9, tile), :]
            y = prev * w0 + cur * w1 + nxt * w2
            y = y * jax.nn.sigmoid(y) * scale
            dst[pl.ds(pl.multiple_of(dst_off + r0, 8), tile), :] = y.astype(BF16)
            return 0

        lax.fori_loop(0, ts // tile, body, 0)

    kscale = M_DIM ** -0.5
    conv_stream(qc_ref, cwq_ref, qs, 0, tc, 1.0)
    conv_stream(kc_ref, cwk_ref, ks, 0, tc, kscale)
    conv_stream(ql_ref, cwq_ref, qs, tc, tl, 1.0)
    conv_stream(kl_ref, cwk_ref, ks, tc, tl, kscale)

    cx[...] = jnp.zeros(cx.shape, F32)
    mst[...] = jnp.zeros(mst.shape, F32)

    row_i = lax.broadcasted_iota(jnp.int32, (CHUNK, CHUNK), 0)
    col_i = lax.broadcasted_iota(jnp.int32, (CHUNK, CHUNK), 1)
    masks =(col_i <= row_i, col_i >= row_i)
    ones_b = jnp.ones((CHUNK, LANES), BF16)
    ones_f = jnp.ones((LANES, CHUNK), F32)

    def colify(row):
        return jnp.broadcast_to(row, (CHUNK, CHUNK)).T

    def chunk_step(d, c, p_ref, v_ref, nc, base, want_h):
        r = head * nc + c
        plane = lambda j: p_ref[0, 5 * d + j, pl.ds(r, 1), :]
        b_row, a_row, pm_row, bend_row, amax_row = (plane(j) for j in range(5))
        m_prev = mst[d:d + 1, :]
        m_end = jnp.maximum(m_prev, amax_row)
        w_end = jnp.exp(a_row - m_end)
        decay = jnp.exp(m_prev - m_end)
        row0 = pl.multiple_of(base + c * CHUNK, CHUNK)
        q = qs[pl.ds(row0, CHUNK), :]
        k = ks[pl.ds(row0, CHUNK), :]
        v = v_ref[0, pl.ds(pl.multiple_of(c * CHUNK, CHUNK), CHUNK), :]
        c_old = cx[d]
        if want_h:
            m_col = jnp.maximum(m_prev, colify(pm_row))
            w_intra = jnp.exp(jnp.where(masks[d], a_row - m_col, -jnp.inf))
            w_inter = jnp.exp(m_prev - m_col)
            floor = jnp.exp(-(colify(b_row) + m_col))
            s = _dot_nt(q, k) * w_intra
            intra = _dot(s.astype(BF16), jnp.concatenate([v, ones_b], axis=1))
            inter = _dot_nt(q, c_old.astype(BF16))
            num = intra[:, 0:M_DIM] + w_inter * inter[:, 0:M_DIM]
            den = intra[:, M_DIM:2 * M_DIM] + w_inter * inter[:, M_DIM:2 * M_DIM]
            hval = num / jnp.maximum(jnp.abs(den), floor)
            dst = hf if d == 0 else hb
            dst[pl.ds(row0, CHUNK), :] = hval
        vxt = jnp.concatenate([v.astype(F32).T, ones_f], axis=0)
        cx[d] = decay * c_old + _dot((vxt * w_end).astype(BF16), k)
        mst[d:d + 1, :] = bend_row + m_end

    ncc, ncl = tc // CHUNK, tl // CHUNK
    for c in range(ncc):
        chunk_step(0, c, pc_ref, vc_ref, ncc, 0, need_ctx)
        chunk_step(1, ncc - 1 - c, pc_ref, vc_ref, ncc, 0, need_ctx)

    def lat_body(c, _):
        chunk_step(0, c, pl_ref, vl_ref, ncl, tc, True)
        chunk_step(1, ncl - 1 - c, pl_ref, vl_ref, ncl, tc, True)
        return 0

    lax.fori_loop(0, ncl, lat_body, 0, unroll=8)

    nrm = nrm_ref[...]

    def finish(o_ref, y_ref, base, ts):
        def body(i, _):
            r0 = pl.multiple_of(i * CHUNK, CHUNK)
            hsum = hf[pl.ds(base + r0, CHUNK), :] + hb[pl.ds(base + r0, CHUNK), :]
            y = hsum * jax.nn.sigmoid(o_ref[0, pl.ds(r0, CHUNK), :].astype(F32))
            ms = jnp.mean(y * y, axis=-1, keepdims=True)
            y_ref[0, pl.ds(r0, CHUNK), :] = (y * lax.rsqrt(ms + EPS) * nrm).astype(BF16)
            return 0

        lax.fori_loop(0, ts // CHUNK, body, 0, unroll=min(4, ts // CHUNK))

    finish(ol_ref, yl_ref, tc, tl)
    if need_ctx:
        finish(oc_ref, yc_ref, 0, tc)


def _mlstm(ctx_p, lat_p, gates_c, gates_l, conv_w, m_norm, B, tc, tl, need_ctx):
    r3 = lambda a, t: a.reshape(B, t, a.shape[-1])
    qk_c, v_c, o_c = r3(ctx_p["mqk"], tc), r3(ctx_p["mv"], tc), r3(ctx_p["mo"], tc)
    qk_l, v_l, o_l = r3(lat_p["mqk"], tl), r3(lat_p["mv"], tl), r3(lat_p["mo"], tl)
    H = M_HEADS
    head_blk = lambda t, off: pl.BlockSpec((1, t, LANES), lambda b, h: (b, 0, h + off))
    gate_blk = lambda g: pl.BlockSpec((1,) + g.shape[1:], lambda b, h: (b, 0, 0, 0))
    in_specs = [head_blk(tc, 0), head_blk(tc, H), head_blk(tc, 0), head_blk(tc, 0), gate_blk(gates_c),
                head_blk(tl, 0), head_blk(tl, H), head_blk(tl, 0), head_blk(tl, 0), gate_blk(gates_l),
                pl.BlockSpec((3, LANES), lambda b, h: (0, h)),
                pl.BlockSpec((3, LANES), lambda b, h: (0, h + H)),
                pl.BlockSpec((1, LANES), lambda b, h: (0, h))]
    out_shape = [jax.ShapeDtypeStruct((B, tl, BRANCH), BF16)]
    out_specs = [head_blk(tl, 0)]
    if need_ctx:
        out_shape.append(jax.ShapeDtypeStruct((B, tc, BRANCH), BF16))
        out_specs.append(head_blk(tc, 0))
    tot = tc + tl
    scratch = [pltpu.VMEM((max(tc, tl) + 16, LANES), F32),
               pltpu.VMEM((tot, LANES), BF16), pltpu.VMEM((tot, LANES), BF16),
               pltpu.VMEM((tot, LANES), F32), pltpu.VMEM((tot, LANES), F32),
               pltpu.VMEM((2, 2 * LANES, LANES), F32), pltpu.VMEM((8, LANES), F32)]
    outs = pl.pallas_call(
        functools.partial(_mlstm_kernel, tc=tc, tl=tl, need_ctx=need_ctx),
        grid=(B, H),
        in_specs=in_specs,
        out_specs=out_specs,
        out_shape=out_shape,
        scratch_shapes=scratch,
        compiler_params=_params(("arbitrary", "arbitrary")),
        name="mlstm",
    )(qk_c, qk_c, v_c, o_c, gates_c, qk_l, qk_l, v_l, o_l, gates_l,
      conv_w, conv_w, m_norm.reshape(1, -1))
    ya = outs[0].reshape(B * tl, BRANCH)
    ya_c = outs[1].reshape(B * tc, BRANCH) if need_ctx else None
    return ya, ya_c


_WB = 128


def _win_kernel(sink_ref, q_ref, *refs, has_latent, qb, seq):
    if has_latent:
        kl_ref, vl_ref, kx_ref, vx_ref, o_ref = refs
    else:
        kx_ref, vx_ref, o_ref = refs
    n = pl.program_id(1)
    rep = W_HEADS // W_KV_HEADS
    rows = rep * _WB
    span = 3 * _WB
    lane = lax.broadcasted_iota(jnp.int32, (_WB, LANES), 1)
    left = lane < W_DIM
    zero = jnp.zeros((_WB, LANES), BF16)
    if has_latent:
        qq = jnp.bitwise_and(lax.broadcasted_iota(jnp.int32, (rows, span), 0), _WB - 1)
        dist = lax.broadcasted_iota(jnp.int32, (rows, span), 1) - qq
    windows = {}

    def window_of(j):
        if j not in windows:
            q_start = (n * qb + j) * _WB
            k_start = pl.multiple_of(jnp.clip(q_start - _WB, 0, seq - span), _WB)
            off = pltpu.bitcast(dist + (k_start - q_start + _WB), jnp.uint32)
            windows[j] = (k_start, off <= jnp.uint32(2 * _WB))
        return windows[j]

    def logits_of(j, g):
        r0 = j * _WB
        pieces = []
        for t in range(rep // 2):
            qt = q_ref[0, r0:r0 + _WB, (g * rep // 2 + t) * LANES:(g * rep // 2 + t + 1) * LANES]
            pieces += [jnp.where(left, qt, zero), jnp.where(left, zero, qt)]
        lhs = jnp.concatenate(pieces, axis=0)
        gsl = slice(g * LANES, (g + 1) * LANES)
        logits = [_dot_nt(lhs, kx_ref[0, :, gsl])]
        vals = [vx_ref[0, :, gsl]]
        if has_latent:
            k_start, near = window_of(j)
            s_win = _dot_nt(lhs, kl_ref[0, pl.ds(k_start, span), gsl])
            logits.append(jnp.where(near, s_win, -jnp.inf))
            vals.append(vl_ref[0, pl.ds(k_start, span), gsl])
        return logits, vals

    def finish(j, g, logits, vals):
        r0 = j * _WB
        sink_rep = jnp.concatenate(
            [jnp.full((_WB, LANES), sink_ref[g * rep + r] * LOG2E, F32) for r in range(rep)], axis=0)
        m_part = None
        for s in logits:
            for t in range(s.shape[1] // LANES):
                st = s[:, t * LANES:(t + 1) * LANES]
                m_part = st if m_part is None else jnp.maximum(m_part, st)
        m = jnp.maximum(sink_rep, jnp.max(m_part, axis=-1, keepdims=True))
        acc = None
        for s, v in zip(logits, vals):
            p = jnp.concatenate([jnp.exp2(s[:, t * LANES:(t + 1) * LANES] - m)
                                 for t in range(s.shape[1] // LANES)], axis=1)
            part = _dot(p.astype(BF16), jnp.concatenate([v, jnp.ones(v.shape, BF16)], axis=1))
            acc = part if acc is None else acc + part
        o = acc[:, 0:LANES] / (acc[:, LANES:2 * LANES] + jnp.exp2(sink_rep - m))
        for t in range(rep // 2):
            tile = jnp.where(left, o[(2 * t) * _WB:(2 * t + 1) * _WB], o[(2 * t + 1) * _WB:(2 * t + 2) * _WB])
            c0 = (g * rep // 2 + t) * LANES
            o_ref[0, r0:r0 + _WB, c0:c0 + LANES] = tile.astype(BF16)

    streams = [(j, g) for j in range(qb) for g in range(W_KV_HEADS)]
    ahead = 3
    pending = []
    for i, (j, g) in enumerate(streams):
        pending.append((j, g) + logits_of(j, g))
        if i >= ahead:
            finish(*pending.pop(0))
    for item in pending:
        finish(*item)


def _window_attention(q, kx, vx, sink, B, tq, tcx, lat_kv=None, qb=4):
    q3 = q.reshape(B, tq, BRANCH)
    kx3, vx3 = kx.reshape(B, tcx, 2 * LANES), vx.reshape(B, tcx, 2 * LANES)
    nb = tq // _WB
    qb = math.gcd(qb, nb)
    has_latent = lat_kv is not None
    whole = lambda t: pl.BlockSpec((1, t, 2 * LANES), lambda b, n: (b, 0, 0))
    in_specs = [pl.BlockSpec(memory_space=pltpu.SMEM),
                pl.BlockSpec((1, qb * _WB, BRANCH), lambda b, n: (b, n, 0))]
    args = [sink.astype(F32), q3]
    if has_latent:
        assert tq >= 3 * _WB
        in_specs += [whole(tq), whole(tq)]
        args += [a.reshape(B, tq, 2 * LANES) for a in lat_kv]
    in_specs += [whole(tcx), whole(tcx)]
    args += [kx3, vx3]
    out = pl.pallas_call(
        functools.partial(_win_kernel, has_latent=has_latent, qb=qb, seq=tq),
        grid=(B, nb // qb),
        in_specs=in_specs,
        out_specs=pl.BlockSpec((1, qb * _WB, BRANCH), lambda b, n: (b, n, 0)),
        out_shape=jax.ShapeDtypeStruct((B, tq, BRANCH), BF16),
        compiler_params=_params(("arbitrary", "arbitrary")),
        name="win_lat" if has_latent else "win_ctx",
    )(*args)
    return out.reshape(B * tq, BRANCH)


def _diff_kernel(lam_ref, q_ref, *refs, tq, tk, tcx, tlat, lam_init):
    if tlat:
        kx_ref, vx_ref, kl_ref, vl_ref, nrm_ref, o_ref = refs
    else:
        kx_ref, vx_ref, nrm_ref, o_ref = refs
        kl_ref = vl_ref = None
    q = q_ref[0]
    lane = lax.broadcasted_iota(jnp.int32, (tq, LANES), 1)
    left = lane < LANES // 2
    zero = jnp.zeros((tq, LANES), BF16)
    tiles = [(kx_ref, vx_ref, 0, tcx)] + [(kl_ref, vl_ref, j * tk, tk) for j in range(tlat // tk)]
    ones = jnp.ones((max(tk, tcx), LANES), BF16)

    lhs = (jnp.where(left, q, zero), jnp.where(left, zero, q))
    state = [None, None]
    for k_ref, v_ref, r0, n in tiles:
        vx = jnp.concatenate([v_ref[0, r0:r0 + n, :], ones[0:n]], axis=1)
        k = k_ref[0, r0:r0 + n, :]
        for comp in range(2):
            s = _dot_nt(lhs[comp], k)
            m_part = s[:, 0:LANES]
            for t in range(1, n // LANES):
                m_part = jnp.maximum(m_part, s[:, t * LANES:(t + 1) * LANES])
            m_j = jnp.max(m_part, axis=-1, keepdims=True)
            part = _dot(jnp.exp2(s - m_j).astype(BF16), vx)
            if state[comp] is None:
                state[comp] = (m_j, part)
            else:
                m_run, acc = state[comp]
                m_new = jnp.maximum(m_run, m_j)
                state[comp] = (m_new, acc * jnp.exp2(m_run - m_new) + part * jnp.exp2(m_j - m_new))
    outs = [acc[:, 0:LANES] / acc[:, LANES:2 * LANES] for _, acc in state]
    lv = lam_ref[...]
    s1 = jnp.sum(lv[0:1] * lv[1:2], axis=-1, keepdims=True)
    s2 = jnp.sum(lv[2:3] * lv[3:4], axis=-1, keepdims=True)
    lam = jnp.exp(s1) - jnp.exp(s2) + lam_init
    y = outs[0] - lam * outs[1]
    ms = jnp.mean(y * y, axis=-1, keepdims=True)
    o_ref[0] = (y * lax.rsqrt(ms + EPS) * nrm_ref[...] * (1.0 - lam_init)).astype(BF16)


def _diff_attention(q, kx, vx, lamv, d_norm, lam_init, B, tq_total, tcx, lat_kv=None, tq=1024, tk=1024):
    q3 = q.reshape(B, tq_total, BRANCH)
    kx3, vx3 = kx.reshape(B, tcx, BRANCH), vx.reshape(B, tcx, BRANCH)
    tq = min(tq, tq_total)
    head_blk = lambda t, f: pl.BlockSpec((1, t, LANES), f)
    whole = lambda b, h, i: (b, 0, h)
    in_specs = [pl.BlockSpec((8, LANES), lambda b, h, i: (0, 0)),
                head_blk(tq, lambda b, h, i: (b, i, h)),
                head_blk(tcx, whole), head_blk(tcx, whole)]
    args = [lamv, q3, kx3, vx3]
    tlat = 0
    if lat_kv is not None:
        tlat = tq_total
        k3, v3 = (a.reshape(B, tlat, BRANCH) for a in lat_kv)
        in_specs += [head_blk(tlat, whole), head_blk(tlat, whole)]
        args += [k3, v3]
    in_specs.append(pl.BlockSpec((1, LANES), lambda b, h, i: (0, h)))
    args.append(d_norm.reshape(1, -1))
    tk = min(tk, max(tlat, LANES))
    out = pl.pallas_call(
        functools.partial(_diff_kernel, tq=tq, tk=tk, tcx=tcx, tlat=tlat, lam_init=lam_init),
        grid=(B, DF_HEADS, tq_total // tq),
        in_specs=in_specs,
        out_specs=head_blk(tq, lambda b, h, i: (b, i, h)),
        out_shape=jax.ShapeDtypeStruct((B, tq_total, BRANCH), BF16),
        compiler_params=_params(("arbitrary", "arbitrary", "arbitrary")),
        name="diff_lat" if tlat else "diff_ctx",
    )(*args)
    return out.reshape(B * tq_total, BRANCH)


def _merge_kernel(ya_ref, yb_ref, yc_ref, gate_ref, x_ref, mod_ref, wb_ref, wo_ref, o_ref, *, d):
    merged = None
    for i, y_ref in enumerate((ya_ref, yb_ref, yc_ref)):
        gate = jax.nn.sigmoid(gate_ref[:, i * d:(i + 1) * d].astype(F32))
        term = gate * _dot(y_ref[...], wb_ref[i])
        merged = term if merged is None else merged + term
    out = _dot(merged.astype(BF16), wo_ref[...])
    gt = mod_ref[0][:, 2 * d:3 * d]
    o_ref[...] = x_ref[...] + gt * out


def _merge(ya, yb, yc, gate, x2, mod, wb, wo, *, tm, rows_per_mod):
    N, D = x2.shape
    row = lambda w: pl.BlockSpec((tm, w), lambda i: (i, 0))
    return pl.pallas_call(
        functools.partial(_merge_kernel, d=D),
        grid=(N // tm,),
        in_specs=[row(BRANCH), row(BRANCH), row(BRANCH), row(N_BRANCH * D), row(D),
                  pl.BlockSpec((1, 1, 6 * D), lambda i: ((i * tm) // rows_per_mod, 0, 0)),
                  _resident((N_BRANCH, BRANCH, D), lambda i: (0, 0, 0)),
                  _resident((D, D), lambda i: (0, 0))],
        out_specs=row(D),
        out_shape=jax.ShapeDtypeStruct((N, D), F32),
        compiler_params=_params(("arbitrary",)),
        name="merge",
    )(ya, yb, yc, gate, x2, mod, wb, wo)


_FFN_CHUNK = 256


def _ffn_kernel(x_ref, mod_ref, g_ref, wi_ref, wo_ref, gf_ref, o_ref, *, d, hidden, final_norm):
    x = x_ref[...]
    mod = mod_ref[0]
    h = _modnorm(x, g_ref[...], mod[:, 3 * d:4 * d], mod[:, 4 * d:5 * d]).astype(BF16)
    acc = None
    for c0 in range(0, hidden, _FFN_CHUNK):
        gate = _dot(h, wi_ref[:, c0:c0 + _FFN_CHUNK])
        up = _dot(h, wi_ref[:, hidden + c0:hidden + c0 + _FFN_CHUNK])
        a = (gate * jax.nn.sigmoid(gate) * up).astype(BF16)
        part = _dot(a, wo_ref[c0:c0 + _FFN_CHUNK, :])
        acc = part if acc is None else acc + part
    y = x + mod[:, 5 * d:6 * d] * acc
    if final_norm:
        ms = jnp.mean(y * y, axis=-1, keepdims=True)
        y = y * lax.rsqrt(ms + EPS) * gf_ref[...]
    o_ref[...] = y


def _ffn(x2, mod, g, wi, wo, g_final, *, tm, rows_per_mod, final_norm):
    N, D = x2.shape
    hidden = wo.shape[0]
    return pl.pallas_call(
        functools.partial(_ffn_kernel, d=D, hidden=hidden, final_norm=final_norm),
        grid=(N // tm,),
        in_specs=[pl.BlockSpec((tm, D), lambda i: (i, 0)),
                  pl.BlockSpec((1, 1, 6 * D), lambda i: ((i * tm) // rows_per_mod, 0, 0)),
                  pl.BlockSpec((1, D), lambda i: (0, 0)),
                  _resident((D, 2 * hidden), lambda i: (0, 0)),
                  _resident((hidden, D), lambda i: (0, 0)),
                  pl.BlockSpec((1, D), lambda i: (0, 0))],
        out_specs=pl.BlockSpec((tm, D), lambda i: (i, 0)),
        out_shape=jax.ShapeDtypeStruct((N, D), F32),
        compiler_params=_params(("arbitrary",)),
        name="ffn",
    )(x2, mod, g, wi, wo, g_final)


def _row_tile(n, want=512):
    t = want
    while n % t:
        t //= 2
    return t


def kernel(x, c, ctx, c_ctx, w_mod, b_mod, g_mix, g_ffn, w_in, b_gate, conv_w, m_norm, sink,
           lam_q1, lam_k1, lam_q2, lam_k2, d_norm, w_branch, w_out, w_ffn_in, w_ffn_out, g_final):
    B, T, D = x.shape
    Tc = ctx.shape[1]
    depth = w_mod.shape[0]
    assert T % CHUNK == 0 and Tc % CHUNK == 0 and T % GRID_W == 0

    n_rows = -(-(B + 1) // 16) * 16
    cc = jnp.concatenate([c, c_ctx[None, :], jnp.zeros((n_rows - B - 1, D), F32)], axis=0)
    mods = _adaln(cc, w_mod, b_mod)

    rope = _rope_tables(T)
    no_rope = tuple(jnp.zeros((Tc, LANES), F32) for _ in range(3))
    tm = _row_tile(T)
    tmc = _row_tile(Tc)
    names = [s[0] for s in _SEGS]

    xl = x.reshape(B * T, D)
    xc = ctx.reshape(B * Tc, D)
    for l in range(depth):
        need_ctx = l < depth - 1
        lam_init = 0.8 - 0.6 * math.exp(-0.3 * l)
        mod_l = mods[l, :B].reshape(B, 1, 6 * D)
        mod_c = mods[l, B:B + 1].reshape(1, 1, 6 * D)
        w_ext = _extend_w_in(w_in[l])
        g_mix_l = g_mix[l].reshape(1, D)
        g_ffn_l = g_ffn[l].reshape(1, D)
        lamv = jnp.zeros((8, LANES), F32)
        for i, v in enumerate((lam_q1[l], lam_k1[l], lam_q2[l], lam_k2[l])):
            lamv = lamv.at[i, :v.shape[0]].set(v.astype(F32))

        pl_ = dict(zip(names, _inproj(xl, mod_l, g_mix_l, rope, w_ext, tm=tm, rows_per_mod=T, seq=T, use_rope=True)))
        pc_ = dict(zip(names, _inproj(xc, mod_c, g_mix_l, no_rope, w_ext, tm=tmc, rows_per_mod=B * Tc, seq=Tc,
                                      use_rope=False)))

        gates_l = _gates(pl_["mg"], b_gate[l], B, T)
        gates_c = _gates(pc_["mg"], b_gate[l], B, Tc)
        ya, ya_c = _mlstm(pc_, pl_, gates_c, gates_l, conv_w[l], m_norm[l], B, Tc, T, need_ctx)
        yb = _window_attention(pl_["wq"], pc_["wk"], pc_["wv"], sink[l], B, T, Tc, lat_kv=(pl_["wk"], pl_["wv"]))
        yc = _diff_attention(pl_["dq"], pc_["dk"], pc_["dv"], lamv, d_norm[l], lam_init, B, T, Tc,
                             lat_kv=(pl_["dk"], pl_["dv"]))
        wb = w_branch[l].astype(BF16)
        wo = w_out[l].astype(BF16)
        wfi = w_ffn_in[l].astype(BF16)
        wfo = w_ffn_out[l].astype(BF16)
        gfin = g_final.reshape(1, D)
        last = l == depth - 1
        xl = _merge(ya, yb, yc, pl_["gate"], xl, mod_l, wb, wo, tm=tm, rows_per_mod=T)
        xl = _ffn(xl, mod_l, g_ffn_l, wfi, wfo, gfin, tm=tm, rows_per_mod=T, final_norm=last)
        if need_ctx:
            yb_c = _window_attention(pc_["wq"], pc_["wk"], pc_["wv"], sink[l], B, Tc, Tc)
            yc_c = _diff_attention(pc_["dq"], pc_["dk"], pc_["dv"], lamv, d_norm[l], lam_init, B, Tc, Tc)
            xc = _merge(ya_c, yb_c, yc_c, pc_["gate"], xc, mod_c, wb, wo, tm=tmc, rows_per_mod=B * Tc)
            xc = _ffn(xc, mod_c, g_ffn_l, wfi, wfo, gfin, tm=tmc, rows_per_mod=B * Tc, final_norm=False)
    return xl.reshape(B, T, D)
```

```python
import functools
import math

import jax
import jax.numpy as jnp
import numpy as np
from jax import lax
from jax.experimental import pallas as pl
from jax.experimental.pallas import tpu as pltpu

F32 = jnp.float32
BF16 = jnp.bfloat16

GRID_W = 64
BRANCH = 512
N_BRANCH = 3
M_HEADS = 4
M_DIM = 128
W_HEADS = 8
W_KV_HEADS = 2
W_DIM = 64
DF_HEADS = 4
ROPE_DIM = 64
ROPE_BASE = 10000.0
EPS = 1e-6
LANES = 128
CHUNK = 128
VMEM_LIMIT = 56 * 1024 * 1024

_NT = (((1,), (1,)), ((), ()))
LOG2E = math.log2(math.e)
_Q_SCALE = W_DIM ** -0.5 * LOG2E


def _dot(a, b):
    return jnp.dot(a, b, preferred_element_type=F32)


def _dot_nt(a, b):
    return lax.dot_general(a, b, _NT, preferred_element_type=F32)


def _params(sem):
    return pltpu.CompilerParams(dimension_semantics=sem, vmem_limit_bytes=VMEM_LIMIT)


def _resident(shape, index_map):
    return pl.BlockSpec(shape, index_map, pipeline_mode=pl.Buffered(1))


def _adaln_kernel(c_ref, w_ref, b_ref, o_ref):
    c = c_ref[...]
    s = (c * jax.nn.sigmoid(c)).astype(BF16)
    o_ref[0] = _dot(s, w_ref[0].astype(BF16)) + b_ref[0]


def _adaln(cc, w_mod, b_mod):
    L, D, N = w_mod.shape
    R = cc.shape[0]
    tn = 1536
    return pl.pallas_call(
        _adaln_kernel,
        grid=(L, N // tn),
        in_specs=[pl.BlockSpec((R, D), lambda l, j: (0, 0)),
                  pl.BlockSpec((1, D, tn), lambda l, j: (l, 0, j)),
                  pl.BlockSpec((1, 1, tn), lambda l, j: (l, 0, j))],
        out_specs=pl.BlockSpec((1, R, tn), lambda l, j: (l, 0, j)),
        out_shape=jax.ShapeDtypeStruct((L, R, N), F32),
        compiler_params=_params(("arbitrary", "arbitrary")),
        name="adaln",
    )(cc, w_mod, b_mod.reshape(L, 1, N))


def _modnorm(x, g, shift, scale):
    ms = jnp.mean(x * x, axis=-1, keepdims=True)
    return x * lax.rsqrt(ms + EPS) * (g * (1.0 + scale)) + shift


_SEGS = (("mqk", 1024, BF16, "plain"), ("mv", 512, BF16, "plain"), ("mo", 512, BF16, "plain"),
         ("mg", 128, F32, "plain"), ("wq", 512, BF16, "ropeq"), ("wk", 256, BF16, "rope"),
         ("wv", 256, BF16, "plain"), ("dq", 512, BF16, "ropeq"), ("dk", 512, BF16, "rope"),
         ("dv", 512, BF16, "plain"), ("gate", 3072, BF16, "plain"))
_EXT_WIDTH = sum(s[1] for s in _SEGS)
_COL_CHUNK = 512


def _inproj_kernel(x_ref, mod_ref, g_ref, cos_ref, sa_ref, sb_ref, w_ref, *out_refs, d, use_rope):
    x = x_ref[...]
    mod = mod_ref[0]
    h = _modnorm(x, g_ref[...], mod[:, 0:d], mod[:, d:2 * d]).astype(BF16)
    if use_rope:
        cos, sa, sb = cos_ref[...], sa_ref[...], sb_ref[...]
    off = 0
    for (name, width, dt, kind), o_ref in zip(_SEGS, out_refs):
        for c0 in range(0, width, _COL_CHUNK):
            cw = min(_COL_CHUNK, width - c0)
            acc = _dot(h, w_ref[:, off + c0:off + c0 + cw])
            if kind != "plain":
                tiles = []
                for t in range(cw // LANES):
                    a = acc[:, t * LANES:(t + 1) * LANES]
                    if use_rope:
                        a = (a * cos + pltpu.roll(a, LANES - 16, 1) * sa + pltpu.roll(a, 16, 1) * sb)
                    if kind == "ropeq":
                        a = a * _Q_SCALE
                    tiles.append(a)
                acc = jnp.concatenate(tiles, axis=1) if len(tiles) > 1 else tiles[0]
            o_ref[:, c0:c0 + cw] = acc.astype(dt)
        off += width


def _inproj(x2, mod, g, rope, w_ext, *, tm, rows_per_mod, seq, use_rope):
    N, D = x2.shape
    nt = seq // tm
    kern = functools.partial(_inproj_kernel, d=D, use_rope=use_rope)
    rope_spec = pl.BlockSpec((tm, LANES), lambda i: (i % nt, 0))
    out_shape = [jax.ShapeDtypeStruct((N, w), dt) for _, w, dt, _ in _SEGS]
    out_specs = [pl.BlockSpec((tm, w), lambda i: (i, 0)) for _, w, _, _ in _SEGS]
    return pl.pallas_call(
        kern,
        grid=(N // tm,),
        in_specs=[pl.BlockSpec((tm, D), lambda i: (i, 0)),
                  pl.BlockSpec((1, 1, 6 * D), lambda i: ((i * tm) // rows_per_mod, 0, 0)),
                  pl.BlockSpec((1, D), lambda i: (0, 0)),
                  rope_spec, rope_spec, rope_spec,
                  _resident((D, _EXT_WIDTH), lambda i: (0, 0))],
        out_specs=out_specs,
        out_shape=out_shape,
        compiler_params=_params(("arbitrary",)),
        name="inproj_lat" if use_rope else "inproj_ctx",
    )(x2, mod, g, *rope, w_ext)


def _extend_w_in(w):
    D = w.shape[0]
    o = np.cumsum([0, 512, 512, 512, 512, 16, 512, 128, 128, 512, 512, 512, 3072])
    p = [w[:, o[i]:o[i + 1]] for i in range(12)]
    dup = lambda a: jnp.concatenate([a[:, 0:64], a[:, 0:64], a[:, 64:128], a[:, 64:128]], axis=1)
    mg = jnp.concatenate([p[4], jnp.zeros((D, LANES - 16), w.dtype)], axis=1)
    ext = jnp.concatenate([p[0], p[1], p[2], p[3], mg, p[5], dup(p[6]), dup(p[7]), p[8], p[9], p[10], p[11]], axis=1)
    return ext.astype(BF16)


def _rope_tables(n_tokens):
    rows = n_tokens // GRID_W
    r, col = jnp.meshgrid(jnp.arange(rows), jnp.arange(GRID_W), indexing="ij")
    half = ROPE_DIM // 2
    inv = ROPE_BASE ** (-jnp.arange(0, half, 2, dtype=F32) / half)
    ang_r = r.reshape(-1, 1).astype(F32) * inv
    ang_c = col.reshape(-1, 1).astype(F32) * inv
    ang = jnp.concatenate([ang_r, ang_r, ang_c, ang_c], axis=-1)
    cos, sin = jnp.cos(ang), jnp.sin(ang)
    cos2 = jnp.concatenate([cos, cos], axis=-1)
    sin2 = jnp.concatenate([sin, sin], axis=-1)
    first = (jnp.arange(LANES) % 32) < 16
    sin_a = jnp.where(first, -sin2, 0.0)
    sin_b = jnp.where(first, 0.0, sin2)
    return cos2, sin_a, sin_b


def _scan_lanes(x, op, fill, reverse):
    lane = lax.broadcasted_iota(jnp.int32, x.shape, 1)
    d = 1
    while d < LANES:
        if reverse:
            shifted = jnp.where(lane < LANES - d, pltpu.roll(x, LANES - d, 1), fill)
        else:
            shifted = jnp.where(lane >= d, pltpu.roll(x, d, 1), fill)
        x = op(x, shifted)
        d *= 2
    return x


def _gates_kernel(x_ref, bias_ref, o_ref, *, rows):
    x = x_ref[0] + bias_ref[...]
    lane = lax.broadcasted_iota(jnp.int32, (rows, LANES), 1)
    for d, reverse in enumerate((False, True)):
        i_pre = x[(2 * d) * rows:(2 * d + 1) * rows]
        f_pre = x[(2 * d + 1) * rows:(2 * d + 2) * rows]
        log_f = jax.nn.log_sigmoid(f_pre)
        b = _scan_lanes(log_f, jnp.add, 0.0, reverse)
        a = i_pre - b
        pm = _scan_lanes(a, jnp.maximum, -jnp.inf, reverse)
        last = 0 if reverse else LANES - 1
        b_end = jnp.sum(jnp.where(lane == last, b, 0.0), axis=1, keepdims=True)
        a_max = jnp.max(a, axis=1, keepdims=True)
        o_ref[0, 5 * d + 0] = b
        o_ref[0, 5 * d + 1] = a
        o_ref[0, 5 * d + 2] = pm
        o_ref[0, 5 * d + 3] = jnp.broadcast_to(b_end, (rows, LANES))
        o_ref[0, 5 * d + 4] = jnp.broadcast_to(a_max, (rows, LANES))


def _gates(mg, b_gate, B, T):
    nc = T // CHUNK
    rows = M_HEADS * nc
    g = mg[:, :4 * M_HEADS].reshape(B, T, 4 * M_HEADS)
    gt = jnp.transpose(g, (0, 2, 1)).reshape(B, 4 * rows, LANES)
    bias = jnp.repeat(b_gate.astype(F32), nc).reshape(4 * rows, 1)
    return pl.pallas_call(
        functools.partial(_gates_kernel, rows=rows),
        grid=(B,),
        in_specs=[pl.BlockSpec((1, 4 * rows, LANES), lambda b: (b, 0, 0)),
                  pl.BlockSpec((4 * rows, 1), lambda b: (0, 0))],
        out_specs=pl.BlockSpec((1, 10, rows, LANES), lambda b: (b, 0, 0, 0)),
        out_shape=jax.ShapeDtypeStruct((B, 10, rows, LANES), F32),
        compiler_params=_params(("arbitrary",)),
        name="mlstm_gates",
    )(gt, bias)


_CONV_TILE = 256


def _mlstm_kernel(qc_ref, kc_ref, vc_ref, oc_ref, pc_ref, ql_ref, kl_ref, vl_ref, ol_ref, pl_ref,
                  cwq_ref, cwk_ref, nrm_ref, *rest, tc, tl, need_ctx):
    if need_ctx:
        yl_ref, yc_ref = rest[0], rest[1]
        scr = rest[2:]
    else:
        yl_ref, yc_ref = rest[0], None
        scr = rest[1:]
    xpad, qs, ks, hf, hb, cx, mst = scr
    head = pl.program_id(1)
    tot = tc + tl

    def conv_stream(u_ref, w_ref, dst, dst_off, ts, scale):
        xpad[0:8, :] = jnp.zeros((8, LANES), F32)
        xpad[8:8 + ts, :] = u_ref[0].astype(F32)
        xpad[8 + ts:16 + ts, :] = jnp.zeros((8, LANES), F32)
        w = w_ref[...]
        w0, w1, w2 = w[0:1], w[1:2], w[2:3]
        tile = min(_CONV_TILE, ts)

        def body(i, _):
            r0 = pl.multiple_of(i * tile, 8)
            prev = xpad[pl.ds(r0 + 7, tile), :]
            cur = xpad[pl.ds(r0 + 8, tile), :]
            nxt = xpad[pl.ds(r0 + 9, tile), :]
            y = prev * w0 + cur * w1 + nxt * w2
            y = y * jax.nn.sigmoid(y) * scale
            dst[pl.ds(pl.multiple_of(dst_off + r0, 8), tile), :] = y.astype(BF16)
            return 0

        lax.fori_loop(0, ts // tile, body, 0)

    kscale = M_DIM ** -0.5
    conv_stream(qc_ref, cwq_ref, qs, 0, tc, 1.0)
    conv_stream(kc_ref, cwk_ref, ks, 0, tc, kscale)
    conv_stream(ql_ref, cwq_ref, qs, tc, tl, 1.0)
    conv_stream(kl_ref, cwk_ref, ks, tc, tl, kscale)

    cx[...] = jnp.zeros(cx.shape, F32)
    mst[...] = jnp.zeros(mst.shape, F32)

    row_i = lax.broadcasted_iota(jnp.int32, (CHUNK, CHUNK), 0)
    col_i = lax.broadcasted_iota(jnp.int32, (CHUNK, CHUNK), 1)
    masks =(col_i <= row_i, col_i >= row_i)
    ones_b = jnp.ones((CHUNK, LANES), BF16)
    ones_f = jnp.ones((LANES, CHUNK), F32)

    def colify(row):
        return jnp.broadcast_to(row, (CHUNK, CHUNK)).T

    def chunk_step(d, c, p_ref, v_ref, nc, base, want_h):
        r = head * nc + c
        plane = lambda j: p_ref[0, 5 * d + j, pl.ds(r, 1), :]
        b_row, a_row, pm_row, bend_row, amax_row = (plane(j) for j in range(5))
        m_prev = mst[d:d + 1, :]
        m_end = jnp.maximum(m_prev, amax_row)
        w_end = jnp.exp(a_row - m_end)
        decay = jnp.exp(m_prev - m_end)
        row0 = pl.multiple_of(base + c * CHUNK, CHUNK)
        q = qs[pl.ds(row0, CHUNK), :]
        k = ks[pl.ds(row0, CHUNK), :]
        v = v_ref[0, pl.ds(pl.multiple_of(c * CHUNK, CHUNK), CHUNK), :]
        c_old = cx[d]
        if want_h:
            m_col = jnp.maximum(m_prev, colify(pm_row))
            w_intra = jnp.exp(jnp.where(masks[d], a_row - m_col, -jnp.inf))
            w_inter = jnp.exp(m_prev - m_col)
            floor = jnp.exp(-(colify(b_row) + m_col))
            s = _dot_nt(q, k) * w_intra
            intra = _dot(s.astype(BF16), jnp.concatenate([v, ones_b], axis=1))
            inter = _dot_nt(q, c_old.astype(BF16))
            num = intra[:, 0:M_DIM] + w_inter * inter[:, 0:M_DIM]
            den = intra[:, M_DIM:2 * M_DIM] + w_inter * inter[:, M_DIM:2 * M_DIM]
            hval = num / jnp.maximum(jnp.abs(den), floor)
            dst = hf if d == 0 else hb
            dst[pl.ds(row0, CHUNK), :] = hval
        vxt = jnp.concatenate([v.astype(F32).T, ones_f], axis=0)
        cx[d] = decay * c_old + _dot((vxt * w_end).astype(BF16), k)
        mst[d:d + 1, :] = bend_row + m_end

    ncc, ncl = tc // CHUNK, tl // CHUNK
    for c in range(ncc):
        chunk_step(0, c, pc_ref, vc_ref, ncc, 0, need_ctx)
        chunk_step(1, ncc - 1 - c, pc_ref, vc_ref, ncc, 0, need_ctx)

    def lat_body(c, _):
        chunk_step(0, c, pl_ref, vl_ref, ncl, tc, True)
        chunk_step(1, ncl - 1 - c, pl_ref, vl_ref, ncl, tc, True)
        return 0

    lax.fori_loop(0, ncl, lat_body, 0, unroll=8)

    nrm = nrm_ref[...]

    def finish(o_ref, y_ref, base, ts):
        def body(i, _):
            r0 = pl.multiple_of(i * CHUNK, CHUNK)
            hsum = hf[pl.ds(base + r0, CHUNK), :] + hb[pl.ds(base + r0, CHUNK), :]
            y = hsum * jax.nn.sigmoid(o_ref[0, pl.ds(r0, CHUNK), :].astype(F32))
            ms = jnp.mean(y * y, axis=-1, keepdims=True)
            y_ref[0, pl.ds(r0, CHUNK), :] = (y * lax.rsqrt(ms + EPS) * nrm).astype(BF16)
            return 0

        lax.fori_loop(0, ts // CHUNK, body, 0, unroll=min(4, ts // CHUNK))

    finish(ol_ref, yl_ref, tc, tl)
    if need_ctx:
        finish(oc_ref, yc_ref, 0, tc)


def _mlstm(ctx_p, lat_p, gates_c, gates_l, conv_w, m_norm, B, tc, tl, need_ctx):
    r3 = lambda a, t: a.reshape(B, t, a.shape[-1])
    qk_c, v_c, o_c = r3(ctx_p["mqk"], tc), r3(ctx_p["mv"], tc), r3(ctx_p["mo"], tc)
    qk_l, v_l, o_l = r3(lat_p["mqk"], tl), r3(lat_p["mv"], tl), r3(lat_p["mo"], tl)
    H = M_HEADS
    head_blk = lambda t, off: pl.BlockSpec((1, t, LANES), lambda b, h: (b, 0, h + off))
    gate_blk = lambda g: pl.BlockSpec((1,) + g.shape[1:], lambda b, h: (b, 0, 0, 0))
    in_specs = [head_blk(tc, 0), head_blk(tc, H), head_blk(tc, 0), head_blk(tc, 0), gate_blk(gates_c),
                head_blk(tl, 0), head_blk(tl, H), head_blk(tl, 0), head_blk(tl, 0), gate_blk(gates_l),
                pl.BlockSpec((3, LANES), lambda b, h: (0, h)),
                pl.BlockSpec((3, LANES), lambda b, h: (0, h + H)),
                pl.BlockSpec((1, LANES), lambda b, h: (0, h))]
    out_shape = [jax.ShapeDtypeStruct((B, tl, BRANCH), BF16)]
    out_specs = [head_blk(tl, 0)]
    if need_ctx:
        out_shape.append(jax.ShapeDtypeStruct((B, tc, BRANCH), BF16))
        out_specs.append(head_blk(tc, 0))
    tot = tc + tl
    scratch = [pltpu.VMEM((max(tc, tl) + 16, LANES), F32),
               pltpu.VMEM((tot, LANES), BF16), pltpu.VMEM((tot, LANES), BF16),
               pltpu.VMEM((tot, LANES), F32), pltpu.VMEM((tot, LANES), F32),
               pltpu.VMEM((2, 2 * LANES, LANES), F32), pltpu.VMEM((8, LANES), F32)]
    outs = pl.pallas_call(
        functools.partial(_mlstm_kernel, tc=tc, tl=tl, need_ctx=need_ctx),
        grid=(B, H),
        in_specs=in_specs,
        out_specs=out_specs,
        out_shape=out_shape,
        scratch_shapes=scratch,
        compiler_params=_params(("arbitrary", "arbitrary")),
        name="mlstm",
    )(qk_c, qk_c, v_c, o_c, gates_c, qk_l, qk_l, v_l, o_l, gates_l,
      conv_w, conv_w, m_norm.reshape(1, -1))
    ya = outs[0].reshape(B * tl, BRANCH)
    ya_c = outs[1].reshape(B * tc, BRANCH) if need_ctx else None
    return ya, ya_c


_WB = 128


def _win_kernel(sink_ref, q_ref, *refs, has_latent, qb, seq):
    if has_latent:
        kl_ref, vl_ref, kx_ref, vx_ref, o_ref = refs
    else:
        kx_ref, vx_ref, o_ref = refs
    n = pl.program_id(1)
    rep = W_HEADS // W_KV_HEADS
    rows = rep * _WB
    span = 3 * _WB
    lane = lax.broadcasted_iota(jnp.int32, (_WB, LANES), 1)
    left = lane < W_DIM
    zero = jnp.zeros((_WB, LANES), BF16)
    if has_latent:
        qq = jnp.bitwise_and(lax.broadcasted_iota(jnp.int32, (rows, span), 0), _WB - 1)
        dist = lax.broadcasted_iota(jnp.int32, (rows, span), 1) - qq
    windows = {}

    def window_of(j):
        if j not in windows:
            q_start = (n * qb + j) * _WB
            k_start = pl.multiple_of(jnp.clip(q_start - _WB, 0, seq - span), _WB)
            off = pltpu.bitcast(dist + (k_start - q_start + _WB), jnp.uint32)
            windows[j] = (k_start, off <= jnp.uint32(2 * _WB))
        return windows[j]

    def logits_of(j, g):
        r0 = j * _WB
        pieces = []
        for t in range(rep // 2):
            qt = q_ref[0, r0:r0 + _WB, (g * rep // 2 + t) * LANES:(g * rep // 2 + t + 1) * LANES]
            pieces += [jnp.where(left, qt, zero), jnp.where(left, zero, qt)]
        lhs = jnp.concatenate(pieces, axis=0)
        gsl = slice(g * LANES, (g + 1) * LANES)
        logits = [_dot_nt(lhs, kx_ref[0, :, gsl])]
        vals = [vx_ref[0, :, gsl]]
        if has_latent:
            k_start, near = window_of(j)
            s_win = _dot_nt(lhs, kl_ref[0, pl.ds(k_start, span), gsl])
            logits.append(jnp.where(near, s_win, -jnp.inf))
            vals.append(vl_ref[0, pl.ds(k_start, span), gsl])
        return logits, vals

    def finish(j, g, logits, vals):
        r0 = j * _WB
        sink_rep = jnp.concatenate(
            [jnp.full((_WB, LANES), sink_ref[g * rep + r] * LOG2E, F32) for r in range(rep)], axis=0)
        m_part = None
        for s in logits:
            for t in range(s.shape[1] // LANES):
                st = s[:, t * LANES:(t + 1) * LANES]
                m_part = st if m_part is None else jnp.maximum(m_part, st)
        m = jnp.maximum(sink_rep, jnp.max(m_part, axis=-1, keepdims=True))
        acc = None
        for s, v in zip(logits, vals):
            p = jnp.concatenate([jnp.exp2(s[:, t * LANES:(t + 1) * LANES] - m)
                                 for t in range(s.shape[1] // LANES)], axis=1)
            part = _dot(p.astype(BF16), jnp.concatenate([v, jnp.ones(v.shape, BF16)], axis=1))
            acc = part if acc is None else acc + part
        o = acc[:, 0:LANES] / (acc[:, LANES:2 * LANES] + jnp.exp2(sink_rep - m))
        for t in range(rep // 2):
            tile = jnp.where(left, o[(2 * t) * _WB:(2 * t + 1) * _WB], o[(2 * t + 1) * _WB:(2 * t + 2) * _WB])
            c0 = (g * rep // 2 + t) * LANES
            o_ref[0, r0:r0 + _WB, c0:c0 + LANES] = tile.astype(BF16)

    streams = [(j, g) for j in range(qb) for g in range(W_KV_HEADS)]
    ahead = 3
    pending = []
    for i, (j, g) in enumerate(streams):
        pending.append((j, g) + logits_of(j, g))
        if i >= ahead:
            finish(*pending.pop(0))
    for item in pending:
        finish(*item)


def _window_attention(q, kx, vx, sink, B, tq, tcx, lat_kv=None, qb=4):
    q3 = q.reshape(B, tq, BRANCH)
    kx3, vx3 = kx.reshape(B, tcx, 2 * LANES), vx.reshape(B, tcx, 2 * LANES)
    nb = tq // _WB
    qb = math.gcd(qb, nb)
    has_latent = lat_kv is not None
    whole = lambda t: pl.BlockSpec((1, t, 2 * LANES), lambda b, n: (b, 0, 0))
    in_specs = [pl.BlockSpec(memory_space=pltpu.SMEM),
                pl.BlockSpec((1, qb * _WB, BRANCH), lambda b, n: (b, n, 0))]
    args = [sink.astype(F32), q3]
    if has_latent:
        assert tq >= 3 * _WB
        in_specs += [whole(tq), whole(tq)]
        args += [a.reshape(B, tq, 2 * LANES) for a in lat_kv]
    in_specs += [whole(tcx), whole(tcx)]
    args += [kx3, vx3]
    out = pl.pallas_call(
        functools.partial(_win_kernel, has_latent=has_latent, qb=qb, seq=tq),
        grid=(B, nb // qb),
        in_specs=in_specs,
        out_specs=pl.BlockSpec((1, qb * _WB, BRANCH), lambda b, n: (b, n, 0)),
        out_shape=jax.ShapeDtypeStruct((B, tq, BRANCH), BF16),
        compiler_params=_params(("arbitrary", "arbitrary")),
        name="win_lat" if has_latent else "win_ctx",
    )(*args)
    return out.reshape(B * tq, BRANCH)


def _diff_kernel(lam_ref, q_ref, *refs, tq, nsub, tk, tcx, tlat, lam_init):
    if tlat:
        kx_ref, vx_ref, kl_ref, vl_ref, nrm_ref, o_ref = refs
    else:
        kx_ref, vx_ref, nrm_ref, o_ref = refs
        kl_ref = vl_ref = None
    sub = tq // nsub
    lane = lax.broadcasted_iota(jnp.int32, (sub, LANES), 1)
    left = lane < LANES // 2
    zero = jnp.zeros((sub, LANES), BF16)
    tiles = [(kx_ref, vx_ref, 0, tcx)] + [(kl_ref, vl_ref, j * tk, tk) for j in range(tlat // tk)]
    ones = jnp.ones((max(tk, tcx), LANES), BF16)

    lhs = []
    for u in range(nsub):
        q = q_ref[0, u * sub:(u + 1) * sub, :]
        lhs += [jnp.where(left, q, zero), jnp.where(left, zero, q)]
    state = [None] * (2 * nsub)
    for k_ref, v_ref, r0, n in tiles:
        vx = jnp.concatenate([v_ref[0, r0:r0 + n, :], ones[0:n]], axis=1)
        k = k_ref[0, r0:r0 + n, :]
        for i in range(2 * nsub):
            s = _dot_nt(lhs[i], k)
            m_part = s[:, 0:LANES]
            for t in range(1, n // LANES):
                m_part = jnp.maximum(m_part, s[:, t * LANES:(t + 1) * LANES])
            m_j = jnp.max(m_part, axis=-1, keepdims=True)
            part = _dot(jnp.exp2(s - m_j).astype(BF16), vx)
            if state[i] is None:
                state[i] = (m_j, part)
            else:
                m_run, acc = state[i]
                m_new = jnp.maximum(m_run, m_j)
                state[i] = (m_new, acc * jnp.exp2(m_run - m_new) + part * jnp.exp2(m_j - m_new))
    outs = [acc[:, 0:LANES] / acc[:, LANES:2 * LANES] for _, acc in state]
    lv = lam_ref[...]
    s1 = jnp.sum(lv[0:1] * lv[1:2], axis=-1, keepdims=True)
    s2 = jnp.sum(lv[2:3] * lv[3:4], axis=-1, keepdims=True)
    lam = jnp.exp(s1) - jnp.exp(s2) + lam_init
    for u in range(nsub):
        y = outs[2 * u] - lam * outs[2 * u + 1]
        ms = jnp.mean(y * y, axis=-1, keepdims=True)
        o_ref[0, u * sub:(u + 1) * sub, :] = (y * lax.rsqrt(ms + EPS) * nrm_ref[...] * (1.0 - lam_init)).astype(BF16)


def _diff_attention(q, kx, vx, lamv, d_norm, lam_init, B, tq_total, tcx, lat_kv=None, sub=1024, nsub=2, tk=1024):
    q3 = q.reshape(B, tq_total, BRANCH)
    kx3, vx3 = kx.reshape(B, tcx, BRANCH), vx.reshape(B, tcx, BRANCH)
    sub = min(sub, tq_total)
    nsub = math.gcd(nsub, tq_total // sub)
    tq = sub * nsub
    head_blk = lambda t, f: pl.BlockSpec((1, t, LANES), f)
    whole = lambda b, h, i: (b, 0, h)
    in_specs = [pl.BlockSpec((8, LANES), lambda b, h, i: (0, 0)),
                head_blk(tq, lambda b, h, i: (b, i, h)),
                head_blk(tcx, whole), head_blk(tcx, whole)]
    args = [lamv, q3, kx3, vx3]
    tlat = 0
    if lat_kv is not None:
        tlat = tq_total
        k3, v3 = (a.reshape(B, tlat, BRANCH) for a in lat_kv)
        in_specs += [head_blk(tlat, whole), head_blk(tlat, whole)]
        args += [k3, v3]
    in_specs.append(pl.BlockSpec((1, LANES), lambda b, h, i: (0, h)))
    args.append(d_norm.reshape(1, -1))
    tk = min(tk, max(tlat, LANES))
    out = pl.pallas_call(
        functools.partial(_diff_kernel, tq=tq, nsub=nsub, tk=tk, tcx=tcx, tlat=tlat, lam_init=lam_init),
        grid=(B, DF_HEADS, tq_total // tq),
        in_specs=in_specs,
        out_specs=head_blk(tq, lambda b, h, i: (b, i, h)),
        out_shape=jax.ShapeDtypeStruct((B, tq_total, BRANCH), BF16),
        compiler_params=_params(("arbitrary", "arbitrary", "arbitrary")),
        name="diff_lat" if tlat else "diff_ctx",
    )(*args)
    return out.reshape(B * tq_total, BRANCH)


def _merge_kernel(ya_ref, yb_ref, yc_ref, gate_ref, x_ref, mod_ref, wb_ref, wo_ref, o_ref, *, d):
    merged = None
    for i, y_ref in enumerate((ya_ref, yb_ref, yc_ref)):
        gate = jax.nn.sigmoid(gate_ref[:, i * d:(i + 1) * d].astype(F32))
        term = gate * _dot(y_ref[...], wb_ref[i])
        merged = term if merged is None else merged + term
    out = _dot(merged.astype(BF16), wo_ref[...])
    gt = mod_ref[0][:, 2 * d:3 * d]
    o_ref[...] = x_ref[...] + gt * out


def _merge(ya, yb, yc, gate, x2, mod, wb, wo, *, tm, rows_per_mod):
    N, D = x2.shape
    row = lambda w: pl.BlockSpec((tm, w), lambda i: (i, 0))
    return pl.pallas_call(
        functools.partial(_merge_kernel, d=D),
        grid=(N // tm,),
        in_specs=[row(BRANCH), row(BRANCH), row(BRANCH), row(N_BRANCH * D), row(D),
                  pl.BlockSpec((1, 1, 6 * D), lambda i: ((i * tm) // rows_per_mod, 0, 0)),
                  _resident((N_BRANCH, BRANCH, D), lambda i: (0, 0, 0)),
                  _resident((D, D), lambda i: (0, 0))],
        out_specs=row(D),
        out_shape=jax.ShapeDtypeStruct((N, D), F32),
        compiler_params=_params(("arbitrary",)),
        name="merge",
    )(ya, yb, yc, gate, x2, mod, wb, wo)


_FFN_CHUNK = 256


def _ffn_kernel(x_ref, mod_ref, g_ref, wi_ref, wo_ref, gf_ref, o_ref, *, d, hidden, final_norm):
    x = x_ref[...]
    mod = mod_ref[0]
    h = _modnorm(x, g_ref[...], mod[:, 3 * d:4 * d], mod[:, 4 * d:5 * d]).astype(BF16)
    acc = None
    for c0 in range(0, hidden, _FFN_CHUNK):
        gate = _dot(h, wi_ref[:, c0:c0 + _FFN_CHUNK])
        up = _dot(h, wi_ref[:, hidden + c0:hidden + c0 + _FFN_CHUNK])
        a = (gate * jax.nn.sigmoid(gate) * up).astype(BF16)
        part = _dot(a, wo_ref[c0:c0 + _FFN_CHUNK, :])
        acc = part if acc is None else acc + part
    y = x + mod[:, 5 * d:6 * d] * acc
    if final_norm:
        ms = jnp.mean(y * y, axis=-1, keepdims=True)
        y = y * lax.rsqrt(ms + EPS) * gf_ref[...]
    o_ref[...] = y


def _ffn(x2, mod, g, wi, wo, g_final, *, tm, rows_per_mod, final_norm):
    N, D = x2.shape
    hidden = wo.shape[0]
    return pl.pallas_call(
        functools.partial(_ffn_kernel, d=D, hidden=hidden, final_norm=final_norm),
        grid=(N // tm,),
        in_specs=[pl.BlockSpec((tm, D), lambda i: (i, 0)),
                  pl.BlockSpec((1, 1, 6 * D), lambda i: ((i * tm) // rows_per_mod, 0, 0)),
                  pl.BlockSpec((1, D), lambda i: (0, 0)),
                  _resident((D, 2 * hidden), lambda i: (0, 0)),
                  _resident((hidden, D), lambda i: (0, 0)),
                  pl.BlockSpec((1, D), lambda i: (0, 0))],
        out_specs=pl.BlockSpec((tm, D), lambda i: (i, 0)),
        out_shape=jax.ShapeDtypeStruct((N, D), F32),
        compiler_params=_params(("arbitrary",)),
        name="ffn",
    )(x2, mod, g, wi, wo, g_final)


def _row_tile(n, want=512):
    t = want
    while n % t:
        t //= 2
    return t


def kernel(x, c, ctx, c_ctx, w_mod, b_mod, g_mix, g_ffn, w_in, b_gate, conv_w, m_norm, sink,
           lam_q1, lam_k1, lam_q2, lam_k2, d_norm, w_branch, w_out, w_ffn_in, w_ffn_out, g_final):
    B, T, D = x.shape
    Tc = ctx.shape[1]
    depth = w_mod.shape[0]
    assert T % CHUNK == 0 and Tc % CHUNK == 0 and T % GRID_W == 0

    n_rows = -(-(B + 1) // 16) * 16
    cc = jnp.concatenate([c, c_ctx[None, :], jnp.zeros((n_rows - B - 1, D), F32)], axis=0)
    mods = _adaln(cc, w_mod, b_mod)

    rope = _rope_tables(T)
    no_rope = tuple(jnp.zeros((Tc, LANES), F32) for _ in range(3))
    tm = _row_tile(T)
    tmc = _row_tile(Tc)
    names = [s[0] for s in _SEGS]

    xl = x.reshape(B * T, D)
    xc = ctx.reshape(B * Tc, D)
    for l in range(depth):
        need_ctx = l < depth - 1
        lam_init = 0.8 - 0.6 * math.exp(-0.3 * l)
        mod_l = mods[l, :B].reshape(B, 1, 6 * D)
        mod_c = mods[l, B:B + 1].reshape(1, 1, 6 * D)
        w_ext = _extend_w_in(w_in[l])
        g_mix_l = g_mix[l].reshape(1, D)
        g_ffn_l = g_ffn[l].reshape(1, D)
        lamv = jnp.zeros((8, LANES), F32)
        for i, v in enumerate((lam_q1[l], lam_k1[l], lam_q2[l], lam_k2[l])):
            lamv = lamv.at[i, :v.shape[0]].set(v.astype(F32))

        pl_ = dict(zip(names, _inproj(xl, mod_l, g_mix_l, rope, w_ext, tm=tm, rows_per_mod=T, seq=T, use_rope=True)))
        pc_ = dict(zip(names, _inproj(xc, mod_c, g_mix_l, no_rope, w_ext, tm=tmc, rows_per_mod=B * Tc, seq=Tc,
                                      use_rope=False)))

        gates_l = _gates(pl_["mg"], b_gate[l], B, T)
        gates_c = _gates(pc_["mg"], b_gate[l], B, Tc)
        ya, ya_c = _mlstm(pc_, pl_, gates_c, gates_l, conv_w[l], m_norm[l], B, Tc, T, need_ctx)
        yb = _window_attention(pl_["wq"], pc_["wk"], pc_["wv"], sink[l], B, T, Tc, lat_kv=(pl_["wk"], pl_["wv"]))
        yc = _diff_attention(pl_["dq"], pc_["dk"], pc_["dv"], lamv, d_norm[l], lam_init, B, T, Tc,
                             lat_kv=(pl_["dk"], pl_["dv"]))
        wb = w_branch[l].astype(BF16)
        wo = w_out[l].astype(BF16)
        wfi = w_ffn_in[l].astype(BF16)
        wfo = w_ffn_out[l].astype(BF16)
        gfin = g_final.reshape(1, D)
        last = l == depth - 1
        xl = _merge(ya, yb, yc, pl_["gate"], xl, mod_l, wb, wo, tm=tm, rows_per_mod=T)
        xl = _ffn(xl, mod_l, g_ffn_l, wfi, wfo, gfin, tm=tm, rows_per_mod=T, final_norm=last)
        if need_ctx:
            yb_c = _window_attention(pc_["wq"], pc_["wk"], pc_["wv"], sink[l], B, Tc, Tc)
            yc_c = _diff_attention(pc_["dq"], pc_["dk"], pc_["dv"], lamv, d_norm[l], lam_init, B, Tc, Tc)
            xc = _merge(ya_c, yb_c, yc_c, pc_["gate"], xc, mod_c, wb, wo, tm=tmc, rows_per_mod=B * Tc)
            xc = _ffn(xc, mod_c, g_ffn_l, wfi, wfo, gfin, tm=tmc, rows_per_mod=B * Tc, final_norm=False)
    return xl.reshape(B, T, D)
```

```python
import functools
import math

import jax
import jax.numpy as jnp
import numpy as np
from jax import lax
from jax.experimental import pallas as pl
from jax.experimental.pallas import tpu as pltpu

F32 = jnp.float32
BF16 = jnp.bfloat16

GRID_W = 64
BRANCH = 512
N_BRANCH = 3
M_HEADS = 4
M_DIM = 128
W_HEADS = 8
W_KV_HEADS = 2
W_DIM = 64
DF_HEADS = 4
ROPE_DIM = 64
ROPE_BASE = 10000.0
EPS = 1e-6
LANES = 128
CHUNK = 128
VMEM_LIMIT = 56 * 1024 * 1024

_NT = (((1,), (1,)), ((), ()))
LOG2E = math.log2(math.e)
_Q_SCALE = W_DIM ** -0.5 * LOG2E


def _dot(a, b):
    return jnp.dot(a, b, preferred_element_type=F32)


def _dot_nt(a, b):
    return lax.dot_general(a, b, _NT, preferred_element_type=F32)


def _params(sem):
    return pltpu.CompilerParams(dimension_semantics=sem, vmem_limit_bytes=VMEM_LIMIT)


def _resident(shape, index_map):
    return pl.BlockSpec(shape, index_map, pipeline_mode=pl.Buffered(1))


def _adaln_kernel(c_ref, w_ref, b_ref, o_ref):
    c = c_ref[...]
    s = (c * jax.nn.sigmoid(c)).astype(BF16)
    o_ref[0] = _dot(s, w_ref[0].astype(BF16)) + b_ref[0]


def _adaln(cc, w_mod, b_mod):
    L, D, N = w_mod.shape
    R = cc.shape[0]
    tn = 1536
    return pl.pallas_call(
        _adaln_kernel,
        grid=(L, N // tn),
        in_specs=[pl.BlockSpec((R, D), lambda l, j: (0, 0)),
                  pl.BlockSpec((1, D, tn), lambda l, j: (l, 0, j)),
                  pl.BlockSpec((1, 1, tn), lambda l, j: (l, 0, j))],
        out_specs=pl.BlockSpec((1, R, tn), lambda l, j: (l, 0, j)),
        out_shape=jax.ShapeDtypeStruct((L, R, N), F32),
        compiler_params=_params(("arbitrary", "arbitrary")),
        name="adaln",
    )(cc, w_mod, b_mod.reshape(L, 1, N))


def _modnorm(x, g, shift, scale):
    ms = jnp.mean(x * x, axis=-1, keepdims=True)
    return x * lax.rsqrt(ms + EPS) * (g * (1.0 + scale)) + shift


_SEGS = (("mqk", 1024, BF16, "plain"), ("mv", 512, BF16, "plain"), ("mo", 512, BF16, "plain"),
         ("mg", 128, F32, "plain"), ("wq", 512, BF16, "ropeq"), ("wk", 256, BF16, "rope"),
         ("wv", 256, BF16, "plain"), ("dq", 512, BF16, "ropeq"), ("dk", 512, BF16, "rope"),
         ("dv", 512, BF16, "plain"), ("gate", 3072, BF16, "plain"))
_EXT_WIDTH = sum(s[1] for s in _SEGS)
_COL_CHUNK = 512


def _inproj_kernel(x_ref, mod_ref, g_ref, cos_ref, sa_ref, sb_ref, w_ref, *out_refs, d, use_rope):
    x = x_ref[...]
    mod = mod_ref[0]
    h = _modnorm(x, g_ref[...], mod[:, 0:d], mod[:, d:2 * d]).astype(BF16)
    if use_rope:
        cos, sa, sb = cos_ref[...], sa_ref[...], sb_ref[...]
    off = 0
    for (name, width, dt, kind), o_ref in zip(_SEGS, out_refs):
        for c0 in range(0, width, _COL_CHUNK):
            cw = min(_COL_CHUNK, width - c0)
            acc = _dot(h, w_ref[:, off + c0:off + c0 + cw])
            if kind != "plain":
                tiles = []
                for t in range(cw // LANES):
                    a = acc[:, t * LANES:(t + 1) * LANES]
                    if use_rope:
                        a = (a * cos + pltpu.roll(a, LANES - 16, 1) * sa + pltpu.roll(a, 16, 1) * sb)
                    if kind == "ropeq":
                        a = a * _Q_SCALE
                    tiles.append(a)
                acc = jnp.concatenate(tiles, axis=1) if len(tiles) > 1 else tiles[0]
            o_ref[:, c0:c0 + cw] = acc.astype(dt)
        off += width


def _inproj(x2, mod, g, rope, w_ext, *, tm, rows_per_mod, seq, use_rope):
    N, D = x2.shape
    nt = seq // tm
    kern = functools.partial(_inproj_kernel, d=D, use_rope=use_rope)
    rope_spec = pl.BlockSpec((tm, LANES), lambda i: (i % nt, 0))
    out_shape = [jax.ShapeDtypeStruct((N, w), dt) for _, w, dt, _ in _SEGS]
    out_specs = [pl.BlockSpec((tm, w), lambda i: (i, 0)) for _, w, _, _ in _SEGS]
    return pl.pallas_call(
        kern,
        grid=(N // tm,),
        in_specs=[pl.BlockSpec((tm, D), lambda i: (i, 0)),
                  pl.BlockSpec((1, 1, 6 * D), lambda i: ((i * tm) // rows_per_mod, 0, 0)),
                  pl.BlockSpec((1, D), lambda i: (0, 0)),
                  rope_spec, rope_spec, rope_spec,
                  _resident((D, _EXT_WIDTH), lambda i: (0, 0))],
        out_specs=out_specs,
        out_shape=out_shape,
        compiler_params=_params(("arbitrary",)),
        name="inproj_lat" if use_rope else "inproj_ctx",
    )(x2, mod, g, *rope, w_ext)


def _extend_w_in(w):
    D = w.shape[0]
    o = np.cumsum([0, 512, 512, 512, 512, 16, 512, 128, 128, 512, 512, 512, 3072])
    p = [w[:, o[i]:o[i + 1]] for i in range(12)]
    dup = lambda a: jnp.concatenate([a[:, 0:64], a[:, 0:64], a[:, 64:128], a[:, 64:128]], axis=1)
    mg = jnp.concatenate([p[4], jnp.zeros((D, LANES - 16), w.dtype)], axis=1)
    ext = jnp.concatenate([p[0], p[1], p[2], p[3], mg, p[5], dup(p[6]), dup(p[7]), p[8], p[9], p[10], p[11]], axis=1)
    return ext.astype(BF16)


def _rope_tables(n_tokens):
    rows = n_tokens // GRID_W
    r, col = jnp.meshgrid(jnp.arange(rows), jnp.arange(GRID_W), indexing="ij")
    half = ROPE_DIM // 2
    inv = ROPE_BASE ** (-jnp.arange(0, half, 2, dtype=F32) / half)
    ang_r = r.reshape(-1, 1).astype(F32) * inv
    ang_c = col.reshape(-1, 1).astype(F32) * inv
    ang = jnp.concatenate([ang_r, ang_r, ang_c, ang_c], axis=-1)
    cos, sin = jnp.cos(ang), jnp.sin(ang)
    cos2 = jnp.concatenate([cos, cos], axis=-1)
    sin2 = jnp.concatenate([sin, sin], axis=-1)
    first = (jnp.arange(LANES) % 32) < 16
    sin_a = jnp.where(first, -sin2, 0.0)
    sin_b = jnp.where(first, 0.0, sin2)
    return cos2, sin_a, sin_b


def _scan_lanes(x, op, fill, reverse):
    lane = lax.broadcasted_iota(jnp.int32, x.shape, 1)
    d = 1
    while d < LANES:
        if reverse:
            shifted = jnp.where(lane < LANES - d, pltpu.roll(x, LANES - d, 1), fill)
        else:
            shifted = jnp.where(lane >= d, pltpu.roll(x, d, 1), fill)
        x = op(x, shifted)
        d *= 2
    return x


def _gates_kernel(x_ref, bias_ref, o_ref, *, rows):
    x = x_ref[0] + bias_ref[...]
    lane = lax.broadcasted_iota(jnp.int32, (rows, LANES), 1)
    for d, reverse in enumerate((False, True)):
        i_pre = x[(2 * d) * rows:(2 * d + 1) * rows]
        f_pre = x[(2 * d + 1) * rows:(2 * d + 2) * rows]
        log_f = jax.nn.log_sigmoid(f_pre)
        b = _scan_lanes(log_f, jnp.add, 0.0, reverse)
        a = i_pre - b
        pm = _scan_lanes(a, jnp.maximum, -jnp.inf, reverse)
        last = 0 if reverse else LANES - 1
        b_end = jnp.sum(jnp.where(lane == last, b, 0.0), axis=1, keepdims=True)
        a_max = jnp.max(a, axis=1, keepdims=True)
        o_ref[0, 5 * d + 0] = b
        o_ref[0, 5 * d + 1] = a
        o_ref[0, 5 * d + 2] = pm
        o_ref[0, 5 * d + 3] = jnp.broadcast_to(b_end, (rows, LANES))
        o_ref[0, 5 * d + 4] = jnp.broadcast_to(a_max, (rows, LANES))


def _gates(mg, b_gate, B, T):
    nc = T // CHUNK
    rows = M_HEADS * nc
    g = mg[:, :4 * M_HEADS].reshape(B, T, 4 * M_HEADS)
    gt = jnp.transpose(g, (0, 2, 1)).reshape(B, 4 * rows, LANES)
    bias = jnp.repeat(b_gate.astype(F32), nc).reshape(4 * rows, 1)
    return pl.pallas_call(
        functools.partial(_gates_kernel, rows=rows),
        grid=(B,),
        in_specs=[pl.BlockSpec((1, 4 * rows, LANES), lambda b: (b, 0, 0)),
                  pl.BlockSpec((4 * rows, 1), lambda b: (0, 0))],
        out_specs=pl.BlockSpec((1, 10, rows, LANES), lambda b: (b, 0, 0, 0)),
        out_shape=jax.ShapeDtypeStruct((B, 10, rows, LANES), F32),
        compiler_params=_params(("arbitrary",)),
        name="mlstm_gates",
    )(gt, bias)


_CONV_TILE = 256
_HB = 2


def _mlstm_kernel(qc_ref, kc_ref, vc_ref, oc_ref, pc_ref, ql_ref, kl_ref, vl_ref, ol_ref, pl_ref,
                  cwq_ref, cwk_ref, nrm_ref, *rest, tc, tl, need_ctx):
    if need_ctx:
        yl_ref, yc_ref = rest[0], rest[1]
        scr = rest[2:]
    else:
        yl_ref, yc_ref = rest[0], None
        scr = rest[1:]
    xpad, qs, ks, hf, hb, cx, mst = scr
    head0 = pl.program_id(1) * _HB

    def conv_stream(u_ref, w_ref, dst, dst_off, ts, scale):
        for hh in range(_HB):
            xpad[hh, 0:8, :] = jnp.zeros((8, LANES), F32)
            xpad[hh, 8:8 + ts, :] = u_ref[0, :, hh * LANES:(hh + 1) * LANES].astype(F32)
            xpad[hh, 8 + ts:16 + ts, :] = jnp.zeros((8, LANES), F32)
        w = w_ref[...]
        w0, w1, w2 = w[0:1], w[1:2], w[2:3]
        tile = min(_CONV_TILE, ts)

        def body(i, _):
            r0 = pl.multiple_of(i * tile, 8)
            for hh in range(_HB):
                lanes = slice(hh * LANES, (hh + 1) * LANES)
                prev = xpad[hh, pl.ds(r0 + 7, tile), :]
                cur = xpad[hh, pl.ds(r0 + 8, tile), :]
                nxt = xpad[hh, pl.ds(r0 + 9, tile), :]
                y = prev * w0[:, lanes] + cur * w1[:, lanes] + nxt * w2[:, lanes]
                y = y * jax.nn.sigmoid(y) * scale
                dst[pl.ds(pl.multiple_of(dst_off + r0, 16), tile), lanes] = y.astype(BF16)
            return 0

        lax.fori_loop(0, ts // tile, body, 0)

    kscale = M_DIM ** -0.5
    conv_stream(qc_ref, cwq_ref, qs, 0, tc, 1.0)
    conv_stream(kc_ref, cwk_ref, ks, 0, tc, kscale)
    conv_stream(ql_ref, cwq_ref, qs, tc, tl, 1.0)
    conv_stream(kl_ref, cwk_ref, ks, tc, tl, kscale)

    cx[...] = jnp.zeros(cx.shape, F32)
    mst[...] = jnp.zeros(mst.shape, F32)

    row_i = lax.broadcasted_iota(jnp.int32, (CHUNK, CHUNK), 0)
    col_i = lax.broadcasted_iota(jnp.int32, (CHUNK, CHUNK), 1)
    masks = (col_i <= row_i, col_i >= row_i)
    ones_b = jnp.ones((CHUNK, LANES), BF16)
    ones_f = jnp.ones((LANES, CHUNK), F32)

    def colify(row):
        return jnp.broadcast_to(row, (CHUNK, CHUNK)).T

    def chunk_step(hh, d, c, p_ref, v_ref, nc, base, want_h):
        r = (head0 + hh) * nc + c
        lanes = slice(hh * LANES, (hh + 1) * LANES)
        st = 2 * hh + d
        plane = lambda j: p_ref[0, 5 * d + j, pl.ds(r, 1), :]
        b_row, a_row, pm_row, bend_row, amax_row = (plane(j) for j in range(5))
        m_prev = mst[st:st + 1, :]
        m_end = jnp.maximum(m_prev, amax_row)
        w_end = jnp.exp(a_row - m_end)
        decay = jnp.exp(m_prev - m_end)
        row0 = pl.multiple_of(base + c * CHUNK, CHUNK)
        q = qs[pl.ds(row0, CHUNK), lanes]
        k = ks[pl.ds(row0, CHUNK), lanes]
        v = v_ref[0, pl.ds(pl.multiple_of(c * CHUNK, CHUNK), CHUNK), lanes]
        c_old = cx[st]
        if want_h:
            m_col = jnp.maximum(m_prev, colify(pm_row))
            w_intra = jnp.exp(jnp.where(masks[d], a_row - m_col, -jnp.inf))
            w_inter = jnp.exp(m_prev - m_col)
            floor = jnp.exp(-(colify(b_row) + m_col))
            s = _dot_nt(q, k) * w_intra
            intra = _dot(s.astype(BF16), jnp.concatenate([v, ones_b], axis=1))
            inter = _dot_nt(q, c_old.astype(BF16))
            num = intra[:, 0:M_DIM] + w_inter * inter[:, 0:M_DIM]
            den = intra[:, M_DIM:2 * M_DIM] + w_inter * inter[:, M_DIM:2 * M_DIM]
            hval = num / jnp.maximum(jnp.abs(den), floor)
            dst = hf if d == 0 else hb
            dst[pl.ds(row0, CHUNK), lanes] = hval
        vxt = jnp.concatenate([v.astype(F32).T, ones_f], axis=0)
        cx[st] = decay * c_old + _dot((vxt * w_end).astype(BF16), k)
        mst[st:st + 1, :] = bend_row + m_end

    ncc, ncl = tc // CHUNK, tl // CHUNK
    for c in range(ncc):
        for hh in range(_HB):
            chunk_step(hh, 0, c, pc_ref, vc_ref, ncc, 0, need_ctx)
            chunk_step(hh, 1, ncc - 1 - c, pc_ref, vc_ref, ncc, 0, need_ctx)

    def lat_body(c, _):
        for hh in range(_HB):
            chunk_step(hh, 0, c, pl_ref, vl_ref, ncl, tc, True)
            chunk_step(hh, 1, ncl - 1 - c, pl_ref, vl_ref, ncl, tc, True)
        return 0

    lax.fori_loop(0, ncl, lat_body, 0, unroll=4)

    nrm = nrm_ref[...]

    def finish(o_ref, y_ref, base, ts):
        def body(i, _):
            r0 = pl.multiple_of(i * CHUNK, CHUNK)
            hsum = hf[pl.ds(base + r0, CHUNK), :] + hb[pl.ds(base + r0, CHUNK), :]
            y = hsum * jax.nn.sigmoid(o_ref[0, pl.ds(r0, CHUNK), :].astype(F32))
            for hh in range(_HB):
                lanes = slice(hh * LANES, (hh + 1) * LANES)
                yh = y[:, lanes]
                ms = jnp.mean(yh * yh, axis=-1, keepdims=True)
                y_ref[0, pl.ds(r0, CHUNK), lanes] = (yh * lax.rsqrt(ms + EPS) * nrm[:, lanes]).astype(BF16)
            return 0

        lax.fori_loop(0, ts // CHUNK, body, 0, unroll=min(4, ts // CHUNK))

    finish(ol_ref, yl_ref, tc, tl)
    if need_ctx:
        finish(oc_ref, yc_ref, 0, tc)


def _mlstm(ctx_p, lat_p, gates_c, gates_l, conv_w, m_norm, B, tc, tl, need_ctx):
    r3 = lambda a, t: a.reshape(B, t, a.shape[-1])
    qk_c, v_c, o_c = r3(ctx_p["mqk"], tc), r3(ctx_p["mv"], tc), r3(ctx_p["mo"], tc)
    qk_l, v_l, o_l = r3(lat_p["mqk"], tl), r3(lat_p["mv"], tl), r3(lat_p["mo"], tl)
    G = M_HEADS // _HB
    width = _HB * LANES
    head_blk = lambda t, off: pl.BlockSpec((1, t, width), lambda b, g: (b, 0, g + off))
    gate_blk = lambda g: pl.BlockSpec((1,) + g.shape[1:], lambda b, h: (b, 0, 0, 0))
    in_specs = [head_blk(tc, 0), head_blk(tc, G), head_blk(tc, 0), head_blk(tc, 0), gate_blk(gates_c),
                head_blk(tl, 0), head_blk(tl, G), head_blk(tl, 0), head_blk(tl, 0), gate_blk(gates_l),
                pl.BlockSpec((3, width), lambda b, g: (0, g)),
                pl.BlockSpec((3, width), lambda b, g: (0, g + G)),
                pl.BlockSpec((1, width), lambda b, g: (0, g))]
    out_shape = [jax.ShapeDtypeStruct((B, tl, BRANCH), BF16)]
    out_specs = [head_blk(tl, 0)]
    if need_ctx:
        out_shape.append(jax.ShapeDtypeStruct((B, tc, BRANCH), BF16))
        out_specs.append(head_blk(tc, 0))
    tot = tc + tl
    scratch = [pltpu.VMEM((_HB, max(tc, tl) + 16, LANES), F32),
               pltpu.VMEM((tot, width), BF16), pltpu.VMEM((tot, width), BF16),
               pltpu.VMEM((tot, width), F32), pltpu.VMEM((tot, width), F32),
               pltpu.VMEM((2 * _HB, 2 * LANES, LANES), F32), pltpu.VMEM((8, LANES), F32)]
    outs = pl.pallas_call(
        functools.partial(_mlstm_kernel, tc=tc, tl=tl, need_ctx=need_ctx),
        grid=(B, G),
        in_specs=in_specs,
        out_specs=out_specs,
        out_shape=out_shape,
        scratch_shapes=scratch,
        compiler_params=_params(("arbitrary", "arbitrary")),
        name="mlstm",
    )(qk_c, qk_c, v_c, o_c, gates_c, qk_l, qk_l, v_l, o_l, gates_l,
      conv_w, conv_w, m_norm.reshape(1, -1))
    ya = outs[0].reshape(B * tl, BRANCH)
    ya_c = outs[1].reshape(B * tc, BRANCH) if need_ctx else None
    return ya, ya_c


_WB = 128


def _win_kernel(sink_ref, q_ref, *refs, has_latent, qb, seq):
    if has_latent:
        kl_ref, vl_ref, kx_ref, vx_ref, o_ref = refs
    else:
        kx_ref, vx_ref, o_ref = refs
    n = pl.program_id(1)
    rep = W_HEADS // W_KV_HEADS
    rows = rep * _WB
    span = 3 * _WB
    lane = lax.broadcasted_iota(jnp.int32, (_WB, LANES), 1)
    left = lane < W_DIM
    zero = jnp.zeros((_WB, LANES), BF16)
    if has_latent:
        qq = jnp.bitwise_and(lax.broadcasted_iota(jnp.int32, (rows, span), 0), _WB - 1)
        dist = lax.broadcasted_iota(jnp.int32, (rows, span), 1) - qq
    windows = {}

    def window_of(j):
        if j not in windows:
            q_start = (n * qb + j) * _WB
            k_start = pl.multiple_of(jnp.clip(q_start - _WB, 0, seq - span), _WB)
            off = pltpu.bitcast(dist + (k_start - q_start + _WB), jnp.uint32)
            windows[j] = (k_start, off <= jnp.uint32(2 * _WB))
        return windows[j]

    def logits_of(j, g):
        r0 = j * _WB
        pieces = []
        for t in range(rep // 2):
            qt = q_ref[0, r0:r0 + _WB, (g * rep // 2 + t) * LANES:(g * rep // 2 + t + 1) * LANES]
            pieces += [jnp.where(left, qt, zero), jnp.where(left, zero, qt)]
        lhs = jnp.concatenate(pieces, axis=0)
        gsl = slice(g * LANES, (g + 1) * LANES)
        logits = [_dot_nt(lhs, kx_ref[0, :, gsl])]
        vals = [vx_ref[0, :, gsl]]
        if has_latent:
            k_start, near = window_of(j)
            s_win = _dot_nt(lhs, kl_ref[0, pl.ds(k_start, span), gsl])
            logits.append(jnp.where(near, s_win, -jnp.inf))
            vals.append(vl_ref[0, pl.ds(k_start, span), gsl])
        return logits, vals

    def finish(j, g, logits, vals):
        r0 = j * _WB
        sink_rep = jnp.concatenate(
            [jnp.full((_WB, LANES), sink_ref[g * rep + r] * LOG2E, F32) for r in range(rep)], axis=0)
        m_part = None
        for s in logits:
            for t in range(s.shape[1] // LANES):
                st = s[:, t * LANES:(t + 1) * LANES]
                m_part = st if m_part is None else jnp.maximum(m_part, st)
        m = jnp.maximum(sink_rep, jnp.max(m_part, axis=-1, keepdims=True))
        acc = None
        for s, v in zip(logits, vals):
            p = jnp.concatenate([jnp.exp2(s[:, t * LANES:(t + 1) * LANES] - m)
                                 for t in range(s.shape[1] // LANES)], axis=1)
            part = _dot(p.astype(BF16), jnp.concatenate([v, jnp.ones(v.shape, BF16)], axis=1))
            acc = part if acc is None else acc + part
        o = acc[:, 0:LANES] / (acc[:, LANES:2 * LANES] + jnp.exp2(sink_rep - m))
        for t in range(rep // 2):
            tile = jnp.where(left, o[(2 * t) * _WB:(2 * t + 1) * _WB], o[(2 * t + 1) * _WB:(2 * t + 2) * _WB])
            c0 = (g * rep // 2 + t) * LANES
            o_ref[0, r0:r0 + _WB, c0:c0 + LANES] = tile.astype(BF16)

    streams = [(j, g) for j in range(qb) for g in range(W_KV_HEADS)]
    ahead = 3
    pending = []
    for i, (j, g) in enumerate(streams):
        pending.append((j, g) + logits_of(j, g))
        if i >= ahead:
            finish(*pending.pop(0))
    for item in pending:
        finish(*item)


def _window_attention(q, kx, vx, sink, B, tq, tcx, lat_kv=None, qb=4):
    q3 = q.reshape(B, tq, BRANCH)
    kx3, vx3 = kx.reshape(B, tcx, 2 * LANES), vx.reshape(B, tcx, 2 * LANES)
    nb = tq // _WB
    qb = math.gcd(qb, nb)
    has_latent = lat_kv is not None
    whole = lambda t: pl.BlockSpec((1, t, 2 * LANES), lambda b, n: (b, 0, 0))
    in_specs = [pl.BlockSpec(memory_space=pltpu.SMEM),
                pl.BlockSpec((1, qb * _WB, BRANCH), lambda b, n: (b, n, 0))]
    args = [sink.astype(F32), q3]
    if has_latent:
        assert tq >= 3 * _WB
        in_specs += [whole(tq), whole(tq)]
        args += [a.reshape(B, tq, 2 * LANES) for a in lat_kv]
    in_specs += [whole(tcx), whole(tcx)]
    args += [kx3, vx3]
    out = pl.pallas_call(
        functools.partial(_win_kernel, has_latent=has_latent, qb=qb, seq=tq),
        grid=(B, nb // qb),
        in_specs=in_specs,
        out_specs=pl.BlockSpec((1, qb * _WB, BRANCH), lambda b, n: (b, n, 0)),
        out_shape=jax.ShapeDtypeStruct((B, tq, BRANCH), BF16),
        compiler_params=_params(("arbitrary", "arbitrary")),
        name="win_lat" if has_latent else "win_ctx",
    )(*args)
    return out.reshape(B * tq, BRANCH)


def _diff_kernel(lam_ref, q_ref, *refs, tq, nsub, tk, tcx, tlat, lam_init):
    if tlat:
        kx_ref, vx_ref, kl_ref, vl_ref, nrm_ref, o_ref = refs
    else:
        kx_ref, vx_ref, nrm_ref, o_ref = refs
        kl_ref = vl_ref = None
    sub = tq // nsub
    lane = lax.broadcasted_iota(jnp.int32, (sub, LANES), 1)
    left = lane < LANES // 2
    zero = jnp.zeros((sub, LANES), BF16)
    tiles = [(kx_ref, vx_ref, 0, tcx)] + [(kl_ref, vl_ref, j * tk, tk) for j in range(tlat // tk)]
    ones = jnp.ones((max(tk, tcx), LANES), BF16)

    lhs = []
    for u in range(nsub):
        q = q_ref[0, u * sub:(u + 1) * sub, :]
        lhs += [jnp.where(left, q, zero), jnp.where(left, zero, q)]
    state = [None] * (2 * nsub)
    for k_ref, v_ref, r0, n in tiles:
        vx = jnp.concatenate([v_ref[0, r0:r0 + n, :], ones[0:n]], axis=1)
        k = k_ref[0, r0:r0 + n, :]
        for i in range(2 * nsub):
            s = _dot_nt(lhs[i], k)
            m_part = s[:, 0:LANES]
            for t in range(1, n // LANES):
                m_part = jnp.maximum(m_part, s[:, t * LANES:(t + 1) * LANES])
            m_j = jnp.max(m_part, axis=-1, keepdims=True)
            part = _dot(jnp.exp2(s - m_j).astype(BF16), vx)
            if state[i] is None:
                state[i] = (m_j, part)
            else:
                m_run, acc = state[i]
                m_new = jnp.maximum(m_run, m_j)
                state[i] = (m_new, acc * jnp.exp2(m_run - m_new) + part * jnp.exp2(m_j - m_new))
    outs = [acc[:, 0:LANES] / acc[:, LANES:2 * LANES] for _, acc in state]
    lv = lam_ref[...]
    s1 = jnp.sum(lv[0:1] * lv[1:2], axis=-1, keepdims=True)
    s2 = jnp.sum(lv[2:3] * lv[3:4], axis=-1, keepdims=True)
    lam = jnp.exp(s1) - jnp.exp(s2) + lam_init
    for u in range(nsub):
        y = outs[2 * u] - lam * outs[2 * u + 1]
        ms = jnp.mean(y * y, axis=-1, keepdims=True)
        o_ref[0, u * sub:(u + 1) * sub, :] = (y * lax.rsqrt(ms + EPS) * nrm_ref[...] * (1.0 - lam_init)).astype(BF16)


def _diff_attention(q, kx, vx, lamv, d_norm, lam_init, B, tq_total, tcx, lat_kv=None, sub=1024, nsub=2, tk=1024):
    q3 = q.reshape(B, tq_total, BRANCH)
    kx3, vx3 = kx.reshape(B, tcx, BRANCH), vx.reshape(B, tcx, BRANCH)
    sub = min(sub, tq_total)
    nsub = math.gcd(nsub, tq_total // sub)
    tq = sub * nsub
    head_blk = lambda t, f: pl.BlockSpec((1, t, LANES), f)
    whole = lambda b, h, i: (b, 0, h)
    in_specs = [pl.BlockSpec((8, LANES), lambda b, h, i: (0, 0)),
                head_blk(tq, lambda b, h, i: (b, i, h)),
                head_blk(tcx, whole), head_blk(tcx, whole)]
    args = [lamv, q3, kx3, vx3]
    tlat = 0
    if lat_kv is not None:
        tlat = tq_total
        k3, v3 = (a.reshape(B, tlat, BRANCH) for a in lat_kv)
        in_specs += [head_blk(tlat, whole), head_blk(tlat, whole)]
        args += [k3, v3]
    in_specs.append(pl.BlockSpec((1, LANES), lambda b, h, i: (0, h)))
    args.append(d_norm.reshape(1, -1))
    tk = min(tk, max(tlat, LANES))
    out = pl.pallas_call(
        functools.partial(_diff_kernel, tq=tq, nsub=nsub, tk=tk, tcx=tcx, tlat=tlat, lam_init=lam_init),
        grid=(B, DF_HEADS, tq_total // tq),
        in_specs=in_specs,
        out_specs=head_blk(tq, lambda b, h, i: (b, i, h)),
        out_shape=jax.ShapeDtypeStruct((B, tq_total, BRANCH), BF16),
        compiler_params=_params(("arbitrary", "arbitrary", "arbitrary")),
        name="diff_lat" if tlat else "diff_ctx",
    )(*args)
    return out.reshape(B * tq_total, BRANCH)


def _merge_kernel(ya_ref, yb_ref, yc_ref, gate_ref, x_ref, mod_ref, wb_ref, wo_ref, o_ref, *, d):
    merged = None
    for i, y_ref in enumerate((ya_ref, yb_ref, yc_ref)):
        gate = jax.nn.sigmoid(gate_ref[:, i * d:(i + 1) * d].astype(F32))
        term = gate * _dot(y_ref[...], wb_ref[i])
        merged = term if merged is None else merged + term
    out = _dot(merged.astype(BF16), wo_ref[...])
    gt = mod_ref[0][:, 2 * d:3 * d]
    o_ref[...] = x_ref[...] + gt * out


def _merge(ya, yb, yc, gate, x2, mod, wb, wo, *, tm, rows_per_mod):
    N, D = x2.shape
    row = lambda w: pl.BlockSpec((tm, w), lambda i: (i, 0))
    return pl.pallas_call(
        functools.partial(_merge_kernel, d=D),
        grid=(N // tm,),
        in_specs=[row(BRANCH), row(BRANCH), row(BRANCH), row(N_BRANCH * D), row(D),
                  pl.BlockSpec((1, 1, 6 * D), lambda i: ((i * tm) // rows_per_mod, 0, 0)),
                  _resident((N_BRANCH, BRANCH, D), lambda i: (0, 0, 0)),
                  _resident((D, D), lambda i: (0, 0))],
        out_specs=row(D),
        out_shape=jax.ShapeDtypeStruct((N, D), F32),
        compiler_params=_params(("arbitrary",)),
        name="merge",
    )(ya, yb, yc, gate, x2, mod, wb, wo)


_FFN_CHUNK = 256


def _ffn_kernel(x_ref, mod_ref, g_ref, wi_ref, wo_ref, gf_ref, o_ref, *, d, hidden, final_norm):
    x = x_ref[...]
    mod = mod_ref[0]
    h = _modnorm(x, g_ref[...], mod[:, 3 * d:4 * d], mod[:, 4 * d:5 * d]).astype(BF16)
    acc = None
    for c0 in range(0, hidden, _FFN_CHUNK):
        gate = _dot(h, wi_ref[:, c0:c0 + _FFN_CHUNK])
        up = _dot(h, wi_ref[:, hidden + c0:hidden + c0 + _FFN_CHUNK])
        a = (gate * jax.nn.sigmoid(gate) * up).astype(BF16)
        part = _dot(a, wo_ref[c0:c0 + _FFN_CHUNK, :])
        acc = part if acc is None else acc + part
    y = x + mod[:, 5 * d:6 * d] * acc
    if final_norm:
        ms = jnp.mean(y * y, axis=-1, keepdims=True)
        y = y * lax.rsqrt(ms + EPS) * gf_ref[...]
    o_ref[...] = y


def _ffn(x2, mod, g, wi, wo, g_final, *, tm, rows_per_mod, final_norm):
    N, D = x2.shape
    hidden = wo.shape[0]
    return pl.pallas_call(
        functools.partial(_ffn_kernel, d=D, hidden=hidden, final_norm=final_norm),
        grid=(N // tm,),
        in_specs=[pl.BlockSpec((tm, D), lambda i: (i, 0)),
                  pl.BlockSpec((1, 1, 6 * D), lambda i: ((i * tm) // rows_per_mod, 0, 0)),
                  pl.BlockSpec((1, D), lambda i: (0, 0)),
                  _resident((D, 2 * hidden), lambda i: (0, 0)),
                  _resident((hidden, D), lambda i: (0, 0)),
                  pl.BlockSpec((1, D), lambda i: (0, 0))],
        out_specs=pl.BlockSpec((tm, D), lambda i: (i, 0)),
        out_shape=jax.ShapeDtypeStruct((N, D), F32),
        compiler_params=_params(("arbitrary",)),
        name="ffn",
    )(x2, mod, g, wi, wo, g_final)


def _row_tile(n, want=512):
    t = want
    while n % t:
        t //= 2
    return t


def kernel(x, c, ctx, c_ctx, w_mod, b_mod, g_mix, g_ffn, w_in, b_gate, conv_w, m_norm, sink,
           lam_q1, lam_k1, lam_q2, lam_k2, d_norm, w_branch, w_out, w_ffn_in, w_ffn_out, g_final):
    B, T, D = x.shape
    Tc = ctx.shape[1]
    depth = w_mod.shape[0]
    assert T % CHUNK == 0 and Tc % CHUNK == 0 and T % GRID_W == 0

    n_rows = -(-(B + 1) // 16) * 16
    cc = jnp.concatenate([c, c_ctx[None, :], jnp.zeros((n_rows - B - 1, D), F32)], axis=0)
    mods = _adaln(cc, w_mod, b_mod)

    rope = _rope_tables(T)
    no_rope = tuple(jnp.zeros((Tc, LANES), F32) for _ in range(3))
    tm = _row_tile(T)
    tmc = _row_tile(Tc)
    names = [s[0] for s in _SEGS]

    xl = x.reshape(B * T, D)
    xc = ctx.reshape(B * Tc, D)
    for l in range(depth):
        need_ctx = l < depth - 1
        lam_init = 0.8 - 0.6 * math.exp(-0.3 * l)
        mod_l = mods[l, :B].reshape(B, 1, 6 * D)
        mod_c = mods[l, B:B + 1].reshape(1, 1, 6 * D)
        w_ext = _extend_w_in(w_in[l])
        g_mix_l = g_mix[l].reshape(1, D)
        g_ffn_l = g_ffn[l].reshape(1, D)
        lamv = jnp.zeros((8, LANES), F32)
        for i, v in enumerate((lam_q1[l], lam_k1[l], lam_q2[l], lam_k2[l])):
            lamv = lamv.at[i, :v.shape[0]].set(v.astype(F32))

        pl_ = dict(zip(names, _inproj(xl, mod_l, g_mix_l, rope, w_ext, tm=tm, rows_per_mod=T, seq=T, use_rope=True)))
        pc_ = dict(zip(names, _inproj(xc, mod_c, g_mix_l, no_rope, w_ext, tm=tmc, rows_per_mod=B * Tc, seq=Tc,
                                      use_rope=False)))

        gates_l = _gates(pl_["mg"], b_gate[l], B, T)
        gates_c = _gates(pc_["mg"], b_gate[l], B, Tc)
        ya, ya_c = _mlstm(pc_, pl_, gates_c, gates_l, conv_w[l], m_norm[l], B, Tc, T, need_ctx)
        yb = _window_attention(pl_["wq"], pc_["wk"], pc_["wv"], sink[l], B, T, Tc, lat_kv=(pl_["wk"], pl_["wv"]))
        yc = _diff_attention(pl_["dq"], pc_["dk"], pc_["dv"], lamv, d_norm[l], lam_init, B, T, Tc,
                             lat_kv=(pl_["dk"], pl_["dv"]))
        wb = w_branch[l].astype(BF16)
        wo = w_out[l].astype(BF16)
        wfi = w_ffn_in[l].astype(BF16)
        wfo = w_ffn_out[l].astype(BF16)
        gfin = g_final.reshape(1, D)
        last = l == depth - 1
        xl = _merge(ya, yb, yc, pl_["gate"], xl, mod_l, wb, wo, tm=tm, rows_per_mod=T)
        xl = _ffn(xl, mod_l, g_ffn_l, wfi, wfo, gfin, tm=tm, rows_per_mod=T, final_norm=last)
        if need_ctx:
            yb_c = _window_attention(pc_["wq"], pc_["wk"], pc_["wv"], sink[l], B, Tc, Tc)
            yc_c = _diff_attention(pc_["dq"], pc_["dk"], pc_["dv"], lamv, d_norm[l], lam_init, B, Tc, Tc)
            xc = _merge(ya_c, yb_c, yc_c, pc_["gate"], xc, mod_c, wb, wo, tm=tmc, rows_per_mod=B * Tc)
            xc = _ffn(xc, mod_c, g_ffn_l, wfi, wfo, gfin, tm=tmc, rows_per_mod=B * Tc, final_norm=False)
    return xl.reshape(B, T, D)
```

```python
import functools
import math

import jax
import jax.numpy as jnp
import numpy as np
from jax import lax
from jax.experimental import pallas as pl
from jax.experimental.pallas import tpu as pltpu

F32 = jnp.float32
BF16 = jnp.bfloat16

GRID_W = 64
BRANCH = 512
N_BRANCH = 3
M_HEADS = 4
M_DIM = 128
W_HEADS = 8
W_KV_HEADS = 2
W_DIM = 64
DF_HEADS = 4
ROPE_DIM = 64
ROPE_BASE = 10000.0
EPS = 1e-6
LANES = 128
CHUNK = 128
VMEM_LIMIT = 56 * 1024 * 1024

_NT = (((1,), (1,)), ((), ()))
LOG2E = math.log2(math.e)
_Q_SCALE = W_DIM ** -0.5 * LOG2E


def _dot(a, b):
    return jnp.dot(a, b, preferred_element_type=F32)


def _dot_nt(a, b):
    return lax.dot_general(a, b, _NT, preferred_element_type=F32)


def _params(sem):
    return pltpu.CompilerParams(dimension_semantics=sem, vmem_limit_bytes=VMEM_LIMIT)


def _resident(shape, index_map):
    return pl.BlockSpec(shape, index_map, pipeline_mode=pl.Buffered(1))


def _adaln_kernel(c_ref, w_ref, b_ref, o_ref):
    c = c_ref[...]
    s = (c * jax.nn.sigmoid(c)).astype(BF16)
    o_ref[0] = _dot(s, w_ref[0].astype(BF16)) + b_ref[0]


def _adaln(cc, w_mod, b_mod):
    L, D, N = w_mod.shape
    R = cc.shape[0]
    tn = 1536
    return pl.pallas_call(
        _adaln_kernel,
        grid=(L, N // tn),
        in_specs=[pl.BlockSpec((R, D), lambda l, j: (0, 0)),
                  pl.BlockSpec((1, D, tn), lambda l, j: (l, 0, j)),
                  pl.BlockSpec((1, 1, tn), lambda l, j: (l, 0, j))],
        out_specs=pl.BlockSpec((1, R, tn), lambda l, j: (l, 0, j)),
        out_shape=jax.ShapeDtypeStruct((L, R, N), F32),
        compiler_params=_params(("arbitrary", "arbitrary")),
        name="adaln",
    )(cc, w_mod, b_mod.reshape(L, 1, N))


def _modnorm(x, g, shift, scale):
    ms = jnp.mean(x * x, axis=-1, keepdims=True)
    return x * lax.rsqrt(ms + EPS) * (g * (1.0 + scale)) + shift


_SEGS = (("mqk", 1024, BF16, "plain"), ("mv", 512, BF16, "plain"), ("mo", 512, BF16, "plain"),
         ("mg", 128, F32, "plain"), ("wq", 512, BF16, "ropeq"), ("wk", 256, BF16, "rope"),
         ("wv", 256, BF16, "plain"), ("dq", 512, BF16, "ropeq"), ("dk", 512, BF16, "rope"),
         ("dv", 512, BF16, "plain"), ("gate", 3072, BF16, "plain"))
_EXT_WIDTH = sum(s[1] for s in _SEGS)
_COL_CHUNK = 512


def _inproj_kernel(x_ref, mod_ref, g_ref, cos_ref, sa_ref, sb_ref, w_ref, *out_refs, d, use_rope):
    x = x_ref[...]
    mod = mod_ref[0]
    h = _modnorm(x, g_ref[...], mod[:, 0:d], mod[:, d:2 * d]).astype(BF16)
    if use_rope:
        cos, sa, sb = cos_ref[...], sa_ref[...], sb_ref[...]
    off = 0
    for (name, width, dt, kind), o_ref in zip(_SEGS, out_refs):
        for c0 in range(0, width, _COL_CHUNK):
            cw = min(_COL_CHUNK, width - c0)
            acc = _dot(h, w_ref[:, off + c0:off + c0 + cw])
            if kind != "plain":
                tiles = []
                for t in range(cw // LANES):
                    a = acc[:, t * LANES:(t + 1) * LANES]
                    if use_rope:
                        a = (a * cos + pltpu.roll(a, LANES - 16, 1) * sa + pltpu.roll(a, 16, 1) * sb)
                    if kind == "ropeq":
                        a = a * _Q_SCALE
                    tiles.append(a)
                acc = jnp.concatenate(tiles, axis=1) if len(tiles) > 1 else tiles[0]
            o_ref[:, c0:c0 + cw] = acc.astype(dt)
        off += width


def _inproj(x2, mod, g, rope, w_ext, *, tm, rows_per_mod, seq, use_rope):
    N, D = x2.shape
    nt = seq // tm
    kern = functools.partial(_inproj_kernel, d=D, use_rope=use_rope)
    rope_spec = pl.BlockSpec((tm, LANES), lambda i: (i % nt, 0))
    out_shape = [jax.ShapeDtypeStruct((N, w), dt) for _, w, dt, _ in _SEGS]
    out_specs = [pl.BlockSpec((tm, w), lambda i: (i, 0)) for _, w, _, _ in _SEGS]
    return pl.pallas_call(
        kern,
        grid=(N // tm,),
        in_specs=[pl.BlockSpec((tm, D), lambda i: (i, 0)),
                  pl.BlockSpec((1, 1, 6 * D), lambda i: ((i * tm) // rows_per_mod, 0, 0)),
                  pl.BlockSpec((1, D), lambda i: (0, 0)),
                  rope_spec, rope_spec, rope_spec,
                  _resident((D, _EXT_WIDTH), lambda i: (0, 0))],
        out_specs=out_specs,
        out_shape=out_shape,
        compiler_params=_params(("arbitrary",)),
        name="inproj_lat" if use_rope else "inproj_ctx",
    )(x2, mod, g, *rope, w_ext)


def _extend_w_in(w):
    D = w.shape[0]
    o = np.cumsum([0, 512, 512, 512, 512, 16, 512, 128, 128, 512, 512, 512, 3072])
    p = [w[:, o[i]:o[i + 1]] for i in range(12)]
    dup = lambda a: jnp.concatenate([a[:, 0:64], a[:, 0:64], a[:, 64:128], a[:, 64:128]], axis=1)
    mg = jnp.concatenate([p[4], jnp.zeros((D, LANES - 16), w.dtype)], axis=1)
    ext = jnp.concatenate([p[0], p[1], p[2], p[3], mg, p[5], dup(p[6]), dup(p[7]), p[8], p[9], p[10], p[11]], axis=1)
    return ext.astype(BF16)


def _rope_tables(n_tokens):
    rows = n_tokens // GRID_W
    r, col = jnp.meshgrid(jnp.arange(rows), jnp.arange(GRID_W), indexing="ij")
    half = ROPE_DIM // 2
    inv = ROPE_BASE ** (-jnp.arange(0, half, 2, dtype=F32) / half)
    ang_r = r.reshape(-1, 1).astype(F32) * inv
    ang_c = col.reshape(-1, 1).astype(F32) * inv
    ang = jnp.concatenate([ang_r, ang_r, ang_c, ang_c], axis=-1)
    cos, sin = jnp.cos(ang), jnp.sin(ang)
    cos2 = jnp.concatenate([cos, cos], axis=-1)
    sin2 = jnp.concatenate([sin, sin], axis=-1)
    first = (jnp.arange(LANES) % 32) < 16
    sin_a = jnp.where(first, -sin2, 0.0)
    sin_b = jnp.where(first, 0.0, sin2)
    return cos2, sin_a, sin_b


def _scan_lanes(x, op, fill, reverse):
    lane = lax.broadcasted_iota(jnp.int32, x.shape, 1)
    d = 1
    while d < LANES:
        if reverse:
            shifted = jnp.where(lane < LANES - d, pltpu.roll(x, LANES - d, 1), fill)
        else:
            shifted = jnp.where(lane >= d, pltpu.roll(x, d, 1), fill)
        x = op(x, shifted)
        d *= 2
    return x


def _gates_kernel(x_ref, bias_ref, o_ref, *, rows):
    x = x_ref[0] + bias_ref[...]
    lane = lax.broadcasted_iota(jnp.int32, (rows, LANES), 1)
    for d, reverse in enumerate((False, True)):
        i_pre = x[(2 * d) * rows:(2 * d + 1) * rows]
        f_pre = x[(2 * d + 1) * rows:(2 * d + 2) * rows]
        log_f = jax.nn.log_sigmoid(f_pre)
        b = _scan_lanes(log_f, jnp.add, 0.0, reverse)
        a = i_pre - b
        pm = _scan_lanes(a, jnp.maximum, -jnp.inf, reverse)
        last = 0 if reverse else LANES - 1
        b_end = jnp.sum(jnp.where(lane == last, b, 0.0), axis=1, keepdims=True)
        a_max = jnp.max(a, axis=1, keepdims=True)
        o_ref[0, 5 * d + 0] = b
        o_ref[0, 5 * d + 1] = a
        o_ref[0, 5 * d + 2] = pm
        o_ref[0, 5 * d + 3] = jnp.broadcast_to(b_end, (rows, LANES))
        o_ref[0, 5 * d + 4] = jnp.broadcast_to(a_max, (rows, LANES))


def _gates(mg, b_gate, B, T):
    nc = T // CHUNK
    rows = M_HEADS * nc
    g = mg[:, :4 * M_HEADS].reshape(B, T, 4 * M_HEADS)
    gt = jnp.transpose(g, (0, 2, 1)).reshape(B, 4 * rows, LANES)
    bias = jnp.repeat(b_gate.astype(F32), nc).reshape(4 * rows, 1)
    return pl.pallas_call(
        functools.partial(_gates_kernel, rows=rows),
        grid=(B,),
        in_specs=[pl.BlockSpec((1, 4 * rows, LANES), lambda b: (b, 0, 0)),
                  pl.BlockSpec((4 * rows, 1), lambda b: (0, 0))],
        out_specs=pl.BlockSpec((1, 10, rows, LANES), lambda b: (b, 0, 0, 0)),
        out_shape=jax.ShapeDtypeStruct((B, 10, rows, LANES), F32),
        compiler_params=_params(("arbitrary",)),
        name="mlstm_gates",
    )(gt, bias)


_CONV_TILE = 256
_HB = 2


def _mlstm_kernel(qc_ref, kc_ref, vc_ref, oc_ref, pc_ref, ql_ref, kl_ref, vl_ref, ol_ref, pl_ref,
                  cwq_ref, cwk_ref, nrm_ref, *rest, tc, tl, need_ctx):
    if need_ctx:
        yl_ref, yc_ref = rest[0], rest[1]
        scr = rest[2:]
    else:
        yl_ref, yc_ref = rest[0], None
        scr = rest[1:]
    xpad, qs, ks, hf, hb, cx, mst = scr
    head0 = pl.program_id(1) * _HB

    def conv_stream(u_ref, w_ref, dst, dst_off, ts, scale):
        for hh in range(_HB):
            xpad[hh, 0:8, :] = jnp.zeros((8, LANES), F32)
            xpad[hh, 8:8 + ts, :] = u_ref[0, :, hh * LANES:(hh + 1) * LANES].astype(F32)
            xpad[hh, 8 + ts:16 + ts, :] = jnp.zeros((8, LANES), F32)
        w = w_ref[...]
        w0, w1, w2 = w[0:1], w[1:2], w[2:3]
        tile = min(_CONV_TILE, ts)

        def body(i, _):
            r0 = pl.multiple_of(i * tile, 8)
            for hh in range(_HB):
                lanes = slice(hh * LANES, (hh + 1) * LANES)
                prev = xpad[hh, pl.ds(r0 + 7, tile), :]
                cur = xpad[hh, pl.ds(r0 + 8, tile), :]
                nxt = xpad[hh, pl.ds(r0 + 9, tile), :]
                y = prev * w0[:, lanes] + cur * w1[:, lanes] + nxt * w2[:, lanes]
                y = y * jax.nn.sigmoid(y) * scale
                dst[pl.ds(pl.multiple_of(dst_off + r0, 16), tile), lanes] = y.astype(BF16)
            return 0

        lax.fori_loop(0, ts // tile, body, 0)

    kscale = M_DIM ** -0.5
    conv_stream(qc_ref, cwq_ref, qs, 0, tc, 1.0)
    conv_stream(kc_ref, cwk_ref, ks, 0, tc, kscale)
    conv_stream(ql_ref, cwq_ref, qs, tc, tl, 1.0)
    conv_stream(kl_ref, cwk_ref, ks, tc, tl, kscale)

    cx[...] = jnp.zeros(cx.shape, F32)
    mst[...] = jnp.zeros(mst.shape, F32)

    row_i = lax.broadcasted_iota(jnp.int32, (CHUNK, CHUNK), 0)
    col_i = lax.broadcasted_iota(jnp.int32, (CHUNK, CHUNK), 1)
    masks = (col_i <= row_i, col_i >= row_i)
    ones_b = jnp.ones((CHUNK, LANES), BF16)
    ones_f = jnp.ones((LANES, CHUNK), F32)

    def colify(row):
        return jnp.broadcast_to(row, (CHUNK, CHUNK)).T

    def chunk_step(hh, d, c, p_ref, v_ref, nc, base, want_h):
        r = (head0 + hh) * nc + c
        lanes = slice(hh * LANES, (hh + 1) * LANES)
        st = 2 * hh + d
        plane = lambda j: p_ref[0, 5 * d + j, pl.ds(r, 1), :]
        b_row, a_row, pm_row, bend_row, amax_row = (plane(j) for j in range(5))
        m_prev = mst[st:st + 1, :]
        m_end = jnp.maximum(m_prev, amax_row)
        w_end = jnp.exp(a_row - m_end)
        decay = jnp.exp(m_prev - m_end)
        row0 = pl.multiple_of(base + c * CHUNK, CHUNK)
        q = qs[pl.ds(row0, CHUNK), lanes]
        k = ks[pl.ds(row0, CHUNK), lanes]
        v = v_ref[0, pl.ds(pl.multiple_of(c * CHUNK, CHUNK), CHUNK), lanes]
        c_old = cx[st]
        if want_h:
            m_col = jnp.maximum(m_prev, colify(pm_row))
            w_intra = jnp.exp(jnp.where(masks[d], a_row - m_col, -jnp.inf))
            w_inter = jnp.exp(m_prev - m_col)
            floor = jnp.exp(-(colify(b_row) + m_col))
            s = _dot_nt(q, k) * w_intra
            intra = _dot(s.astype(BF16), jnp.concatenate([v, ones_b], axis=1))
            inter = _dot_nt(q, c_old.astype(BF16))
            num = intra[:, 0:M_DIM] + w_inter * inter[:, 0:M_DIM]
            den = intra[:, M_DIM:2 * M_DIM] + w_inter * inter[:, M_DIM:2 * M_DIM]
            hval = num / jnp.maximum(jnp.abs(den), floor)
            dst = hf if d == 0 else hb
            dst[pl.ds(row0, CHUNK), lanes] = hval
        vxt = jnp.concatenate([v.astype(F32).T, ones_f], axis=0)
        cx[st] = decay * c_old + _dot((vxt * w_end).astype(BF16), k)
        mst[st:st + 1, :] = bend_row + m_end

    ncc, ncl = tc // CHUNK, tl // CHUNK
    for c in range(ncc):
        for hh in range(_HB):
            chunk_step(hh, 0, c, pc_ref, vc_ref, ncc, 0, need_ctx)
            chunk_step(hh, 1, ncc - 1 - c, pc_ref, vc_ref, ncc, 0, need_ctx)

    def lat_body(c, _):
        for hh in range(_HB):
            chunk_step(hh, 0, c, pl_ref, vl_ref, ncl, tc, True)
            chunk_step(hh, 1, ncl - 1 - c, pl_ref, vl_ref, ncl, tc, True)
        return 0

    lax.fori_loop(0, ncl, lat_body, 0, unroll=8)

    nrm = nrm_ref[...]

    def finish(o_ref, y_ref, base, ts):
        def body(i, _):
            r0 = pl.multiple_of(i * CHUNK, CHUNK)
            hsum = hf[pl.ds(base + r0, CHUNK), :] + hb[pl.ds(base + r0, CHUNK), :]
            y = hsum * jax.nn.sigmoid(o_ref[0, pl.ds(r0, CHUNK), :].astype(F32))
            for hh in range(_HB):
                lanes = slice(hh * LANES, (hh + 1) * LANES)
                yh = y[:, lanes]
                ms = jnp.mean(yh * yh, axis=-1, keepdims=True)
                y_ref[0, pl.ds(r0, CHUNK), lanes] = (yh * lax.rsqrt(ms + EPS) * nrm[:, lanes]).astype(BF16)
            return 0

        lax.fori_loop(0, ts // CHUNK, body, 0, unroll=min(4, ts // CHUNK))

    finish(ol_ref, yl_ref, tc, tl)
    if need_ctx:
        finish(oc_ref, yc_ref, 0, tc)


def _mlstm(ctx_p, lat_p, gates_c, gates_l, conv_w, m_norm, B, tc, tl, need_ctx):
    r3 = lambda a, t: a.reshape(B, t, a.shape[-1])
    qk_c, v_c, o_c = r3(ctx_p["mqk"], tc), r3(ctx_p["mv"], tc), r3(ctx_p["mo"], tc)
    qk_l, v_l, o_l = r3(lat_p["mqk"], tl), r3(lat_p["mv"], tl), r3(lat_p["mo"], tl)
    G = M_HEADS // _HB
    width = _HB * LANES
    head_blk = lambda t, off: pl.BlockSpec((1, t, width), lambda b, g: (b, 0, g + off))
    gate_blk = lambda g: pl.BlockSpec((1,) + g.shape[1:], lambda b, h: (b, 0, 0, 0))
    in_specs = [head_blk(tc, 0), head_blk(tc, G), head_blk(tc, 0), head_blk(tc, 0), gate_blk(gates_c),
                head_blk(tl, 0), head_blk(tl, G), head_blk(tl, 0), head_blk(tl, 0), gate_blk(gates_l),
                pl.BlockSpec((3, width), lambda b, g: (0, g)),
                pl.BlockSpec((3, width), lambda b, g: (0, g + G)),
                pl.BlockSpec((1, width), lambda b, g: (0, g))]
    out_shape = [jax.ShapeDtypeStruct((B, tl, BRANCH), BF16)]
    out_specs = [head_blk(tl, 0)]
    if need_ctx:
        out_shape.append(jax.ShapeDtypeStruct((B, tc, BRANCH), BF16))
        out_specs.append(head_blk(tc, 0))
    tot = tc + tl
    scratch = [pltpu.VMEM((_HB, max(tc, tl) + 16, LANES), F32),
               pltpu.VMEM((tot, width), BF16), pltpu.VMEM((tot, width), BF16),
               pltpu.VMEM((tot, width), F32), pltpu.VMEM((tot, width), F32),
               pltpu.VMEM((2 * _HB, 2 * LANES, LANES), F32), pltpu.VMEM((8, LANES), F32)]
    outs = pl.pallas_call(
        functools.partial(_mlstm_kernel, tc=tc, tl=tl, need_ctx=need_ctx),
        grid=(B, G),
        in_specs=in_specs,
        out_specs=out_specs,
        out_shape=out_shape,
        scratch_shapes=scratch,
        compiler_params=_params(("arbitrary", "arbitrary")),
        name="mlstm",
    )(qk_c, qk_c, v_c, o_c, gates_c, qk_l, qk_l, v_l, o_l, gates_l,
      conv_w, conv_w, m_norm.reshape(1, -1))
    ya = outs[0].reshape(B * tl, BRANCH)
    ya_c = outs[1].reshape(B * tc, BRANCH) if need_ctx else None
    return ya, ya_c


_WB = 128


def _win_kernel(sink_ref, q_ref, *refs, has_latent, qb, seq):
    if has_latent:
        kl_ref, vl_ref, kx_ref, vx_ref, o_ref = refs
    else:
        kx_ref, vx_ref, o_ref = refs
    n = pl.program_id(1)
    rep = W_HEADS // W_KV_HEADS
    rows = rep * _WB
    span = 3 * _WB
    lane = lax.broadcasted_iota(jnp.int32, (_WB, LANES), 1)
    left = lane < W_DIM
    zero = jnp.zeros((_WB, LANES), BF16)
    if has_latent:
        qq = jnp.bitwise_and(lax.broadcasted_iota(jnp.int32, (rows, span), 0), _WB - 1)
        dist = lax.broadcasted_iota(jnp.int32, (rows, span), 1) - qq
    windows = {}

    def window_of(j):
        if j not in windows:
            q_start = (n * qb + j) * _WB
            k_start = pl.multiple_of(jnp.clip(q_start - _WB, 0, seq - span), _WB)
            off = pltpu.bitcast(dist + (k_start - q_start + _WB), jnp.uint32)
            windows[j] = (k_start, off <= jnp.uint32(2 * _WB))
        return windows[j]

    def logits_of(j, g):
        r0 = j * _WB
        pieces = []
        for t in range(rep // 2):
            qt = q_ref[0, r0:r0 + _WB, (g * rep // 2 + t) * LANES:(g * rep // 2 + t + 1) * LANES]
            pieces += [jnp.where(left, qt, zero), jnp.where(left, zero, qt)]
        lhs = jnp.concatenate(pieces, axis=0)
        gsl = slice(g * LANES, (g + 1) * LANES)
        logits = [_dot_nt(lhs, kx_ref[0, :, gsl])]
        vals = [vx_ref[0, :, gsl]]
        if has_latent:
            k_start, near = window_of(j)
            s_win = _dot_nt(lhs, kl_ref[0, pl.ds(k_start, span), gsl])
            logits.append(jnp.where(near, s_win, -jnp.inf))
            vals.append(vl_ref[0, pl.ds(k_start, span), gsl])
        return logits, vals

    def finish(j, g, logits, vals):
        r0 = j * _WB
        sink_rep = jnp.concatenate(
            [jnp.full((_WB, LANES), sink_ref[g * rep + r] * LOG2E, F32) for r in range(rep)], axis=0)
        m_part = None
        for s in logits:
            for t in range(s.shape[1] // LANES):
                st = s[:, t * LANES:(t + 1) * LANES]
                m_part = st if m_part is None else jnp.maximum(m_part, st)
        m = jnp.maximum(sink_rep, jnp.max(m_part, axis=-1, keepdims=True))
        acc = None
        for s, v in zip(logits, vals):
            p = jnp.concatenate([jnp.exp2(s[:, t * LANES:(t + 1) * LANES] - m)
                                 for t in range(s.shape[1] // LANES)], axis=1)
            part = _dot(p.astype(BF16), jnp.concatenate([v, jnp.ones(v.shape, BF16)], axis=1))
            acc = part if acc is None else acc + part
        o = acc[:, 0:LANES] / (acc[:, LANES:2 * LANES] + jnp.exp2(sink_rep - m))
        for t in range(rep // 2):
            tile = jnp.where(left, o[(2 * t) * _WB:(2 * t + 1) * _WB], o[(2 * t + 1) * _WB:(2 * t + 2) * _WB])
            c0 = (g * rep // 2 + t) * LANES
            o_ref[0, r0:r0 + _WB, c0:c0 + LANES] = tile.astype(BF16)

    streams = [(j, g) for j in range(qb) for g in range(W_KV_HEADS)]
    ahead = 4
    pending = []
    for i, (j, g) in enumerate(streams):
        pending.append((j, g) + logits_of(j, g))
        if i >= ahead:
            finish(*pending.pop(0))
    for item in pending:
        finish(*item)


def _window_attention(q, kx, vx, sink, B, tq, tcx, lat_kv=None, qb=4):
    q3 = q.reshape(B, tq, BRANCH)
    kx3, vx3 = kx.reshape(B, tcx, 2 * LANES), vx.reshape(B, tcx, 2 * LANES)
    nb = tq // _WB
    qb = math.gcd(qb, nb)
    has_latent = lat_kv is not None
    whole = lambda t: pl.BlockSpec((1, t, 2 * LANES), lambda b, n: (b, 0, 0))
    in_specs = [pl.BlockSpec(memory_space=pltpu.SMEM),
                pl.BlockSpec((1, qb * _WB, BRANCH), lambda b, n: (b, n, 0))]
    args = [sink.astype(F32), q3]
    if has_latent:
        assert tq >= 3 * _WB
        in_specs += [whole(tq), whole(tq)]
        args += [a.reshape(B, tq, 2 * LANES) for a in lat_kv]
    in_specs += [whole(tcx), whole(tcx)]
    args += [kx3, vx3]
    out = pl.pallas_call(
        functools.partial(_win_kernel, has_latent=has_latent, qb=qb, seq=tq),
        grid=(B, nb // qb),
        in_specs=in_specs,
        out_specs=pl.BlockSpec((1, qb * _WB, BRANCH), lambda b, n: (b, n, 0)),
        out_shape=jax.ShapeDtypeStruct((B, tq, BRANCH), BF16),
        compiler_params=_params(("arbitrary", "arbitrary")),
        name="win_lat" if has_latent else "win_ctx",
    )(*args)
    return out.reshape(B * tq, BRANCH)


def _diff_kernel(lam_ref, q_ref, *refs, tq, nsub, tk, tcx, tlat, lam_init):
    if tlat:
        kx_ref, vx_ref, kl_ref, vl_ref, nrm_ref, o_ref = refs
    else:
        kx_ref, vx_ref, nrm_ref, o_ref = refs
        kl_ref = vl_ref = None
    sub = tq // nsub
    lane = lax.broadcasted_iota(jnp.int32, (sub, LANES), 1)
    left = lane < LANES // 2
    zero = jnp.zeros((sub, LANES), BF16)
    tiles = [(kx_ref, vx_ref, 0, tcx)] + [(kl_ref, vl_ref, j * tk, tk) for j in range(tlat // tk)]
    ones = jnp.ones((max(tk, tcx), LANES), BF16)

    lhs = []
    for u in range(nsub):
        q = q_ref[0, u * sub:(u + 1) * sub, :]
        lhs += [jnp.where(left, q, zero), jnp.where(left, zero, q)]
    state = [None] * (2 * nsub)
    for k_ref, v_ref, r0, n in tiles:
        vx = jnp.concatenate([v_ref[0, r0:r0 + n, :], ones[0:n]], axis=1)
        k = k_ref[0, r0:r0 + n, :]
        for i in range(2 * nsub):
            s = _dot_nt(lhs[i], k)
            m_part = s[:, 0:LANES]
            for t in range(1, n // LANES):
                m_part = jnp.maximum(m_part, s[:, t * LANES:(t + 1) * LANES])
            m_j = jnp.max(m_part, axis=-1, keepdims=True)
            part = _dot(jnp.exp2(s - m_j).astype(BF16), vx)
            if state[i] is None:
                state[i] = (m_j, part)
            else:
                m_run, acc = state[i]
                m_new = jnp.maximum(m_run, m_j)
                state[i] = (m_new, acc * jnp.exp2(m_run - m_new) + part * jnp.exp2(m_j - m_new))
    outs = [acc[:, 0:LANES] / acc[:, LANES:2 * LANES] for _, acc in state]
    lv = lam_ref[...]
    s1 = jnp.sum(lv[0:1] * lv[1:2], axis=-1, keepdims=True)
    s2 = jnp.sum(lv[2:3] * lv[3:4], axis=-1, keepdims=True)
    lam = jnp.exp(s1) - jnp.exp(s2) + lam_init
    for u in range(nsub):
        y = outs[2 * u] - lam * outs[2 * u + 1]
        ms = jnp.mean(y * y, axis=-1, keepdims=True)
        o_ref[0, u * sub:(u + 1) * sub, :] = (y * lax.rsqrt(ms + EPS) * nrm_ref[...] * (1.0 - lam_init)).astype(BF16)


def _diff_attention(q, kx, vx, lamv, d_norm, lam_init, B, tq_total, tcx, lat_kv=None, sub=1024, nsub=2, tk=1024):
    q3 = q.reshape(B, tq_total, BRANCH)
    kx3, vx3 = kx.reshape(B, tcx, BRANCH), vx.reshape(B, tcx, BRANCH)
    sub = min(sub, tq_total)
    nsub = math.gcd(nsub, tq_total // sub)
    tq = sub * nsub
    head_blk = lambda t, f: pl.BlockSpec((1, t, LANES), f)
    whole = lambda b, h, i: (b, 0, h)
    in_specs = [pl.BlockSpec((8, LANES), lambda b, h, i: (0, 0)),
                head_blk(tq, lambda b, h, i: (b, i, h)),
                head_blk(tcx, whole), head_blk(tcx, whole)]
    args = [lamv, q3, kx3, vx3]
    tlat = 0
    if lat_kv is not None:
        tlat = tq_total
        k3, v3 = (a.reshape(B, tlat, BRANCH) for a in lat_kv)
        in_specs += [head_blk(tlat, whole), head_blk(tlat, whole)]
        args += [k3, v3]
    in_specs.append(pl.BlockSpec((1, LANES), lambda b, h, i: (0, h)))
    args.append(d_norm.reshape(1, -1))
    tk = min(tk, max(tlat, LANES))
    out = pl.pallas_call(
        functools.partial(_diff_kernel, tq=tq, nsub=nsub, tk=tk, tcx=tcx, tlat=tlat, lam_init=lam_init),
        grid=(B, DF_HEADS, tq_total // tq),
        in_specs=in_specs,
        out_specs=head_blk(tq, lambda b, h, i: (b, i, h)),
        out_shape=jax.ShapeDtypeStruct((B, tq_total, BRANCH), BF16),
        compiler_params=_params(("arbitrary", "arbitrary", "arbitrary")),
        name="diff_lat" if tlat else "diff_ctx",
    )(*args)
    return out.reshape(B * tq_total, BRANCH)


def _merge_kernel(ya_ref, yb_ref, yc_ref, gate_ref, x_ref, mod_ref, wb_ref, wo_ref, o_ref, *, d):
    merged = None
    for i, y_ref in enumerate((ya_ref, yb_ref, yc_ref)):
        gate = jax.nn.sigmoid(gate_ref[:, i * d:(i + 1) * d].astype(F32))
        term = gate * _dot(y_ref[...], wb_ref[i])
        merged = term if merged is None else merged + term
    out = _dot(merged.astype(BF16), wo_ref[...])
    gt = mod_ref[0][:, 2 * d:3 * d]
    o_ref[...] = x_ref[...] + gt * out


def _merge(ya, yb, yc, gate, x2, mod, wb, wo, *, tm, rows_per_mod):
    N, D = x2.shape
    row = lambda w: pl.BlockSpec((tm, w), lambda i: (i, 0))
    return pl.pallas_call(
        functools.partial(_merge_kernel, d=D),
        grid=(N // tm,),
        in_specs=[row(BRANCH), row(BRANCH), row(BRANCH), row(N_BRANCH * D), row(D),
                  pl.BlockSpec((1, 1, 6 * D), lambda i: ((i * tm) // rows_per_mod, 0, 0)),
                  _resident((N_BRANCH, BRANCH, D), lambda i: (0, 0, 0)),
                  _resident((D, D), lambda i: (0, 0))],
        out_specs=row(D),
        out_shape=jax.ShapeDtypeStruct((N, D), F32),
        compiler_params=_params(("arbitrary",)),
        name="merge",
    )(ya, yb, yc, gate, x2, mod, wb, wo)


_FFN_CHUNK = 256


def _ffn_kernel(x_ref, mod_ref, g_ref, wi_ref, wo_ref, gf_ref, o_ref, *, d, hidden, final_norm):
    x = x_ref[...]
    mod = mod_ref[0]
    h = _modnorm(x, g_ref[...], mod[:, 3 * d:4 * d], mod[:, 4 * d:5 * d]).astype(BF16)
    acc = None
    for c0 in range(0, hidden, _FFN_CHUNK):
        gate = _dot(h, wi_ref[:, c0:c0 + _FFN_CHUNK])
        up = _dot(h, wi_ref[:, hidden + c0:hidden + c0 + _FFN_CHUNK])
        a = (gate * jax.nn.sigmoid(gate) * up).astype(BF16)
        part = _dot(a, wo_ref[c0:c0 + _FFN_CHUNK, :])
        acc = part if acc is None else acc + part
    y = x + mod[:, 5 * d:6 * d] * acc
    if final_norm:
        ms = jnp.mean(y * y, axis=-1, keepdims=True)
        y = y * lax.rsqrt(ms + EPS) * gf_ref[...]
    o_ref[...] = y


def _ffn(x2, mod, g, wi, wo, g_final, *, tm, rows_per_mod, final_norm):
    N, D = x2.shape
    hidden = wo.shape[0]
    return pl.pallas_call(
        functools.partial(_ffn_kernel, d=D, hidden=hidden, final_norm=final_norm),
        grid=(N // tm,),
        in_specs=[pl.BlockSpec((tm, D), lambda i: (i, 0)),
                  pl.BlockSpec((1, 1, 6 * D), lambda i: ((i * tm) // rows_per_mod, 0, 0)),
                  pl.BlockSpec((1, D), lambda i: (0, 0)),
                  _resident((D, 2 * hidden), lambda i: (0, 0)),
                  _resident((hidden, D), lambda i: (0, 0)),
                  pl.BlockSpec((1, D), lambda i: (0, 0))],
        out_specs=pl.BlockSpec((tm, D), lambda i: (i, 0)),
        out_shape=jax.ShapeDtypeStruct((N, D), F32),
        compiler_params=_params(("arbitrary",)),
        name="ffn",
    )(x2, mod, g, wi, wo, g_final)


def _row_tile(n, want=512):
    t = want
    while n % t:
        t //= 2
    return t


def kernel(x, c, ctx, c_ctx, w_mod, b_mod, g_mix, g_ffn, w_in, b_gate, conv_w, m_norm, sink,
           lam_q1, lam_k1, lam_q2, lam_k2, d_norm, w_branch, w_out, w_ffn_in, w_ffn_out, g_final):
    B, T, D = x.shape
    Tc = ctx.shape[1]
    depth = w_mod.shape[0]
    assert T % CHUNK == 0 and Tc % CHUNK == 0 and T % GRID_W == 0

    n_rows = -(-(B + 1) // 16) * 16
    cc = jnp.concatenate([c, c_ctx[None, :], jnp.zeros((n_rows - B - 1, D), F32)], axis=0)
    mods = _adaln(cc, w_mod, b_mod)

    rope = _rope_tables(T)
    no_rope = tuple(jnp.zeros((Tc, LANES), F32) for _ in range(3))
    tm = _row_tile(T)
    tmc = _row_tile(Tc)
    names = [s[0] for s in _SEGS]

    xl = x.reshape(B * T, D)
    xc = ctx.reshape(B * Tc, D)
    for l in range(depth):
        need_ctx = l < depth - 1
        lam_init = 0.8 - 0.6 * math.exp(-0.3 * l)
        mod_l = mods[l, :B].reshape(B, 1, 6 * D)
        mod_c = mods[l, B:B + 1].reshape(1, 1, 6 * D)
        w_ext = _extend_w_in(w_in[l])
        g_mix_l = g_mix[l].reshape(1, D)
        g_ffn_l = g_ffn[l].reshape(1, D)
        lamv = jnp.zeros((8, LANES), F32)
        for i, v in enumerate((lam_q1[l], lam_k1[l], lam_q2[l], lam_k2[l])):
            lamv = lamv.at[i, :v.shape[0]].set(v.astype(F32))

        pl_ = dict(zip(names, _inproj(xl, mod_l, g_mix_l, rope, w_ext, tm=tm, rows_per_mod=T, seq=T, use_rope=True)))
        pc_ = dict(zip(names, _inproj(xc, mod_c, g_mix_l, no_rope, w_ext, tm=tmc, rows_per_mod=B * Tc, seq=Tc,
                                      use_rope=False)))

        gates_l = _gates(pl_["mg"], b_gate[l], B, T)
        gates_c = _gates(pc_["mg"], b_gate[l], B, Tc)
        ya, ya_c = _mlstm(pc_, pl_, gates_c, gates_l, conv_w[l], m_norm[l], B, Tc, T, need_ctx)
        yb = _window_attention(pl_["wq"], pc_["wk"], pc_["wv"], sink[l], B, T, Tc, lat_kv=(pl_["wk"], pl_["wv"]))
        yc = _diff_attention(pl_["dq"], pc_["dk"], pc_["dv"], lamv, d_norm[l], lam_init, B, T, Tc,
                             lat_kv=(pl_["dk"], pl_["dv"]))
        wb = w_branch[l].astype(BF16)
        wo = w_out[l].astype(BF16)
        wfi = w_ffn_in[l].astype(BF16)
        wfo = w_ffn_out[l].astype(BF16)
        gfin = g_final.reshape(1, D)
        last = l == depth - 1
        xl = _merge(ya, yb, yc, pl_["gate"], xl, mod_l, wb, wo, tm=tm, rows_per_mod=T)
        xl = _ffn(xl, mod_l, g_ffn_l, wfi, wfo, gfin, tm=tm, rows_per_mod=T, final_norm=last)
        if need_ctx:
            yb_c = _window_attention(pc_["wq"], pc_["wk"], pc_["wv"], sink[l], B, Tc, Tc)
            yc_c = _diff_attention(pc_["dq"], pc_["dk"], pc_["dv"], lamv, d_norm[l], lam_init, B, Tc, Tc)
            xc = _merge(ya_c, yb_c, yc_c, pc_["gate"], xc, mod_c, wb, wo, tm=tmc, rows_per_mod=B * Tc)
            xc = _ffn(xc, mod_c, g_ffn_l, wfi, wfo, gfin, tm=tmc, rows_per_mod=B * Tc, final_norm=False)
    return xl.reshape(B, T, D)
```

```python
import functools
import math

import jax
import jax.numpy as jnp
import numpy as np
from jax import lax
from jax.experimental import pallas as pl
from jax.experimental.pallas import tpu as pltpu

F32 = jnp.float32
BF16 = jnp.bfloat16

GRID_W = 64
BRANCH = 512
N_BRANCH = 3
M_HEADS = 4
M_DIM = 128
W_HEADS = 8
W_KV_HEADS = 2
W_DIM = 64
DF_HEADS = 4
ROPE_DIM = 64
ROPE_BASE = 10000.0
EPS = 1e-6
LANES = 128
CHUNK = 128
VMEM_LIMIT = 56 * 1024 * 1024

_NT = (((1,), (1,)), ((), ()))
LOG2E = math.log2(math.e)
_Q_SCALE = W_DIM ** -0.5 * LOG2E


def _dot(a, b):
    return jnp.dot(a, b, preferred_element_type=F32)


def _dot_nt(a, b):
    return lax.dot_general(a, b, _NT, preferred_element_type=F32)


def _params(sem):
    return pltpu.CompilerParams(dimension_semantics=sem, vmem_limit_bytes=VMEM_LIMIT)


def _resident(shape, index_map):
    return pl.BlockSpec(shape, index_map, pipeline_mode=pl.Buffered(1))


def _adaln_kernel(c_ref, w_ref, b_ref, o_ref):
    c = c_ref[...]
    s = (c * jax.nn.sigmoid(c)).astype(BF16)
    o_ref[0] = _dot(s, w_ref[0].astype(BF16)) + b_ref[0]


def _adaln(cc, w_mod, b_mod):
    L, D, N = w_mod.shape
    R = cc.shape[0]
    tn = 1536
    return pl.pallas_call(
        _adaln_kernel,
        grid=(L, N // tn),
        in_specs=[pl.BlockSpec((R, D), lambda l, j: (0, 0)),
                  pl.BlockSpec((1, D, tn), lambda l, j: (l, 0, j)),
                  pl.BlockSpec((1, 1, tn), lambda l, j: (l, 0, j))],
        out_specs=pl.BlockSpec((1, R, tn), lambda l, j: (l, 0, j)),
        out_shape=jax.ShapeDtypeStruct((L, R, N), F32),
        compiler_params=_params(("arbitrary", "arbitrary")),
        name="adaln",
    )(cc, w_mod, b_mod.reshape(L, 1, N))


def _modnorm(x, g, shift, scale):
    ms = jnp.mean(x * x, axis=-1, keepdims=True)
    return x * lax.rsqrt(ms + EPS) * (g * (1.0 + scale)) + shift


_SEGS = (("mqk", 1024, BF16, "plain"), ("mv", 512, BF16, "plain"), ("mo", 512, BF16, "plain"),
         ("mg", 128, F32, "plain"), ("wq", 512, BF16, "ropeq"), ("wk", 256, BF16, "rope"),
         ("wv", 256, BF16, "plain"), ("dq", 512, BF16, "ropeq"), ("dk", 512, BF16, "rope"),
         ("dv", 512, BF16, "plain"), ("gate", 3072, BF16, "plain"))
_EXT_WIDTH = sum(s[1] for s in _SEGS)
_COL_CHUNK = 512


def _inproj_kernel(x_ref, mod_ref, g_ref, cos_ref, sa_ref, sb_ref, w_ref, *out_refs, d, use_rope):
    x = x_ref[...]
    mod = mod_ref[0]
    h = _modnorm(x, g_ref[...], mod[:, 0:d], mod[:, d:2 * d]).astype(BF16)
    if use_rope:
        cos, sa, sb = cos_ref[...], sa_ref[...], sb_ref[...]
    off = 0
    for (name, width, dt, kind), o_ref in zip(_SEGS, out_refs):
        for c0 in range(0, width, _COL_CHUNK):
            cw = min(_COL_CHUNK, width - c0)
            acc = _dot(h, w_ref[:, off + c0:off + c0 + cw])
            if kind != "plain":
                tiles = []
                for t in range(cw // LANES):
                    a = acc[:, t * LANES:(t + 1) * LANES]
                    if use_rope:
                        a = (a * cos + pltpu.roll(a, LANES - 16, 1) * sa + pltpu.roll(a, 16, 1) * sb)
                    if kind == "ropeq":
                        a = a * _Q_SCALE
                    tiles.append(a)
                acc = jnp.concatenate(tiles, axis=1) if len(tiles) > 1 else tiles[0]
            o_ref[:, c0:c0 + cw] = acc.astype(dt)
        off += width


def _inproj(x2, mod, g, rope, w_ext, *, tm, rows_per_mod, seq, use_rope):
    N, D = x2.shape
    nt = seq // tm
    kern = functools.partial(_inproj_kernel, d=D, use_rope=use_rope)
    rope_spec = pl.BlockSpec((tm, LANES), lambda i: (i % nt, 0))
    out_shape = [jax.ShapeDtypeStruct((N, w), dt) for _, w, dt, _ in _SEGS]
    out_specs = [pl.BlockSpec((tm, w), lambda i: (i, 0)) for _, w, _, _ in _SEGS]
    return pl.pallas_call(
        kern,
        grid=(N // tm,),
        in_specs=[pl.BlockSpec((tm, D), lambda i: (i, 0)),
                  pl.BlockSpec((1, 1, 6 * D), lambda i: ((i * tm) // rows_per_mod, 0, 0)),
                  pl.BlockSpec((1, D), lambda i: (0, 0)),
                  rope_spec, rope_spec, rope_spec,
                  _resident((D, _EXT_WIDTH), lambda i: (0, 0))],
        out_specs=out_specs,
        out_shape=out_shape,
        compiler_params=_params(("arbitrary",)),
        name="inproj_lat" if use_rope else "inproj_ctx",
    )(x2, mod, g, *rope, w_ext)


def _extend_w_in(w):
    D = w.shape[0]
    o = np.cumsum([0, 512, 512, 512, 512, 16, 512, 128, 128, 512, 512, 512, 3072])
    p = [w[:, o[i]:o[i + 1]] for i in range(12)]
    dup = lambda a: jnp.concatenate([a[:, 0:64], a[:, 0:64], a[:, 64:128], a[:, 64:128]], axis=1)
    mg = jnp.concatenate([p[4], jnp.zeros((D, LANES - 16), w.dtype)], axis=1)
    ext = jnp.concatenate([p[0], p[1], p[2], p[3], mg, p[5], dup(p[6]), dup(p[7]), p[8], p[9], p[10], p[11]], axis=1)
    return ext.astype(BF16)


def _rope_tables(n_tokens):
    rows = n_tokens // GRID_W
    r, col = jnp.meshgrid(jnp.arange(rows), jnp.arange(GRID_W), indexing="ij")
    half = ROPE_DIM // 2
    inv = ROPE_BASE ** (-jnp.arange(0, half, 2, dtype=F32) / half)
    ang_r = r.reshape(-1, 1).astype(F32) * inv
    ang_c = col.reshape(-1, 1).astype(F32) * inv
    ang = jnp.concatenate([ang_r, ang_r, ang_c, ang_c], axis=-1)
    cos, sin = jnp.cos(ang), jnp.sin(ang)
    cos2 = jnp.concatenate([cos, cos], axis=-1)
    sin2 = jnp.concatenate([sin, sin], axis=-1)
    first = (jnp.arange(LANES) % 32) < 16
    sin_a = jnp.where(first, -sin2, 0.0)
    sin_b = jnp.where(first, 0.0, sin2)
    return cos2, sin_a, sin_b


def _scan_lanes(x, op, fill, reverse):
    lane = lax.broadcasted_iota(jnp.int32, x.shape, 1)
    d = 1
    while d < LANES:
        if reverse:
            shifted = jnp.where(lane < LANES - d, pltpu.roll(x, LANES - d, 1), fill)
        else:
            shifted = jnp.where(lane >= d, pltpu.roll(x, d, 1), fill)
        x = op(x, shifted)
        d *= 2
    return x


def _gates_kernel(x_ref, bias_ref, o_ref, *, rows):
    x = x_ref[0] + bias_ref[...]
    lane = lax.broadcasted_iota(jnp.int32, (rows, LANES), 1)
    for d, reverse in enumerate((False, True)):
        i_pre = x[(2 * d) * rows:(2 * d + 1) * rows]
        f_pre = x[(2 * d + 1) * rows:(2 * d + 2) * rows]
        log_f = jax.nn.log_sigmoid(f_pre)
        b = _scan_lanes(log_f, jnp.add, 0.0, reverse)
        a = i_pre - b
        pm = _scan_lanes(a, jnp.maximum, -jnp.inf, reverse)
        last = 0 if reverse else LANES - 1
        b_end = jnp.sum(jnp.where(lane == last, b, 0.0), axis=1, keepdims=True)
        a_max = jnp.max(a, axis=1, keepdims=True)
        o_ref[0, 5 * d + 0] = b
        o_ref[0, 5 * d + 1] = a
        o_ref[0, 5 * d + 2] = pm
        o_ref[0, 5 * d + 3] = jnp.broadcast_to(b_end, (rows, LANES))
        o_ref[0, 5 * d + 4] = jnp.broadcast_to(a_max, (rows, LANES))


def _gates(mg, b_gate, B, T):
    nc = T // CHUNK
    rows = M_HEADS * nc
    g = mg[:, :4 * M_HEADS].reshape(B, T, 4 * M_HEADS)
    gt = jnp.transpose(g, (0, 2, 1)).reshape(B, 4 * rows, LANES)
    bias = jnp.repeat(b_gate.astype(F32), nc).reshape(4 * rows, 1)
    return pl.pallas_call(
        functools.partial(_gates_kernel, rows=rows),
        grid=(B,),
        in_specs=[pl.BlockSpec((1, 4 * rows, LANES), lambda b: (b, 0, 0)),
                  pl.BlockSpec((4 * rows, 1), lambda b: (0, 0))],
        out_specs=pl.BlockSpec((1, 10, rows, LANES), lambda b: (b, 0, 0, 0)),
        out_shape=jax.ShapeDtypeStruct((B, 10, rows, LANES), F32),
        compiler_params=_params(("arbitrary",)),
        name="mlstm_gates",
    )(gt, bias)


_CONV_TILE = 256
_HB = 2


def _mlstm_kernel(qc_ref, kc_ref, vc_ref, oc_ref, pc_ref, ql_ref, kl_ref, vl_ref, ol_ref, pl_ref,
                  cwq_ref, cwk_ref, nrm_ref, *rest, tc, tl, need_ctx):
    if need_ctx:
        yl_ref, yc_ref = rest[0], rest[1]
        scr = rest[2:]
    else:
        yl_ref, yc_ref = rest[0], None
        scr = rest[1:]
    xpad, qs, ks, hf, hb, cx, mst = scr
    head0 = pl.program_id(1) * _HB

    def conv_stream(u_ref, w_ref, dst, dst_off, ts, scale):
        for hh in range(_HB):
            xpad[hh, 0:8, :] = jnp.zeros((8, LANES), F32)
            xpad[hh, 8:8 + ts, :] = u_ref[0, :, hh * LANES:(hh + 1) * LANES].astype(F32)
            xpad[hh, 8 + ts:16 + ts, :] = jnp.zeros((8, LANES), F32)
        w = w_ref[...]
        w0, w1, w2 = w[0:1], w[1:2], w[2:3]
        tile = min(_CONV_TILE, ts)

        def body(i, _):
            r0 = pl.multiple_of(i * tile, 8)
            for hh in range(_HB):
                lanes = slice(hh * LANES, (hh + 1) * LANES)
                prev = xpad[hh, pl.ds(r0 + 7, tile), :]
                cur = xpad[hh, pl.ds(r0 + 8, tile), :]
                nxt = xpad[hh, pl.ds(r0 + 9, tile), :]
                y = prev * w0[:, lanes] + cur * w1[:, lanes] + nxt * w2[:, lanes]
                y = y * jax.nn.sigmoid(y) * scale
                dst[pl.ds(pl.multiple_of(dst_off + r0, 16), tile), lanes] = y.astype(BF16)
            return 0

        lax.fori_loop(0, ts // tile, body, 0)

    kscale = M_DIM ** -0.5
    conv_stream(qc_ref, cwq_ref, qs, 0, tc, 1.0)
    conv_stream(kc_ref, cwk_ref, ks, 0, tc, kscale)
    conv_stream(ql_ref, cwq_ref, qs, tc, tl, 1.0)
    conv_stream(kl_ref, cwk_ref, ks, tc, tl, kscale)

    cx[...] = jnp.zeros(cx.shape, F32)
    mst[...] = jnp.zeros(mst.shape, F32)

    row_i = lax.broadcasted_iota(jnp.int32, (CHUNK, CHUNK), 0)
    col_i = lax.broadcasted_iota(jnp.int32, (CHUNK, CHUNK), 1)
    masks = (col_i <= row_i, col_i >= row_i)
    ones_b = jnp.ones((CHUNK, LANES), BF16)
    ones_f = jnp.ones((LANES, CHUNK), F32)

    def colify(row):
        return jnp.broadcast_to(row, (CHUNK, CHUNK)).T

    def chunk_step(hh, d, c, p_ref, v_ref, nc, base, want_h):
        r = (head0 + hh) * nc + c
        lanes = slice(hh * LANES, (hh + 1) * LANES)
        st = 2 * hh + d
        plane = lambda j: p_ref[0, 5 * d + j, pl.ds(r, 1), :]
        b_row, a_row, pm_row, bend_row, amax_row = (plane(j) for j in range(5))
        m_prev = mst[st:st + 1, :]
        m_end = jnp.maximum(m_prev, amax_row)
        w_end = jnp.exp(a_row - m_end)
        decay = jnp.exp(m_prev - m_end)
        row0 = pl.multiple_of(base + c * CHUNK, CHUNK)
        q = qs[pl.ds(row0, CHUNK), lanes]
        k = ks[pl.ds(row0, CHUNK), lanes]
        v = v_ref[0, pl.ds(pl.multiple_of(c * CHUNK, CHUNK), CHUNK), lanes]
        c_old = cx[st]
        if want_h:
            m_col = jnp.maximum(m_prev, colify(pm_row))
            w_intra = jnp.exp(jnp.where(masks[d], a_row - m_col, -jnp.inf))
            w_inter = jnp.exp(m_prev - m_col)
            floor = jnp.exp(-(colify(b_row) + m_col))
            s = _dot_nt(q, k) * w_intra
            intra = _dot(s.astype(BF16), jnp.concatenate([v, ones_b], axis=1))
            inter = _dot_nt(q, c_old.astype(BF16))
            num = intra[:, 0:M_DIM] + w_inter * inter[:, 0:M_DIM]
            den = intra[:, M_DIM:2 * M_DIM] + w_inter * inter[:, M_DIM:2 * M_DIM]
            hval = num / jnp.maximum(jnp.abs(den), floor)
            dst = hf if d == 0 else hb
            dst[pl.ds(row0, CHUNK), lanes] = hval
        vxt = jnp.concatenate([v.astype(F32).T, ones_f], axis=0)
        cx[st] = decay * c_old + _dot((vxt * w_end).astype(BF16), k)
        mst[st:st + 1, :] = bend_row + m_end

    ncc, ncl = tc // CHUNK, tl // CHUNK
    for c in range(ncc):
        for hh in range(_HB):
            chunk_step(hh, 0, c, pc_ref, vc_ref, ncc, 0, need_ctx)
            chunk_step(hh, 1, ncc - 1 - c, pc_ref, vc_ref, ncc, 0, need_ctx)

    def lat_body(c, _):
        for hh in range(_HB):
            chunk_step(hh, 0, c, pl_ref, vl_ref, ncl, tc, True)
            chunk_step(hh, 1, ncl - 1 - c, pl_ref, vl_ref, ncl, tc, True)
        return 0

    lax.fori_loop(0, ncl, lat_body, 0, unroll=8)

    nrm = nrm_ref[...]

    def finish(o_ref, y_ref, base, ts):
        def body(i, _):
            r0 = pl.multiple_of(i * CHUNK, CHUNK)
            hsum = hf[pl.ds(base + r0, CHUNK), :] + hb[pl.ds(base + r0, CHUNK), :]
            y = hsum * jax.nn.sigmoid(o_ref[0, pl.ds(r0, CHUNK), :].astype(F32))
            for hh in range(_HB):
                lanes = slice(hh * LANES, (hh + 1) * LANES)
                yh = y[:, lanes]
                ms = jnp.mean(yh * yh, axis=-1, keepdims=True)
                y_ref[0, pl.ds(r0, CHUNK), lanes] = (yh * lax.rsqrt(ms + EPS) * nrm[:, lanes]).astype(BF16)
            return 0

        lax.fori_loop(0, ts // CHUNK, body, 0, unroll=min(4, ts // CHUNK))

    finish(ol_ref, yl_ref, tc, tl)
    if need_ctx:
        finish(oc_ref, yc_ref, 0, tc)


def _mlstm(ctx_p, lat_p, gates_c, gates_l, conv_w, m_norm, B, tc, tl, need_ctx):
    r3 = lambda a, t: a.reshape(B, t, a.shape[-1])
    qk_c, v_c, o_c = r3(ctx_p["mqk"], tc), r3(ctx_p["mv"], tc), r3(ctx_p["mo"], tc)
    qk_l, v_l, o_l = r3(lat_p["mqk"], tl), r3(lat_p["mv"], tl), r3(lat_p["mo"], tl)
    G = M_HEADS // _HB
    width = _HB * LANES
    head_blk = lambda t, off: pl.BlockSpec((1, t, width), lambda b, g: (b, 0, g + off))
    gate_blk = lambda g: pl.BlockSpec((1,) + g.shape[1:], lambda b, h: (b, 0, 0, 0))
    in_specs = [head_blk(tc, 0), head_blk(tc, G), head_blk(tc, 0), head_blk(tc, 0), gate_blk(gates_c),
                head_blk(tl, 0), head_blk(tl, G), head_blk(tl, 0), head_blk(tl, 0), gate_blk(gates_l),
                pl.BlockSpec((3, width), lambda b, g: (0, g)),
                pl.BlockSpec((3, width), lambda b, g: (0, g + G)),
                pl.BlockSpec((1, width), lambda b, g: (0, g))]
    out_shape = [jax.ShapeDtypeStruct((B, tl, BRANCH), BF16)]
    out_specs = [head_blk(tl, 0)]
    if need_ctx:
        out_shape.append(jax.ShapeDtypeStruct((B, tc, BRANCH), BF16))
        out_specs.append(head_blk(tc, 0))
    tot = tc + tl
    scratch = [pltpu.VMEM((_HB, max(tc, tl) + 16, LANES), F32),
               pltpu.VMEM((tot, width), BF16), pltpu.VMEM((tot, width), BF16),
               pltpu.VMEM((tot, width), F32), pltpu.VMEM((tot, width), F32),
               pltpu.VMEM((2 * _HB, 2 * LANES, LANES), F32), pltpu.VMEM((8, LANES), F32)]
    outs = pl.pallas_call(
        functools.partial(_mlstm_kernel, tc=tc, tl=tl, need_ctx=need_ctx),
        grid=(B, G),
        in_specs=in_specs,
        out_specs=out_specs,
        out_shape=out_shape,
        scratch_shapes=scratch,
        compiler_params=_params(("arbitrary", "arbitrary")),
        name="mlstm",
    )(qk_c, qk_c, v_c, o_c, gates_c, qk_l, qk_l, v_l, o_l, gates_l,
      conv_w, conv_w, m_norm.reshape(1, -1))
    ya = outs[0].reshape(B * tl, BRANCH)
    ya_c = outs[1].reshape(B * tc, BRANCH) if need_ctx else None
    return ya, ya_c


_WB = 128


def _win_kernel(sink_ref, q_ref, *refs, has_latent, qb, seq):
    if has_latent:
        kl_ref, vl_ref, kx_ref, vx_ref, o_ref = refs
    else:
        kx_ref, vx_ref, o_ref = refs
    n = pl.program_id(1)
    rep = W_HEADS // W_KV_HEADS
    rows = rep * _WB
    span = 3 * _WB
    lane = lax.broadcasted_iota(jnp.int32, (_WB, LANES), 1)
    left = lane < W_DIM
    zero = jnp.zeros((_WB, LANES), BF16)
    if has_latent:
        qq = jnp.bitwise_and(lax.broadcasted_iota(jnp.int32, (rows, span), 0), _WB - 1)
        dist = lax.broadcasted_iota(jnp.int32, (rows, span), 1) - qq
    windows = {}

    def window_of(j):
        if j not in windows:
            q_start = (n * qb + j) * _WB
            k_start = pl.multiple_of(jnp.clip(q_start - _WB, 0, seq - span), _WB)
            off = pltpu.bitcast(dist + (k_start - q_start + _WB), jnp.uint32)
            windows[j] = (k_start, off <= jnp.uint32(2 * _WB))
        return windows[j]

    def logits_of(j, g):
        r0 = j * _WB
        pieces = []
        for t in range(rep // 2):
            qt = q_ref[0, r0:r0 + _WB, (g * rep // 2 + t) * LANES:(g * rep // 2 + t + 1) * LANES]
            pieces += [jnp.where(left, qt, zero), jnp.where(left, zero, qt)]
        lhs = jnp.concatenate(pieces, axis=0)
        gsl = slice(g * LANES, (g + 1) * LANES)
        logits = [_dot_nt(lhs, kx_ref[0, :, gsl])]
        vals = [vx_ref[0, :, gsl]]
        if has_latent:
            k_start, near = window_of(j)
            s_win = _dot_nt(lhs, kl_ref[0, pl.ds(k_start, span), gsl])
            logits.append(jnp.where(near, s_win, -jnp.inf))
            vals.append(vl_ref[0, pl.ds(k_start, span), gsl])
        return logits, vals

    def finish(j, g, logits, vals):
        r0 = j * _WB
        sink_rep = jnp.concatenate(
            [jnp.full((_WB, LANES), sink_ref[g * rep + r] * LOG2E, F32) for r in range(rep)], axis=0)
        m_part = None
        for s in logits:
            for t in range(s.shape[1] // LANES):
                st = s[:, t * LANES:(t + 1) * LANES]
                m_part = st if m_part is None else jnp.maximum(m_part, st)
        m = jnp.maximum(sink_rep, jnp.max(m_part, axis=-1, keepdims=True))
        acc = None
        for s, v in zip(logits, vals):
            p = jnp.concatenate([jnp.exp2(s[:, t * LANES:(t + 1) * LANES] - m)
                                 for t in range(s.shape[1] // LANES)], axis=1)
            part = _dot(p.astype(BF16), jnp.concatenate([v, jnp.ones(v.shape, BF16)], axis=1))
            acc = part if acc is None else acc + part
        o = acc[:, 0:LANES] / (acc[:, LANES:2 * LANES] + jnp.exp2(sink_rep - m))
        for t in range(rep // 2):
            tile = jnp.where(left, o[(2 * t) * _WB:(2 * t + 1) * _WB], o[(2 * t + 1) * _WB:(2 * t + 2) * _WB])
            c0 = (g * rep // 2 + t) * LANES
            o_ref[0, r0:r0 + _WB, c0:c0 + LANES] = tile.astype(BF16)

    streams = [(j, g) for j in range(qb) for g in range(W_KV_HEADS)]
    ahead = 3
    pending = []
    for i, (j, g) in enumerate(streams):
        pending.append((j, g) + logits_of(j, g))
        if i >= ahead:
            finish(*pending.pop(0))
    for item in pending:
        finish(*item)


def _window_attention(q, kx, vx, sink, B, tq, tcx, lat_kv=None, qb=4):
    q3 = q.reshape(B, tq, BRANCH)
    kx3, vx3 = kx.reshape(B, tcx, 2 * LANES), vx.reshape(B, tcx, 2 * LANES)
    nb = tq // _WB
    qb = math.gcd(qb, nb)
    has_latent = lat_kv is not None
    whole = lambda t: pl.BlockSpec((1, t, 2 * LANES), lambda b, n: (b, 0, 0))
    in_specs = [pl.BlockSpec(memory_space=pltpu.SMEM),
                pl.BlockSpec((1, qb * _WB, BRANCH), lambda b, n: (b, n, 0))]
    args = [sink.astype(F32), q3]
    if has_latent:
        assert tq >= 3 * _WB
        in_specs += [whole(tq), whole(tq)]
        args += [a.reshape(B, tq, 2 * LANES) for a in lat_kv]
    in_specs += [whole(tcx), whole(tcx)]
    args += [kx3, vx3]
    out = pl.pallas_call(
        functools.partial(_win_kernel, has_latent=has_latent, qb=qb, seq=tq),
        grid=(B, nb // qb),
        in_specs=in_specs,
        out_specs=pl.BlockSpec((1, qb * _WB, BRANCH), lambda b, n: (b, n, 0)),
        out_shape=jax.ShapeDtypeStruct((B, tq, BRANCH), BF16),
        compiler_params=_params(("arbitrary", "arbitrary")),
        name="win_lat" if has_latent else "win_ctx",
    )(*args)
    return out.reshape(B * tq, BRANCH)


def _diff_kernel(lam_ref, q_ref, *refs, tq, nsub, tk, tcx, tlat, lam_init):
    if tlat:
        kx_ref, vx_ref, kl_ref, vl_ref, nrm_ref, o_ref = refs
    else:
        kx_ref, vx_ref, nrm_ref, o_ref = refs
        kl_ref = vl_ref = None
    sub = tq // nsub
    lane = lax.broadcasted_iota(jnp.int32, (sub, LANES), 1)
    left = lane < LANES // 2
    zero = jnp.zeros((sub, LANES), BF16)
    tiles = [(kx_ref, vx_ref, 0, tcx)] + [(kl_ref, vl_ref, j * tk, tk) for j in range(tlat // tk)]
    ones = jnp.ones((max(tk, tcx), LANES), BF16)

    lhs = []
    for u in range(nsub):
        q = q_ref[0, u * sub:(u + 1) * sub, :]
        lhs += [jnp.where(left, q, zero), jnp.where(left, zero, q)]
    state = [None] * (2 * nsub)
    for k_ref, v_ref, r0, n in tiles:
        vx = jnp.concatenate([v_ref[0, r0:r0 + n, :], ones[0:n]], axis=1)
        k = k_ref[0, r0:r0 + n, :]
        for i in range(2 * nsub):
            s = _dot_nt(lhs[i], k)
            m_part = s[:, 0:LANES]
            for t in range(1, n // LANES):
                m_part = jnp.maximum(m_part, s[:, t * LANES:(t + 1) * LANES])
            m_j = jnp.max(m_part, axis=-1, keepdims=True)
            part = _dot(jnp.exp2(s - m_j).astype(BF16), vx)
            if state[i] is None:
                state[i] = (m_j, part)
            else:
                m_run, acc = state[i]
                m_new = jnp.maximum(m_run, m_j)
                state[i] = (m_new, acc * jnp.exp2(m_run - m_new) + part * jnp.exp2(m_j - m_new))
    outs = [acc[:, 0:LANES] / acc[:, LANES:2 * LANES] for _, acc in state]
    lv = lam_ref[...]
    s1 = jnp.sum(lv[0:1] * lv[1:2], axis=-1, keepdims=True)
    s2 = jnp.sum(lv[2:3] * lv[3:4], axis=-1, keepdims=True)
    lam = jnp.exp(s1) - jnp.exp(s2) + lam_init
    for u in range(nsub):
        y = outs[2 * u] - lam * outs[2 * u + 1]
        ms = jnp.mean(y * y, axis=-1, keepdims=True)
        o_ref[0, u * sub:(u + 1) * sub, :] = (y * lax.rsqrt(ms + EPS) * nrm_ref[...] * (1.0 - lam_init)).astype(BF16)


def _diff_attention(q, kx, vx, lamv, d_norm, lam_init, B, tq_total, tcx, lat_kv=None, sub=1024, nsub=4, tk=1024):
    q3 = q.reshape(B, tq_total, BRANCH)
    kx3, vx3 = kx.reshape(B, tcx, BRANCH), vx.reshape(B, tcx, BRANCH)
    sub = min(sub, tq_total)
    nsub = math.gcd(nsub, tq_total // sub)
    tq = sub * nsub
    head_blk = lambda t, f: pl.BlockSpec((1, t, LANES), f)
    whole = lambda b, h, i: (b, 0, h)
    in_specs = [pl.BlockSpec((8, LANES), lambda b, h, i: (0, 0)),
                head_blk(tq, lambda b, h, i: (b, i, h)),
                head_blk(tcx, whole), head_blk(tcx, whole)]
    args = [lamv, q3, kx3, vx3]
    tlat = 0
    if lat_kv is not None:
        tlat = tq_total
        k3, v3 = (a.reshape(B, tlat, BRANCH) for a in lat_kv)
        in_specs += [head_blk(tlat, whole), head_blk(tlat, whole)]
        args += [k3, v3]
    in_specs.append(pl.BlockSpec((1, LANES), lambda b, h, i: (0, h)))
    args.append(d_norm.reshape(1, -1))
    tk = min(tk, max(tlat, LANES))
    out = pl.pallas_call(
        functools.partial(_diff_kernel, tq=tq, nsub=nsub, tk=tk, tcx=tcx, tlat=tlat, lam_init=lam_init),
        grid=(B, DF_HEADS, tq_total // tq),
        in_specs=in_specs,
        out_specs=head_blk(tq, lambda b, h, i: (b, i, h)),
        out_shape=jax.ShapeDtypeStruct((B, tq_total, BRANCH), BF16),
        compiler_params=_params(("arbitrary", "arbitrary", "arbitrary")),
        name="diff_lat" if tlat else "diff_ctx",
    )(*args)
    return out.reshape(B * tq_total, BRANCH)


def _merge_kernel(ya_ref, yb_ref, yc_ref, gate_ref, x_ref, mod_ref, wb_ref, wo_ref, o_ref, *, d):
    merged = None
    for i, y_ref in enumerate((ya_ref, yb_ref, yc_ref)):
        gate = jax.nn.sigmoid(gate_ref[:, i * d:(i + 1) * d].astype(F32))
        term = gate * _dot(y_ref[...], wb_ref[i])
        merged = term if merged is None else merged + term
    out = _dot(merged.astype(BF16), wo_ref[...])
    gt = mod_ref[0][:, 2 * d:3 * d]
    o_ref[...] = x_ref[...] + gt * out


def _merge(ya, yb, yc, gate, x2, mod, wb, wo, *, tm, rows_per_mod):
    N, D = x2.shape
    row = lambda w: pl.BlockSpec((tm, w), lambda i: (i, 0))
    return pl.pallas_call(
        functools.partial(_merge_kernel, d=D),
        grid=(N // tm,),
        in_specs=[row(BRANCH), row(BRANCH), row(BRANCH), row(N_BRANCH * D), row(D),
                  pl.BlockSpec((1, 1, 6 * D), lambda i: ((i * tm) // rows_per_mod, 0, 0)),
                  _resident((N_BRANCH, BRANCH, D), lambda i: (0, 0, 0)),
                  _resident((D, D), lambda i: (0, 0))],
        out_specs=row(D),
        out_shape=jax.ShapeDtypeStruct((N, D), F32),
        compiler_params=_params(("arbitrary",)),
        name="merge",
    )(ya, yb, yc, gate, x2, mod, wb, wo)


_FFN_CHUNK = 256


def _ffn_kernel(x_ref, mod_ref, g_ref, wi_ref, wo_ref, gf_ref, o_ref, *, d, hidden, final_norm):
    x = x_ref[...]
    mod = mod_ref[0]
    h = _modnorm(x, g_ref[...], mod[:, 3 * d:4 * d], mod[:, 4 * d:5 * d]).astype(BF16)
    acc = None
    for c0 in range(0, hidden, _FFN_CHUNK):
        gate = _dot(h, wi_ref[:, c0:c0 + _FFN_CHUNK])
        up = _dot(h, wi_ref[:, hidden + c0:hidden + c0 + _FFN_CHUNK])
        a = (gate * jax.nn.sigmoid(gate) * up).astype(BF16)
        part = _dot(a, wo_ref[c0:c0 + _FFN_CHUNK, :])
        acc = part if acc is None else acc + part
    y = x + mod[:, 5 * d:6 * d] * acc
    if final_norm:
        ms = jnp.mean(y * y, axis=-1, keepdims=True)
        y = y * lax.rsqrt(ms + EPS) * gf_ref[...]
    o_ref[...] = y


def _ffn(x2, mod, g, wi, wo, g_final, *, tm, rows_per_mod, final_norm):
    N, D = x2.shape
    hidden = wo.shape[0]
    return pl.pallas_call(
        functools.partial(_ffn_kernel, d=D, hidden=hidden, final_norm=final_norm),
        grid=(N // tm,),
        in_specs=[pl.BlockSpec((tm, D), lambda i: (i, 0)),
                  pl.BlockSpec((1, 1, 6 * D), lambda i: ((i * tm) // rows_per_mod, 0, 0)),
                  pl.BlockSpec((1, D), lambda i: (0, 0)),
                  _resident((D, 2 * hidden), lambda i: (0, 0)),
                  _resident((hidden, D), lambda i: (0, 0)),
                  pl.BlockSpec((1, D), lambda i: (0, 0))],
        out_specs=pl.BlockSpec((tm, D), lambda i: (i, 0)),
        out_shape=jax.ShapeDtypeStruct((N, D), F32),
        compiler_params=_params(("arbitrary",)),
        name="ffn",
    )(x2, mod, g, wi, wo, g_final)


def _row_tile(n, want=512):
    t = want
    while n % t:
        t //= 2
    return t


def kernel(x, c, ctx, c_ctx, w_mod, b_mod, g_mix, g_ffn, w_in, b_gate, conv_w, m_norm, sink,
           lam_q1, lam_k1, lam_q2, lam_k2, d_norm, w_branch, w_out, w_ffn_in, w_ffn_out, g_final):
    B, T, D = x.shape
    Tc = ctx.shape[1]
    depth = w_mod.shape[0]
    assert T % CHUNK == 0 and Tc % CHUNK == 0 and T % GRID_W == 0

    n_rows = -(-(B + 1) // 16) * 16
    cc = jnp.concatenate([c, c_ctx[None, :], jnp.zeros((n_rows - B - 1, D), F32)], axis=0)
    mods = _adaln(cc, w_mod, b_mod)

    rope = _rope_tables(T)
    no_rope = tuple(jnp.zeros((Tc, LANES), F32) for _ in range(3))
    tm = _row_tile(T)
    tmc = _row_tile(Tc)
    names = [s[0] for s in _SEGS]

    xl = x.reshape(B * T, D)
    xc = ctx.reshape(B * Tc, D)
    for l in range(depth):
        need_ctx = l < depth - 1
        lam_init = 0.8 - 0.6 * math.exp(-0.3 * l)
        mod_l = mods[l, :B].reshape(B, 1, 6 * D)
        mod_c = mods[l, B:B + 1].reshape(1, 1, 6 * D)
        w_ext = _extend_w_in(w_in[l])
        g_mix_l = g_mix[l].reshape(1, D)
        g_ffn_l = g_ffn[l].reshape(1, D)
        lamv = jnp.zeros((8, LANES), F32)
        for i, v in enumerate((lam_q1[l], lam_k1[l], lam_q2[l], lam_k2[l])):
            lamv = lamv.at[i, :v.shape[0]].set(v.astype(F32))

        pl_ = dict(zip(names, _inproj(xl, mod_l, g_mix_l, rope, w_ext, tm=tm, rows_per_mod=T, seq=T, use_rope=True)))
        pc_ = dict(zip(names, _inproj(xc, mod_c, g_mix_l, no_rope, w_ext, tm=tmc, rows_per_mod=B * Tc, seq=Tc,
                                      use_rope=False)))

        gates_l = _gates(pl_["mg"], b_gate[l], B, T)
        gates_c = _gates(pc_["mg"], b_gate[l], B, Tc)
        ya, ya_c = _mlstm(pc_, pl_, gates_c, gates_l, conv_w[l], m_norm[l], B, Tc, T, need_ctx)
        yb = _window_attention(pl_["wq"], pc_["wk"], pc_["wv"], sink[l], B, T, Tc, lat_kv=(pl_["wk"], pl_["wv"]))
        yc = _diff_attention(pl_["dq"], pc_["dk"], pc_["dv"], lamv, d_norm[l], lam_init, B, T, Tc,
                             lat_kv=(pl_["dk"], pl_["dv"]))
        wb = w_branch[l].astype(BF16)
        wo = w_out[l].astype(BF16)
        wfi = w_ffn_in[l].astype(BF16)
        wfo = w_ffn_out[l].astype(BF16)
        gfin = g_final.reshape(1, D)
        last = l == depth - 1
        xl = _merge(ya, yb, yc, pl_["gate"], xl, mod_l, wb, wo, tm=tm, rows_per_mod=T)
        xl = _ffn(xl, mod_l, g_ffn_l, wfi, wfo, gfin, tm=tm, rows_per_mod=T, final_norm=last)
        if need_ctx:
            yb_c = _window_attention(pc_["wq"], pc_["wk"], pc_["wv"], sink[l], B, Tc, Tc)
            yc_c = _diff_attention(pc_["dq"], pc_["dk"], pc_["dv"], lamv, d_norm[l], lam_init, B, Tc, Tc)
            xc = _merge(ya_c, yb_c, yc_c, pc_["gate"], xc, mod_c, wb, wo, tm=tmc, rows_per_mod=B * Tc)
            xc = _ffn(xc, mod_c, g_ffn_l, wfi, wfo, gfin, tm=tmc, rows_per_mod=B * Tc, final_norm=False)
    return xl.reshape(B, T, D)
```

```python
import functools
import math

import jax
import jax.numpy as jnp
import numpy as np
from jax import lax
from jax.experimental import pallas as pl
from jax.experimental.pallas import tpu as pltpu

F32 = jnp.float32
BF16 = jnp.bfloat16

GRID_W = 64
BRANCH = 512
N_BRANCH = 3
M_HEADS = 4
M_DIM = 128
W_HEADS = 8
W_KV_HEADS = 2
W_DIM = 64
DF_HEADS = 4
ROPE_DIM = 64
ROPE_BASE = 10000.0
EPS = 1e-6
LANES = 128
CHUNK = 128
VMEM_LIMIT = 56 * 1024 * 1024

_NT = (((1,), (1,)), ((), ()))
LOG2E = math.log2(math.e)
_Q_SCALE = W_DIM ** -0.5 * LOG2E


def _dot(a, b):
    return jnp.dot(a, b, preferred_element_type=F32)


def _dot_nt(a, b):
    return lax.dot_general(a, b, _NT, preferred_element_type=F32)


def _params(sem):
    return pltpu.CompilerParams(dimension_semantics=sem, vmem_limit_bytes=VMEM_LIMIT)


def _resident(shape, index_map):
    return pl.BlockSpec(shape, index_map, pipeline_mode=pl.Buffered(1))


def _adaln_kernel(c_ref, w_ref, b_ref, o_ref):
    c = c_ref[...]
    s = (c * jax.nn.sigmoid(c)).astype(BF16)
    o_ref[0] = _dot(s, w_ref[0].astype(BF16)) + b_ref[0]


def _adaln(cc, w_mod, b_mod):
    L, D, N = w_mod.shape
    R = cc.shape[0]
    tn = 1536
    return pl.pallas_call(
        _adaln_kernel,
        grid=(L, N // tn),
        in_specs=[pl.BlockSpec((R, D), lambda l, j: (0, 0)),
                  pl.BlockSpec((1, D, tn), lambda l, j: (l, 0, j)),
                  pl.BlockSpec((1, 1, tn), lambda l, j: (l, 0, j))],
        out_specs=pl.BlockSpec((1, R, tn), lambda l, j: (l, 0, j)),
        out_shape=jax.ShapeDtypeStruct((L, R, N), F32),
        compiler_params=_params(("arbitrary", "arbitrary")),
        name="adaln",
    )(cc, w_mod, b_mod.reshape(L, 1, N))


def _modnorm(x, g, shift, scale):
    ms = jnp.mean(x * x, axis=-1, keepdims=True)
    return x * lax.rsqrt(ms + EPS) * (g * (1.0 + scale)) + shift


_SEGS = (("mqk", 1024, BF16, "plain"), ("mv", 512, BF16, "plain"), ("mo", 512, BF16, "plain"),
         ("mg", 128, F32, "plain"), ("wq", 512, BF16, "ropeq"), ("wk", 256, BF16, "rope"),
         ("wv", 256, BF16, "plain"), ("dq", 512, BF16, "ropeq"), ("dk", 512, BF16, "rope"),
         ("dv", 512, BF16, "plain"), ("gate", 3072, BF16, "plain"))
_EXT_WIDTH = sum(s[1] for s in _SEGS)
_COL_CHUNK = 512


def _inproj_kernel(x_ref, mod_ref, g_ref, cos_ref, sa_ref, sb_ref, w_ref, *out_refs, d, use_rope):
    x = x_ref[...]
    mod = mod_ref[0]
    h = _modnorm(x, g_ref[...], mod[:, 0:d], mod[:, d:2 * d]).astype(BF16)
    if use_rope:
        cos, sa, sb = cos_ref[...], sa_ref[...], sb_ref[...]
    off = 0
    for (name, width, dt, kind), o_ref in zip(_SEGS, out_refs):
        for c0 in range(0, width, _COL_CHUNK):
            cw = min(_COL_CHUNK, width - c0)
            acc = _dot(h, w_ref[:, off + c0:off + c0 + cw])
            if kind != "plain":
                tiles = []
                for t in range(cw // LANES):
                    a = acc[:, t * LANES:(t + 1) * LANES]
                    if use_rope:
                        a = (a * cos + pltpu.roll(a, LANES - 16, 1) * sa + pltpu.roll(a, 16, 1) * sb)
                    if kind == "ropeq":
                        a = a * _Q_SCALE
                    tiles.append(a)
                acc = jnp.concatenate(tiles, axis=1) if len(tiles) > 1 else tiles[0]
            o_ref[:, c0:c0 + cw] = acc.astype(dt)
        off += width


def _inproj(x2, mod, g, rope, w_ext, *, tm, rows_per_mod, seq, use_rope):
    N, D = x2.shape
    nt = seq // tm
    kern = functools.partial(_inproj_kernel, d=D, use_rope=use_rope)
    rope_spec = pl.BlockSpec((tm, LANES), lambda i: (i % nt, 0))
    out_shape = [jax.ShapeDtypeStruct((N, w), dt) for _, w, dt, _ in _SEGS]
    out_specs = [pl.BlockSpec((tm, w), lambda i: (i, 0)) for _, w, _, _ in _SEGS]
    return pl.pallas_call(
        kern,
        grid=(N // tm,),
        in_specs=[pl.BlockSpec((tm, D), lambda i: (i, 0)),
                  pl.BlockSpec((1, 1, 6 * D), lambda i: ((i * tm) // rows_per_mod, 0, 0)),
                  pl.BlockSpec((1, D), lambda i: (0, 0)),
                  rope_spec, rope_spec, rope_spec,
                  _resident((D, _EXT_WIDTH), lambda i: (0, 0))],
        out_specs=out_specs,
        out_shape=out_shape,
        compiler_params=_params(("arbitrary",)),
        name="inproj_lat" if use_rope else "inproj_ctx",
    )(x2, mod, g, *rope, w_ext)


def _extend_w_in(w):
    D = w.shape[0]
    o = np.cumsum([0, 512, 512, 512, 512, 16, 512, 128, 128, 512, 512, 512, 3072])
    p = [w[:, o[i]:o[i + 1]] for i in range(12)]
    dup = lambda a: jnp.concatenate([a[:, 0:64], a[:, 0:64], a[:, 64:128], a[:, 64:128]], axis=1)
    mg = jnp.concatenate([p[4], jnp.zeros((D, LANES - 16), w.dtype)], axis=1)
    ext = jnp.concatenate([p[0], p[1], p[2], p[3], mg, p[5], dup(p[6]), dup(p[7]), p[8], p[9], p[10], p[11]], axis=1)
    return ext.astype(BF16)


def _rope_tables(n_tokens):
    rows = n_tokens // GRID_W
    r, col = jnp.meshgrid(jnp.arange(rows), jnp.arange(GRID_W), indexing="ij")
    half = ROPE_DIM // 2
    inv = ROPE_BASE ** (-jnp.arange(0, half, 2, dtype=F32) / half)
    ang_r = r.reshape(-1, 1).astype(F32) * inv
    ang_c = col.reshape(-1, 1).astype(F32) * inv
    ang = jnp.concatenate([ang_r, ang_r, ang_c, ang_c], axis=-1)
    cos, sin = jnp.cos(ang), jnp.sin(ang)
    cos2 = jnp.concatenate([cos, cos], axis=-1)
    sin2 = jnp.concatenate([sin, sin], axis=-1)
    first = (jnp.arange(LANES) % 32) < 16
    sin_a = jnp.where(first, -sin2, 0.0)
    sin_b = jnp.where(first, 0.0, sin2)
    return cos2, sin_a, sin_b


def _scan_lanes(x, op, fill, reverse):
    lane = lax.broadcasted_iota(jnp.int32, x.shape, 1)
    d = 1
    while d < LANES:
        if reverse:
            shifted = jnp.where(lane < LANES - d, pltpu.roll(x, LANES - d, 1), fill)
        else:
            shifted = jnp.where(lane >= d, pltpu.roll(x, d, 1), fill)
        x = op(x, shifted)
        d *= 2
    return x


def _gates_kernel(x_ref, bias_ref, o_ref, *, rows):
    x = x_ref[0] + bias_ref[...]
    lane = lax.broadcasted_iota(jnp.int32, (rows, LANES), 1)
    for d, reverse in enumerate((False, True)):
        i_pre = x[(2 * d) * rows:(2 * d + 1) * rows]
        f_pre = x[(2 * d + 1) * rows:(2 * d + 2) * rows]
        log_f = jax.nn.log_sigmoid(f_pre)
        b = _scan_lanes(log_f, jnp.add, 0.0, reverse)
        a = i_pre - b
        pm = _scan_lanes(a, jnp.maximum, -jnp.inf, reverse)
        last = 0 if reverse else LANES - 1
        b_end = jnp.sum(jnp.where(lane == last, b, 0.0), axis=1, keepdims=True)
        a_max = jnp.max(a, axis=1, keepdims=True)
        o_ref[0, 5 * d + 0] = b
        o_ref[0, 5 * d + 1] = a
        o_ref[0, 5 * d + 2] = pm
        o_ref[0, 5 * d + 3] = jnp.broadcast_to(b_end, (rows, LANES))
        o_ref[0, 5 * d + 4] = jnp.broadcast_to(a_max, (rows, LANES))


def _gates(mg, b_gate, B, T):
    nc = T // CHUNK
    rows = M_HEADS * nc
    g = mg[:, :4 * M_HEADS].reshape(B, T, 4 * M_HEADS)
    gt = jnp.transpose(g, (0, 2, 1)).reshape(B, 4 * rows, LANES)
    bias = jnp.repeat(b_gate.astype(F32), nc).reshape(4 * rows, 1)
    return pl.pallas_call(
        functools.partial(_gates_kernel, rows=rows),
        grid=(B,),
        in_specs=[pl.BlockSpec((1, 4 * rows, LANES), lambda b: (b, 0, 0)),
                  pl.BlockSpec((4 * rows, 1), lambda b: (0, 0))],
        out_specs=pl.BlockSpec((1, 10, rows, LANES), lambda b: (b, 0, 0, 0)),
        out_shape=jax.ShapeDtypeStruct((B, 10, rows, LANES), F32),
        compiler_params=_params(("arbitrary",)),
        name="mlstm_gates",
    )(gt, bias)


_CONV_TILE = 256
_HB = 2


def _mlstm_kernel(qc_ref, kc_ref, vc_ref, oc_ref, pc_ref, ql_ref, kl_ref, vl_ref, ol_ref, pl_ref,
                  cwq_ref, cwk_ref, nrm_ref, *rest, tc, tl, need_ctx):
    if need_ctx:
        yl_ref, yc_ref = rest[0], rest[1]
        scr = rest[2:]
    else:
        yl_ref, yc_ref = rest[0], None
        scr = rest[1:]
    xpad, qs, ks, hf, hb, cx, mst = scr
    head0 = pl.program_id(1) * _HB

    def conv_stream(u_ref, w_ref, dst, dst_off, ts, scale):
        for hh in range(_HB):
            xpad[hh, 0:8, :] = jnp.zeros((8, LANES), F32)
            xpad[hh, 8:8 + ts, :] = u_ref[0, :, hh * LANES:(hh + 1) * LANES].astype(F32)
            xpad[hh, 8 + ts:16 + ts, :] = jnp.zeros((8, LANES), F32)
        w = w_ref[...]
        w0, w1, w2 = w[0:1], w[1:2], w[2:3]
        tile = min(_CONV_TILE, ts)

        def body(i, _):
            r0 = pl.multiple_of(i * tile, 8)
            for hh in range(_HB):
                lanes = slice(hh * LANES, (hh + 1) * LANES)
                prev = xpad[hh, pl.ds(r0 + 7, tile), :]
                cur = xpad[hh, pl.ds(r0 + 8, tile), :]
                nxt = xpad[hh, pl.ds(r0 + 9, tile), :]
                y = prev * w0[:, lanes] + cur * w1[:, lanes] + nxt * w2[:, lanes]
                y = y * jax.nn.sigmoid(y) * scale
                dst[pl.ds(pl.multiple_of(dst_off + r0, 16), tile), lanes] = y.astype(BF16)
            return 0

        lax.fori_loop(0, ts // tile, body, 0)

    kscale = M_DIM ** -0.5
    conv_stream(qc_ref, cwq_ref, qs, 0, tc, 1.0)
    conv_stream(kc_ref, cwk_ref, ks, 0, tc, kscale)
    conv_stream(ql_ref, cwq_ref, qs, tc, tl, 1.0)
    conv_stream(kl_ref, cwk_ref, ks, tc, tl, kscale)

    cx[...] = jnp.zeros(cx.shape, F32)
    mst[...] = jnp.zeros(mst.shape, F32)

    row_i = lax.broadcasted_iota(jnp.int32, (CHUNK, CHUNK), 0)
    col_i = lax.broadcasted_iota(jnp.int32, (CHUNK, CHUNK), 1)
    masks = (col_i <= row_i, col_i >= row_i)
    ones_b = jnp.ones((CHUNK, LANES), BF16)
    ones_f = jnp.ones((LANES, CHUNK), F32)

    def colify(row):
        return jnp.broadcast_to(row, (CHUNK, CHUNK)).T

    def chunk_step(hh, d, c, p_ref, v_ref, nc, base, want_h):
        r = (head0 + hh) * nc + c
        lanes = slice(hh * LANES, (hh + 1) * LANES)
        st = 2 * hh + d
        plane = lambda j: p_ref[0, 5 * d + j, pl.ds(r, 1), :]
        b_row, a_row, pm_row, bend_row, amax_row = (plane(j) for j in range(5))
        m_prev = mst[st:st + 1, :]
        m_end = jnp.maximum(m_prev, amax_row)
        w_end = jnp.exp(a_row - m_end)
        decay = jnp.exp(m_prev - m_end)
        row0 = pl.multiple_of(base + c * CHUNK, CHUNK)
        q = qs[pl.ds(row0, CHUNK), lanes]
        k = ks[pl.ds(row0, CHUNK), lanes]
        v = v_ref[0, pl.ds(pl.multiple_of(c * CHUNK, CHUNK), CHUNK), lanes]
        c_old = cx[st]
        if want_h:
            m_col = jnp.maximum(m_prev, colify(pm_row))
            w_intra = jnp.exp(jnp.where(masks[d], a_row - m_col, -jnp.inf))
            w_inter = jnp.exp(m_prev - m_col)
            floor = jnp.exp(-(colify(b_row) + m_col))
            s = _dot_nt(q, k) * w_intra
            intra = _dot(s.astype(BF16), jnp.concatenate([v, ones_b], axis=1))
            inter = _dot_nt(q, c_old.astype(BF16))
            num = intra[:, 0:M_DIM] + w_inter * inter[:, 0:M_DIM]
            den = intra[:, M_DIM:2 * M_DIM] + w_inter * inter[:, M_DIM:2 * M_DIM]
            hval = num / jnp.maximum(jnp.abs(den), floor)
            dst = hf if d == 0 else hb
            dst[pl.ds(row0, CHUNK), lanes] = hval
        vxt = jnp.concatenate([v.astype(F32).T, ones_f], axis=0)
        cx[st] = decay * c_old + _dot((vxt * w_end).astype(BF16), k)
        mst[st:st + 1, :] = bend_row + m_end

    ncc, ncl = tc // CHUNK, tl // CHUNK
    for c in range(ncc):
        for hh in range(_HB):
            chunk_step(hh, 0, c, pc_ref, vc_ref, ncc, 0, need_ctx)
            chunk_step(hh, 1, ncc - 1 - c, pc_ref, vc_ref, ncc, 0, need_ctx)

    def lat_body(c, _):
        for hh in range(_HB):
            chunk_step(hh, 0, c, pl_ref, vl_ref, ncl, tc, True)
            chunk_step(hh, 1, ncl - 1 - c, pl_ref, vl_ref, ncl, tc, True)
        return 0

    lax.fori_loop(0, ncl, lat_body, 0, unroll=8)

    nrm = nrm_ref[...]

    def finish(o_ref, y_ref, base, ts):
        def body(i, _):
            r0 = pl.multiple_of(i * CHUNK, CHUNK)
            hsum = hf[pl.ds(base + r0, CHUNK), :] + hb[pl.ds(base + r0, CHUNK), :]
            y = hsum * jax.nn.sigmoid(o_ref[0, pl.ds(r0, CHUNK), :].astype(F32))
            for hh in range(_HB):
                lanes = slice(hh * LANES, (hh + 1) * LANES)
                yh = y[:, lanes]
                ms = jnp.mean(yh * yh, axis=-1, keepdims=True)
                y_ref[0, pl.ds(r0, CHUNK), lanes] = (yh * lax.rsqrt(ms + EPS) * nrm[:, lanes]).astype(BF16)
            return 0

        lax.fori_loop(0, ts // CHUNK, body, 0, unroll=min(4, ts // CHUNK))

    finish(ol_ref, yl_ref, tc, tl)
    if need_ctx:
        finish(oc_ref, yc_ref, 0, tc)


def _mlstm(ctx_p, lat_p, gates_c, gates_l, conv_w, m_norm, B, tc, tl, need_ctx):
    r3 = lambda a, t: a.reshape(B, t, a.shape[-1])
    qk_c, v_c, o_c = r3(ctx_p["mqk"], tc), r3(ctx_p["mv"], tc), r3(ctx_p["mo"], tc)
    qk_l, v_l, o_l = r3(lat_p["mqk"], tl), r3(lat_p["mv"], tl), r3(lat_p["mo"], tl)
    G = M_HEADS // _HB
    width = _HB * LANES
    head_blk = lambda t, off: pl.BlockSpec((1, t, width), lambda b, g: (b, 0, g + off))
    gate_blk = lambda g: pl.BlockSpec((1,) + g.shape[1:], lambda b, h: (b, 0, 0, 0))
    in_specs = [head_blk(tc, 0), head_blk(tc, G), head_blk(tc, 0), head_blk(tc, 0), gate_blk(gates_c),
                head_blk(tl, 0), head_blk(tl, G), head_blk(tl, 0), head_blk(tl, 0), gate_blk(gates_l),
                pl.BlockSpec((3, width), lambda b, g: (0, g)),
                pl.BlockSpec((3, width), lambda b, g: (0, g + G)),
                pl.BlockSpec((1, width), lambda b, g: (0, g))]
    out_shape = [jax.ShapeDtypeStruct((B, tl, BRANCH), BF16)]
    out_specs = [head_blk(tl, 0)]
    if need_ctx:
        out_shape.append(jax.ShapeDtypeStruct((B, tc, BRANCH), BF16))
        out_specs.append(head_blk(tc, 0))
    tot = tc + tl
    scratch = [pltpu.VMEM((_HB, max(tc, tl) + 16, LANES), F32),
               pltpu.VMEM((tot, width), BF16), pltpu.VMEM((tot, width), BF16),
               pltpu.VMEM((tot, width), F32), pltpu.VMEM((tot, width), F32),
               pltpu.VMEM((2 * _HB, 2 * LANES, LANES), F32), pltpu.VMEM((8, LANES), F32)]
    outs = pl.pallas_call(
        functools.partial(_mlstm_kernel, tc=tc, tl=tl, need_ctx=need_ctx),
        grid=(B, G),
        in_specs=in_specs,
        out_specs=out_specs,
        out_shape=out_shape,
        scratch_shapes=scratch,
        compiler_params=_params(("arbitrary", "arbitrary")),
        name="mlstm",
    )(qk_c, qk_c, v_c, o_c, gates_c, qk_l, qk_l, v_l, o_l, gates_l,
      conv_w, conv_w, m_norm.reshape(1, -1))
    ya = outs[0].reshape(B * tl, BRANCH)
    ya_c = outs[1].reshape(B * tc, BRANCH) if need_ctx else None
    return ya, ya_c


_WB = 128


def _win_kernel(sink_ref, q_ref, *refs, has_latent, qb, seq):
    if has_latent:
        kl_ref, vl_ref, kx_ref, vx_ref, o_ref = refs
    else:
        kx_ref, vx_ref, o_ref = refs
    n = pl.program_id(1)
    rep = W_HEADS // W_KV_HEADS
    rows = rep * _WB
    span = 3 * _WB
    lane = lax.broadcasted_iota(jnp.int32, (_WB, LANES), 1)
    left = lane < W_DIM
    zero = jnp.zeros((_WB, LANES), BF16)
    if has_latent:
        qq = jnp.bitwise_and(lax.broadcasted_iota(jnp.int32, (rows, span), 0), _WB - 1)
        dist = lax.broadcasted_iota(jnp.int32, (rows, span), 1) - qq
    windows = {}

    def window_of(j):
        if j not in windows:
            q_start = (n * qb + j) * _WB
            k_start = pl.multiple_of(jnp.clip(q_start - _WB, 0, seq - span), _WB)
            off = pltpu.bitcast(dist + (k_start - q_start + _WB), jnp.uint32)
            windows[j] = (k_start, off <= jnp.uint32(2 * _WB))
        return windows[j]

    def logits_of(j, g):
        r0 = j * _WB
        pieces = []
        for t in range(rep // 2):
            qt = q_ref[0, r0:r0 + _WB, (g * rep // 2 + t) * LANES:(g * rep // 2 + t + 1) * LANES]
            pieces += [jnp.where(left, qt, zero), jnp.where(left, zero, qt)]
        lhs = jnp.concatenate(pieces, axis=0)
        gsl = slice(g * LANES, (g + 1) * LANES)
        logits = [_dot_nt(lhs, kx_ref[0, :, gsl])]
        vals = [vx_ref[0, :, gsl]]
        if has_latent:
            k_start, near = window_of(j)
            s_win = _dot_nt(lhs, kl_ref[0, pl.ds(k_start, span), gsl])
            logits.append(jnp.where(near, s_win, -jnp.inf))
            vals.append(vl_ref[0, pl.ds(k_start, span), gsl])
        return logits, vals

    def finish(j, g, logits, vals):
        r0 = j * _WB
        sink_rep = jnp.concatenate(
            [jnp.full((_WB, LANES), sink_ref[g * rep + r] * LOG2E, F32) for r in range(rep)], axis=0)
        m_part = None
        for s in logits:
            for t in range(s.shape[1] // LANES):
                st = s[:, t * LANES:(t + 1) * LANES]
                m_part = st if m_part is None else jnp.maximum(m_part, st)
        m = jnp.maximum(sink_rep, jnp.max(m_part, axis=-1, keepdims=True))
        acc = None
        for s, v in zip(logits, vals):
            p = jnp.concatenate([jnp.exp2(s[:, t * LANES:(t + 1) * LANES] - m)
                                 for t in range(s.shape[1] // LANES)], axis=1)
            part = _dot(p.astype(BF16), jnp.concatenate([v, jnp.ones(v.shape, BF16)], axis=1))
            acc = part if acc is None else acc + part
        o = acc[:, 0:LANES] / (acc[:, LANES:2 * LANES] + jnp.exp2(sink_rep - m))
        for t in range(rep // 2):
            tile = jnp.where(left, o[(2 * t) * _WB:(2 * t + 1) * _WB], o[(2 * t + 1) * _WB:(2 * t + 2) * _WB])
            c0 = (g * rep // 2 + t) * LANES
            o_ref[0, r0:r0 + _WB, c0:c0 + LANES] = tile.astype(BF16)

    streams = [(j, g) for j in range(qb) for g in range(W_KV_HEADS)]
    ahead = 3
    pending = []
    for i, (j, g) in enumerate(streams):
        pending.append((j, g) + logits_of(j, g))
        if i >= ahead:
            finish(*pending.pop(0))
    for item in pending:
        finish(*item)


def _window_attention(q, kx, vx, sink, B, tq, tcx, lat_kv=None, qb=4):
    q3 = q.reshape(B, tq, BRANCH)
    kx3, vx3 = kx.reshape(B, tcx, 2 * LANES), vx.reshape(B, tcx, 2 * LANES)
    nb = tq // _WB
    qb = math.gcd(qb, nb)
    has_latent = lat_kv is not None
    whole = lambda t: pl.BlockSpec((1, t, 2 * LANES), lambda b, n: (b, 0, 0))
    in_specs = [pl.BlockSpec(memory_space=pltpu.SMEM),
                pl.BlockSpec((1, qb * _WB, BRANCH), lambda b, n: (b, n, 0))]
    args = [sink.astype(F32), q3]
    if has_latent:
        assert tq >= 3 * _WB
        in_specs += [whole(tq), whole(tq)]
        args += [a.reshape(B, tq, 2 * LANES) for a in lat_kv]
    in_specs += [whole(tcx), whole(tcx)]
    args += [kx3, vx3]
    out = pl.pallas_call(
        functools.partial(_win_kernel, has_latent=has_latent, qb=qb, seq=tq),
        grid=(B, nb // qb),
        in_specs=in_specs,
        out_specs=pl.BlockSpec((1, qb * _WB, BRANCH), lambda b, n: (b, n, 0)),
        out_shape=jax.ShapeDtypeStruct((B, tq, BRANCH), BF16),
        compiler_params=_params(("arbitrary", "arbitrary")),
        name="win_lat" if has_latent else "win_ctx",
    )(*args)
    return out.reshape(B * tq, BRANCH)


def _diff_kernel(lam_ref, q_ref, *refs, tq, nsub, tk, tcx, tlat, lam_init):
    if tlat:
        kx_ref, vx_ref, kl_ref, vl_ref, nrm_ref, o_ref = refs
    else:
        kx_ref, vx_ref, nrm_ref, o_ref = refs
        kl_ref = vl_ref = None
    sub = tq // nsub
    lane = lax.broadcasted_iota(jnp.int32, (sub, LANES), 1)
    left = lane < LANES // 2
    zero = jnp.zeros((sub, LANES), BF16)
    tiles = [(kx_ref, vx_ref, 0, tcx)] + [(kl_ref, vl_ref, j * tk, tk) for j in range(tlat // tk)]
    ones = jnp.ones((max(tk, tcx), LANES), BF16)

    lhs = []
    for u in range(nsub):
        q = q_ref[0, u * sub:(u + 1) * sub, :]
        lhs += [jnp.where(left, q, zero), jnp.where(left, zero, q)]
    state = [None] * (2 * nsub)
    for k_ref, v_ref, r0, n in tiles:
        vx = jnp.concatenate([v_ref[0, r0:r0 + n, :], ones[0:n]], axis=1)
        k = k_ref[0, r0:r0 + n, :]
        for i in range(2 * nsub):
            s = _dot_nt(lhs[i], k)
            m_part = s[:, 0:LANES]
            for t in range(1, n // LANES):
                m_part = jnp.maximum(m_part, s[:, t * LANES:(t + 1) * LANES])
            m_j = jnp.max(m_part, axis=-1, keepdims=True)
            part = _dot(jnp.exp2((s - m_j).astype(BF16)), vx)
            if state[i] is None:
                state[i] = (m_j, part)
            else:
                m_run, acc = state[i]
                m_new = jnp.maximum(m_run, m_j)
                state[i] = (m_new, acc * jnp.exp2(m_run - m_new) + part * jnp.exp2(m_j - m_new))
    outs = [acc[:, 0:LANES] / acc[:, LANES:2 * LANES] for _, acc in state]
    lv = lam_ref[...]
    s1 = jnp.sum(lv[0:1] * lv[1:2], axis=-1, keepdims=True)
    s2 = jnp.sum(lv[2:3] * lv[3:4], axis=-1, keepdims=True)
    lam = jnp.exp(s1) - jnp.exp(s2) + lam_init
    for u in range(nsub):
        y = outs[2 * u] - lam * outs[2 * u + 1]
        ms = jnp.mean(y * y, axis=-1, keepdims=True)
        o_ref[0, u * sub:(u + 1) * sub, :] = (y * lax.rsqrt(ms + EPS) * nrm_ref[...] * (1.0 - lam_init)).astype(BF16)


def _diff_attention(q, kx, vx, lamv, d_norm, lam_init, B, tq_total, tcx, lat_kv=None, sub=1024, nsub=2, tk=1024):
    q3 = q.reshape(B, tq_total, BRANCH)
    kx3, vx3 = kx.reshape(B, tcx, BRANCH), vx.reshape(B, tcx, BRANCH)
    sub = min(sub, tq_total)
    nsub = math.gcd(nsub, tq_total // sub)
    tq = sub * nsub
    head_blk = lambda t, f: pl.BlockSpec((1, t, LANES), f)
    whole = lambda b, h, i: (b, 0, h)
    in_specs = [pl.BlockSpec((8, LANES), lambda b, h, i: (0, 0)),
                head_blk(tq, lambda b, h, i: (b, i, h)),
                head_blk(tcx, whole), head_blk(tcx, whole)]
    args = [lamv, q3, kx3, vx3]
    tlat = 0
    if lat_kv is not None:
        tlat = tq_total
        k3, v3 = (a.reshape(B, tlat, BRANCH) for a in lat_kv)
        in_specs += [head_blk(tlat, whole), head_blk(tlat, whole)]
        args += [k3, v3]
    in_specs.append(pl.BlockSpec((1, LANES), lambda b, h, i: (0, h)))
    args.append(d_norm.reshape(1, -1))
    tk = min(tk, max(tlat, LANES))
    out = pl.pallas_call(
        functools.partial(_diff_kernel, tq=tq, nsub=nsub, tk=tk, tcx=tcx, tlat=tlat, lam_init=lam_init),
        grid=(B, DF_HEADS, tq_total // tq),
        in_specs=in_specs,
        out_specs=head_blk(tq, lambda b, h, i: (b, i, h)),
        out_shape=jax.ShapeDtypeStruct((B, tq_total, BRANCH), BF16),
        compiler_params=_params(("arbitrary", "arbitrary", "arbitrary")),
        name="diff_lat" if tlat else "diff_ctx",
    )(*args)
    return out.reshape(B * tq_total, BRANCH)


def _merge_kernel(ya_ref, yb_ref, yc_ref, gate_ref, x_ref, mod_ref, wb_ref, wo_ref, o_ref, *, d):
    merged = None
    for i, y_ref in enumerate((ya_ref, yb_ref, yc_ref)):
        gate = jax.nn.sigmoid(gate_ref[:, i * d:(i + 1) * d].astype(F32))
        term = gate * _dot(y_ref[...], wb_ref[i])
        merged = term if merged is None else merged + term
    out = _dot(merged.astype(BF16), wo_ref[...])
    gt = mod_ref[0][:, 2 * d:3 * d]
    o_ref[...] = x_ref[...] + gt * out


def _merge(ya, yb, yc, gate, x2, mod, wb, wo, *, tm, rows_per_mod):
    N, D = x2.shape
    row = lambda w: pl.BlockSpec((tm, w), lambda i: (i, 0))
    return pl.pallas_call(
        functools.partial(_merge_kernel, d=D),
        grid=(N // tm,),
        in_specs=[row(BRANCH), row(BRANCH), row(BRANCH), row(N_BRANCH * D), row(D),
                  pl.BlockSpec((1, 1, 6 * D), lambda i: ((i * tm) // rows_per_mod, 0, 0)),
                  _resident((N_BRANCH, BRANCH, D), lambda i: (0, 0, 0)),
                  _resident((D, D), lambda i: (0, 0))],
        out_specs=row(D),
        out_shape=jax.ShapeDtypeStruct((N, D), F32),
        compiler_params=_params(("arbitrary",)),
        name="merge",
    )(ya, yb, yc, gate, x2, mod, wb, wo)


_FFN_CHUNK = 256


def _ffn_kernel(x_ref, mod_ref, g_ref, wi_ref, wo_ref, gf_ref, o_ref, *, d, hidden, final_norm):
    x = x_ref[...]
    mod = mod_ref[0]
    h = _modnorm(x, g_ref[...], mod[:, 3 * d:4 * d], mod[:, 4 * d:5 * d]).astype(BF16)
    acc = None
    for c0 in range(0, hidden, _FFN_CHUNK):
        gate = _dot(h, wi_ref[:, c0:c0 + _FFN_CHUNK])
        up = _dot(h, wi_ref[:, hidden + c0:hidden + c0 + _FFN_CHUNK])
        a = (gate * jax.nn.sigmoid(gate) * up).astype(BF16)
        part = _dot(a, wo_ref[c0:c0 + _FFN_CHUNK, :])
        acc = part if acc is None else acc + part
    y = x + mod[:, 5 * d:6 * d] * acc
    if final_norm:
        ms = jnp.mean(y * y, axis=-1, keepdims=True)
        y = y * lax.rsqrt(ms + EPS) * gf_ref[...]
    o_ref[...] = y


def _ffn(x2, mod, g, wi, wo, g_final, *, tm, rows_per_mod, final_norm):
    N, D = x2.shape
    hidden = wo.shape[0]
    return pl.pallas_call(
        functools.partial(_ffn_kernel, d=D, hidden=hidden, final_norm=final_norm),
        grid=(N // tm,),
        in_specs=[pl.BlockSpec((tm, D), lambda i: (i, 0)),
                  pl.BlockSpec((1, 1, 6 * D), lambda i: ((i * tm) // rows_per_mod, 0, 0)),
                  pl.BlockSpec((1, D), lambda i: (0, 0)),
                  _resident((D, 2 * hidden), lambda i: (0, 0)),
                  _resident((hidden, D), lambda i: (0, 0)),
                  pl.BlockSpec((1, D), lambda i: (0, 0))],
        out_specs=pl.BlockSpec((tm, D), lambda i: (i, 0)),
        out_shape=jax.ShapeDtypeStruct((N, D), F32),
        compiler_params=_params(("arbitrary",)),
        name="ffn",
    )(x2, mod, g, wi, wo, g_final)


def _row_tile(n, want=512):
    t = want
    while n % t:
        t //= 2
    return t


def kernel(x, c, ctx, c_ctx, w_mod, b_mod, g_mix, g_ffn, w_in, b_gate, conv_w, m_norm, sink,
           lam_q1, lam_k1, lam_q2, lam_k2, d_norm, w_branch, w_out, w_ffn_in, w_ffn_out, g_final):
    B, T, D = x.shape
    Tc = ctx.shape[1]
    depth = w_mod.shape[0]
    assert T % CHUNK == 0 and Tc % CHUNK == 0 and T % GRID_W == 0

    n_rows = -(-(B + 1) // 16) * 16
    cc = jnp.concatenate([c, c_ctx[None, :], jnp.zeros((n_rows - B - 1, D), F32)], axis=0)
    mods = _adaln(cc, w_mod, b_mod)

    rope = _rope_tables(T)
    no_rope = tuple(jnp.zeros((Tc, LANES), F32) for _ in range(3))
    tm = _row_tile(T)
    tmc = _row_tile(Tc)
    names = [s[0] for s in _SEGS]

    xl = x.reshape(B * T, D)
    xc = ctx.reshape(B * Tc, D)
    for l in range(depth):
        need_ctx = l < depth - 1
        lam_init = 0.8 - 0.6 * math.exp(-0.3 * l)
        mod_l = mods[l, :B].reshape(B, 1, 6 * D)
        mod_c = mods[l, B:B + 1].reshape(1, 1, 6 * D)
        w_ext = _extend_w_in(w_in[l])
        g_mix_l = g_mix[l].reshape(1, D)
        g_ffn_l = g_ffn[l].reshape(1, D)
        lamv = jnp.zeros((8, LANES), F32)
        for i, v in enumerate((lam_q1[l], lam_k1[l], lam_q2[l], lam_k2[l])):
            lamv = lamv.at[i, :v.shape[0]].set(v.astype(F32))

        pl_ = dict(zip(names, _inproj(xl, mod_l, g_mix_l, rope, w_ext, tm=tm, rows_per_mod=T, seq=T, use_rope=True)))
        pc_ = dict(zip(names, _inproj(xc, mod_c, g_mix_l, no_rope, w_ext, tm=tmc, rows_per_mod=B * Tc, seq=Tc,
                                      use_rope=False)))

        gates_l = _gates(pl_["mg"], b_gate[l], B, T)
        gates_c = _gates(pc_["mg"], b_gate[l], B, Tc)
        ya, ya_c = _mlstm(pc_, pl_, gates_c, gates_l, conv_w[l], m_norm[l], B, Tc, T, need_ctx)
        yb = _window_attention(pl_["wq"], pc_["wk"], pc_["wv"], sink[l], B, T, Tc, lat_kv=(pl_["wk"], pl_["wv"]))
        yc = _diff_attention(pl_["dq"], pc_["dk"], pc_["dv"], lamv, d_norm[l], lam_init, B, T, Tc,
                             lat_kv=(pl_["dk"], pl_["dv"]))
        wb = w_branch[l].astype(BF16)
        wo = w_out[l].astype(BF16)
        wfi = w_ffn_in[l].astype(BF16)
        wfo = w_ffn_out[l].astype(BF16)
        gfin = g_final.reshape(1, D)
        last = l == depth - 1
        xl = _merge(ya, yb, yc, pl_["gate"], xl, mod_l, wb, wo, tm=tm, rows_per_mod=T)
        xl = _ffn(xl, mod_l, g_ffn_l, wfi, wfo, gfin, tm=tm, rows_per_mod=T, final_norm=last)
        if need_ctx:
            yb_c = _window_attention(pc_["wq"], pc_["wk"], pc_["wv"], sink[l], B, Tc, Tc)
            yc_c = _diff_attention(pc_["dq"], pc_["dk"], pc_["dv"], lamv, d_norm[l], lam_init, B, Tc, Tc)
            xc = _merge(ya_c, yb_c, yc_c, pc_["gate"], xc, mod_c, wb, wo, tm=tmc, rows_per_mod=B * Tc)
            xc = _ffn(xc, mod_c, g_ffn_l, wfi, wfo, gfin, tm=tmc, rows_per_mod=B * Tc, final_norm=False)
    return xl.reshape(B, T, D)
```

```python
import functools
import math

import jax
import jax.numpy as jnp
import numpy as np
from jax import lax
from jax.experimental import pallas as pl
from jax.experimental.pallas import tpu as pltpu

F32 = jnp.float32
BF16 = jnp.bfloat16

GRID_W = 64
BRANCH = 512
N_BRANCH = 3
M_HEADS = 4
M_DIM = 128
W_HEADS = 8
W_KV_HEADS = 2
W_DIM = 64
DF_HEADS = 4
ROPE_DIM = 64
ROPE_BASE = 10000.0
EPS = 1e-6
LANES = 128
CHUNK = 128
VMEM_LIMIT = 56 * 1024 * 1024

_NT = (((1,), (1,)), ((), ()))
LOG2E = math.log2(math.e)
_Q_SCALE = W_DIM ** -0.5 * LOG2E


def _dot(a, b):
    return jnp.dot(a, b, preferred_element_type=F32)


def _dot_nt(a, b):
    return lax.dot_general(a, b, _NT, preferred_element_type=F32)


def _params(sem):
    return pltpu.CompilerParams(dimension_semantics=sem, vmem_limit_bytes=VMEM_LIMIT)


def _resident(shape, index_map):
    return pl.BlockSpec(shape, index_map, pipeline_mode=pl.Buffered(1))


def _adaln_kernel(c_ref, w_ref, b_ref, o_ref):
    c = c_ref[...]
    s = (c * jax.nn.sigmoid(c)).astype(BF16)
    o_ref[0] = _dot(s, w_ref[0].astype(BF16)) + b_ref[0]


def _adaln(cc, w_mod, b_mod):
    L, D, N = w_mod.shape
    R = cc.shape[0]
    tn = 1536
    return pl.pallas_call(
        _adaln_kernel,
        grid=(L, N // tn),
        in_specs=[pl.BlockSpec((R, D), lambda l, j: (0, 0)),
                  pl.BlockSpec((1, D, tn), lambda l, j: (l, 0, j)),
                  pl.BlockSpec((1, 1, tn), lambda l, j: (l, 0, j))],
        out_specs=pl.BlockSpec((1, R, tn), lambda l, j: (l, 0, j)),
        out_shape=jax.ShapeDtypeStruct((L, R, N), F32),
        compiler_params=_params(("arbitrary", "arbitrary")),
        name="adaln",
    )(cc, w_mod, b_mod.reshape(L, 1, N))


def _modnorm(x, g, shift, scale):
    ms = jnp.mean(x * x, axis=-1, keepdims=True)
    return x * lax.rsqrt(ms + EPS) * (g * (1.0 + scale)) + shift


_SEGS = (("mqk", 1024, BF16, "plain"), ("mv", 512, BF16, "plain"), ("mo", 512, BF16, "plain"),
         ("mg", 128, F32, "plain"), ("wq", 512, BF16, "ropeq"), ("wk", 256, BF16, "rope"),
         ("wv", 256, BF16, "plain"), ("dq", 512, BF16, "ropeq"), ("dk", 512, BF16, "rope"),
         ("dv", 512, BF16, "plain"), ("gate", 3072, BF16, "plain"))
_EXT_WIDTH = sum(s[1] for s in _SEGS)
_COL_CHUNK = 512


def _inproj_kernel(x_ref, mod_ref, g_ref, cos_ref, sa_ref, sb_ref, w_ref, *out_refs, d, use_rope):
    x = x_ref[...]
    mod = mod_ref[0]
    h = _modnorm(x, g_ref[...], mod[:, 0:d], mod[:, d:2 * d]).astype(BF16)
    if use_rope:
        cos, sa, sb = cos_ref[...], sa_ref[...], sb_ref[...]
    off = 0
    for (name, width, dt, kind), o_ref in zip(_SEGS, out_refs):
        for c0 in range(0, width, _COL_CHUNK):
            cw = min(_COL_CHUNK, width - c0)
            acc = _dot(h, w_ref[:, off + c0:off + c0 + cw])
            if kind != "plain":
                tiles = []
                for t in range(cw // LANES):
                    a = acc[:, t * LANES:(t + 1) * LANES]
                    if use_rope:
                        a = (a * cos + pltpu.roll(a, LANES - 16, 1) * sa + pltpu.roll(a, 16, 1) * sb)
                    if kind == "ropeq":
                        a = a * _Q_SCALE
                    tiles.append(a)
                acc = jnp.concatenate(tiles, axis=1) if len(tiles) > 1 else tiles[0]
            o_ref[:, c0:c0 + cw] = acc.astype(dt)
        off += width


def _inproj(x2, mod, g, rope, w_ext, *, tm, rows_per_mod, seq, use_rope):
    N, D = x2.shape
    nt = seq // tm
    kern = functools.partial(_inproj_kernel, d=D, use_rope=use_rope)
    rope_spec = pl.BlockSpec((tm, LANES), lambda i: (i % nt, 0))
    out_shape = [jax.ShapeDtypeStruct((N, w), dt) for _, w, dt, _ in _SEGS]
    out_specs = [pl.BlockSpec((tm, w), lambda i: (i, 0)) for _, w, _, _ in _SEGS]
    return pl.pallas_call(
        kern,
        grid=(N // tm,),
        in_specs=[pl.BlockSpec((tm, D), lambda i: (i, 0)),
                  pl.BlockSpec((1, 1, 6 * D), lambda i: ((i * tm) // rows_per_mod, 0, 0)),
                  pl.BlockSpec((1, D), lambda i: (0, 0)),
                  rope_spec, rope_spec, rope_spec,
                  _resident((D, _EXT_WIDTH), lambda i: (0, 0))],
        out_specs=out_specs,
        out_shape=out_shape,
        compiler_params=_params(("arbitrary",)),
        name="inproj_lat" if use_rope else "inproj_ctx",
    )(x2, mod, g, *rope, w_ext)


def _extend_w_in(w):
    D = w.shape[0]
    o = np.cumsum([0, 512, 512, 512, 512, 16, 512, 128, 128, 512, 512, 512, 3072])
    p = [w[:, o[i]:o[i + 1]] for i in range(12)]
    dup = lambda a: jnp.concatenate([a[:, 0:64], a[:, 0:64], a[:, 64:128], a[:, 64:128]], axis=1)
    mg = jnp.concatenate([p[4], jnp.zeros((D, LANES - 16), w.dtype)], axis=1)
    ext = jnp.concatenate([p[0], p[1], p[2], p[3], mg, p[5], dup(p[6]), dup(p[7]), p[8], p[9], p[10], p[11]], axis=1)
    return ext.astype(BF16)


def _rope_tables(n_tokens):
    rows = n_tokens // GRID_W
    r, col = jnp.meshgrid(jnp.arange(rows), jnp.arange(GRID_W), indexing="ij")
    half = ROPE_DIM // 2
    inv = ROPE_BASE ** (-jnp.arange(0, half, 2, dtype=F32) / half)
    ang_r = r.reshape(-1, 1).astype(F32) * inv
    ang_c = col.reshape(-1, 1).astype(F32) * inv
    ang = jnp.concatenate([ang_r, ang_r, ang_c, ang_c], axis=-1)
    cos, sin = jnp.cos(ang), jnp.sin(ang)
    cos2 = jnp.concatenate([cos, cos], axis=-1)
    sin2 = jnp.concatenate([sin, sin], axis=-1)
    first = (jnp.arange(LANES) % 32) < 16
    sin_a = jnp.where(first, -sin2, 0.0)
    sin_b = jnp.where(first, 0.0, sin2)
    return cos2, sin_a, sin_b


def _scan_lanes(x, op, fill, reverse):
    lane = lax.broadcasted_iota(jnp.int32, x.shape, 1)
    d = 1
    while d < LANES:
        if reverse:
            shifted = jnp.where(lane < LANES - d, pltpu.roll(x, LANES - d, 1), fill)
        else:
            shifted = jnp.where(lane >= d, pltpu.roll(x, d, 1), fill)
        x = op(x, shifted)
        d *= 2
    return x


def _gates_kernel(x_ref, bias_ref, o_ref, *, rows):
    x = x_ref[0] + bias_ref[...]
    lane = lax.broadcasted_iota(jnp.int32, (rows, LANES), 1)
    for d, reverse in enumerate((False, True)):
        i_pre = x[(2 * d) * rows:(2 * d + 1) * rows]
        f_pre = x[(2 * d + 1) * rows:(2 * d + 2) * rows]
        log_f = jax.nn.log_sigmoid(f_pre)
        b = _scan_lanes(log_f, jnp.add, 0.0, reverse)
        a = i_pre - b
        pm = _scan_lanes(a, jnp.maximum, -jnp.inf, reverse)
        last = 0 if reverse else LANES - 1
        b_end = jnp.sum(jnp.where(lane == last, b, 0.0), axis=1, keepdims=True)
        a_max = jnp.max(a, axis=1, keepdims=True)
        o_ref[0, 5 * d + 0] = b
        o_ref[0, 5 * d + 1] = a
        o_ref[0, 5 * d + 2] = pm
        o_ref[0, 5 * d + 3] = jnp.broadcast_to(b_end, (rows, LANES))
        o_ref[0, 5 * d + 4] = jnp.broadcast_to(a_max, (rows, LANES))


def _gates(mg, b_gate, B, T):
    nc = T // CHUNK
    rows = M_HEADS * nc
    g = mg[:, :4 * M_HEADS].reshape(B, T, 4 * M_HEADS)
    gt = jnp.transpose(g, (0, 2, 1)).reshape(B, 4 * rows, LANES)
    bias = jnp.repeat(b_gate.astype(F32), nc).reshape(4 * rows, 1)
    return pl.pallas_call(
        functools.partial(_gates_kernel, rows=rows),
        grid=(B,),
        in_specs=[pl.BlockSpec((1, 4 * rows, LANES), lambda b: (b, 0, 0)),
                  pl.BlockSpec((4 * rows, 1), lambda b: (0, 0))],
        out_specs=pl.BlockSpec((1, 10, rows, LANES), lambda b: (b, 0, 0, 0)),
        out_shape=jax.ShapeDtypeStruct((B, 10, rows, LANES), F32),
        compiler_params=_params(("arbitrary",)),
        name="mlstm_gates",
    )(gt, bias)


_CONV_TILE = 256
_HB = 2


def _mlstm_kernel(qc_ref, kc_ref, vc_ref, oc_ref, pc_ref, ql_ref, kl_ref, vl_ref, ol_ref, pl_ref,
                  cwq_ref, cwk_ref, nrm_ref, *rest, tc, tl, need_ctx):
    if need_ctx:
        yl_ref, yc_ref = rest[0], rest[1]
        scr = rest[2:]
    else:
        yl_ref, yc_ref = rest[0], None
        scr = rest[1:]
    xpad, qs, ks, hf, hb, cx, mst = scr
    head0 = pl.program_id(1) * _HB

    def conv_stream(u_ref, w_ref, dst, dst_off, ts, scale):
        for hh in range(_HB):
            xpad[hh, 0:8, :] = jnp.zeros((8, LANES), F32)
            xpad[hh, 8:8 + ts, :] = u_ref[0, :, hh * LANES:(hh + 1) * LANES].astype(F32)
            xpad[hh, 8 + ts:16 + ts, :] = jnp.zeros((8, LANES), F32)
        w = w_ref[...]
        w0, w1, w2 = w[0:1], w[1:2], w[2:3]
        tile = min(_CONV_TILE, ts)

        def body(i, _):
            r0 = pl.multiple_of(i * tile, 8)
            for hh in range(_HB):
                lanes = slice(hh * LANES, (hh + 1) * LANES)
                prev = xpad[hh, pl.ds(r0 + 7, tile), :]
                cur = xpad[hh, pl.ds(r0 + 8, tile), :]
                nxt = xpad[hh, pl.ds(r0 + 9, tile), :]
                y = prev * w0[:, lanes] + cur * w1[:, lanes] + nxt * w2[:, lanes]
                y = y * jax.nn.sigmoid(y) * scale
                dst[pl.ds(pl.multiple_of(dst_off + r0, 16), tile), lanes] = y.astype(BF16)
            return 0

        lax.fori_loop(0, ts // tile, body, 0)

    kscale = M_DIM ** -0.5
    conv_stream(qc_ref, cwq_ref, qs, 0, tc, 1.0)
    conv_stream(kc_ref, cwk_ref, ks, 0, tc, kscale)
    conv_stream(ql_ref, cwq_ref, qs, tc, tl, 1.0)
    conv_stream(kl_ref, cwk_ref, ks, tc, tl, kscale)

    cx[...] = jnp.zeros(cx.shape, F32)
    mst[...] = jnp.zeros(mst.shape, F32)

    row_i = lax.broadcasted_iota(jnp.int32, (CHUNK, CHUNK), 0)
    col_i = lax.broadcasted_iota(jnp.int32, (CHUNK, CHUNK), 1)
    masks = (col_i <= row_i, col_i >= row_i)
    ones_b = jnp.ones((CHUNK, LANES), BF16)
    ones_f = jnp.ones((LANES, CHUNK), F32)

    def colify(row):
        return jnp.broadcast_to(row, (CHUNK, CHUNK)).T

    def chunk_step(hh, d, c, p_ref, v_ref, nc, base, want_h):
        r = (head0 + hh) * nc + c
        lanes = slice(hh * LANES, (hh + 1) * LANES)
        st = 2 * hh + d
        plane = lambda j: p_ref[0, 5 * d + j, pl.ds(r, 1), :]
        b_row, a_row, pm_row, bend_row, amax_row = (plane(j) for j in range(5))
        m_prev = mst[st:st + 1, :]
        m_end = jnp.maximum(m_prev, amax_row)
        w_end = jnp.exp(a_row - m_end)
        decay = jnp.exp(m_prev - m_end)
        row0 = pl.multiple_of(base + c * CHUNK, CHUNK)
        q = qs[pl.ds(row0, CHUNK), lanes]
        k = ks[pl.ds(row0, CHUNK), lanes]
        v = v_ref[0, pl.ds(pl.multiple_of(c * CHUNK, CHUNK), CHUNK), lanes]
        c_old = cx[st]
        if want_h:
            m_col = jnp.maximum(m_prev, colify(pm_row))
            w_intra = jnp.exp(jnp.where(masks[d], a_row - m_col, -jnp.inf))
            w_inter = jnp.exp(m_prev - m_col)
            floor = jnp.exp(-(colify(b_row) + m_col))
            s = _dot_nt(q, k) * w_intra
            intra = _dot(s.astype(BF16), jnp.concatenate([v, ones_b], axis=1))
            inter = _dot_nt(q, c_old.astype(BF16))
            num = intra[:, 0:M_DIM] + w_inter * inter[:, 0:M_DIM]
            den = intra[:, M_DIM:2 * M_DIM] + w_inter * inter[:, M_DIM:2 * M_DIM]
            hval = num / jnp.maximum(jnp.abs(den), floor)
            dst = hf if d == 0 else hb
            dst[pl.ds(row0, CHUNK), lanes] = hval
        vxt = jnp.concatenate([v.astype(F32).T, ones_f], axis=0)
        cx[st] = decay * c_old + _dot((vxt * w_end).astype(BF16), k)
        mst[st:st + 1, :] = bend_row + m_end

    ncc, ncl = tc // CHUNK, tl // CHUNK
    for c in range(ncc):
        for hh in range(_HB):
            chunk_step(hh, 0, c, pc_ref, vc_ref, ncc, 0, need_ctx)
            chunk_step(hh, 1, ncc - 1 - c, pc_ref, vc_ref, ncc, 0, need_ctx)

    def lat_body(c, _):
        for hh in range(_HB):
            chunk_step(hh, 0, c, pl_ref, vl_ref, ncl, tc, True)
            chunk_step(hh, 1, ncl - 1 - c, pl_ref, vl_ref, ncl, tc, True)
        return 0

    lax.fori_loop(0, ncl, lat_body, 0, unroll=8)

    nrm = nrm_ref[...]

    def finish(o_ref, y_ref, base, ts):
        def body(i, _):
            r0 = pl.multiple_of(i * CHUNK, CHUNK)
            hsum = hf[pl.ds(base + r0, CHUNK), :] + hb[pl.ds(base + r0, CHUNK), :]
            y = hsum * jax.nn.sigmoid(o_ref[0, pl.ds(r0, CHUNK), :].astype(F32))
            for hh in range(_HB):
                lanes = slice(hh * LANES, (hh + 1) * LANES)
                yh = y[:, lanes]
                ms = jnp.mean(yh * yh, axis=-1, keepdims=True)
                y_ref[0, pl.ds(r0, CHUNK), lanes] = (yh * lax.rsqrt(ms + EPS) * nrm[:, lanes]).astype(BF16)
            return 0

        lax.fori_loop(0, ts // CHUNK, body, 0, unroll=min(4, ts // CHUNK))

    finish(ol_ref, yl_ref, tc, tl)
    if need_ctx:
        finish(oc_ref, yc_ref, 0, tc)


def _mlstm(ctx_p, lat_p, gates_c, gates_l, conv_w, m_norm, B, tc, tl, need_ctx):
    r3 = lambda a, t: a.reshape(B, t, a.shape[-1])
    qk_c, v_c, o_c = r3(ctx_p["mqk"], tc), r3(ctx_p["mv"], tc), r3(ctx_p["mo"], tc)
    qk_l, v_l, o_l = r3(lat_p["mqk"], tl), r3(lat_p["mv"], tl), r3(lat_p["mo"], tl)
    G = M_HEADS // _HB
    width = _HB * LANES
    head_blk = lambda t, off: pl.BlockSpec((1, t, width), lambda b, g: (b, 0, g + off))
    gate_blk = lambda g: pl.BlockSpec((1,) + g.shape[1:], lambda b, h: (b, 0, 0, 0))
    in_specs = [head_blk(tc, 0), head_blk(tc, G), head_blk(tc, 0), head_blk(tc, 0), gate_blk(gates_c),
                head_blk(tl, 0), head_blk(tl, G), head_blk(tl, 0), head_blk(tl, 0), gate_blk(gates_l),
                pl.BlockSpec((3, width), lambda b, g: (0, g)),
                pl.BlockSpec((3, width), lambda b, g: (0, g + G)),
                pl.BlockSpec((1, width), lambda b, g: (0, g))]
    out_shape = [jax.ShapeDtypeStruct((B, tl, BRANCH), BF16)]
    out_specs = [head_blk(tl, 0)]
    if need_ctx:
        out_shape.append(jax.ShapeDtypeStruct((B, tc, BRANCH), BF16))
        out_specs.append(head_blk(tc, 0))
    tot = tc + tl
    scratch = [pltpu.VMEM((_HB, max(tc, tl) + 16, LANES), F32),
               pltpu.VMEM((tot, width), BF16), pltpu.VMEM((tot, width), BF16),
               pltpu.VMEM((tot, width), F32), pltpu.VMEM((tot, width), F32),
               pltpu.VMEM((2 * _HB, 2 * LANES, LANES), F32), pltpu.VMEM((8, LANES), F32)]
    outs = pl.pallas_call(
        functools.partial(_mlstm_kernel, tc=tc, tl=tl, need_ctx=need_ctx),
        grid=(B, G),
        in_specs=in_specs,
        out_specs=out_specs,
        out_shape=out_shape,
        scratch_shapes=scratch,
        compiler_params=_params(("arbitrary", "arbitrary")),
        name="mlstm",
    )(qk_c, qk_c, v_c, o_c, gates_c, qk_l, qk_l, v_l, o_l, gates_l,
      conv_w, conv_w, m_norm.reshape(1, -1))
    ya = outs[0].reshape(B * tl, BRANCH)
    ya_c = outs[1].reshape(B * tc, BRANCH) if need_ctx else None
    return ya, ya_c


_WB = 128


def _win_kernel(sink_ref, q_ref, *refs, has_latent, qb, seq):
    if has_latent:
        kl_ref, vl_ref, kx_ref, vx_ref, o_ref = refs
    else:
        kx_ref, vx_ref, o_ref = refs
    n = pl.program_id(1)
    rep = W_HEADS // W_KV_HEADS
    rows = rep * _WB
    span = 3 * _WB
    lane = lax.broadcasted_iota(jnp.int32, (_WB, LANES), 1)
    left = lane < W_DIM
    zero = jnp.zeros((_WB, LANES), BF16)
    if has_latent:
        qq = jnp.bitwise_and(lax.broadcasted_iota(jnp.int32, (rows, span), 0), _WB - 1)
        dist = lax.broadcasted_iota(jnp.int32, (rows, span), 1) - qq
    windows = {}

    def window_of(j):
        if j not in windows:
            q_start = (n * qb + j) * _WB
            k_start = pl.multiple_of(jnp.clip(q_start - _WB, 0, seq - span), _WB)
            off = pltpu.bitcast(dist + (k_start - q_start + _WB), jnp.uint32)
            windows[j] = (k_start, off <= jnp.uint32(2 * _WB))
        return windows[j]

    def logits_of(j, g):
        r0 = j * _WB
        pieces = []
        for t in range(rep // 2):
            qt = q_ref[0, r0:r0 + _WB, (g * rep // 2 + t) * LANES:(g * rep // 2 + t + 1) * LANES]
            pieces += [jnp.where(left, qt, zero), jnp.where(left, zero, qt)]
        lhs = jnp.concatenate(pieces, axis=0)
        gsl = slice(g * LANES, (g + 1) * LANES)
        logits = [_dot_nt(lhs, kx_ref[0, :, gsl])]
        vals = [vx_ref[0, :, gsl]]
        if has_latent:
            k_start, near = window_of(j)
            s_win = _dot_nt(lhs, kl_ref[0, pl.ds(k_start, span), gsl])
            logits.append(jnp.where(near, s_win, -jnp.inf))
            vals.append(vl_ref[0, pl.ds(k_start, span), gsl])
        return logits, vals

    def finish(j, g, logits, vals):
        r0 = j * _WB
        sink_rep = jnp.concatenate(
            [jnp.full((_WB, LANES), sink_ref[g * rep + r] * LOG2E, F32) for r in range(rep)], axis=0)
        m_part = None
        for s in logits:
            for t in range(s.shape[1] // LANES):
                st = s[:, t * LANES:(t + 1) * LANES]
                m_part = st if m_part is None else jnp.maximum(m_part, st)
        m = jnp.maximum(sink_rep, jnp.max(m_part, axis=-1, keepdims=True))
        acc = None
        for s, v in zip(logits, vals):
            p = jnp.concatenate([jnp.exp2(s[:, t * LANES:(t + 1) * LANES] - m)
                                 for t in range(s.shape[1] // LANES)], axis=1)
            part = _dot(p.astype(BF16), jnp.concatenate([v, jnp.ones(v.shape, BF16)], axis=1))
            acc = part if acc is None else acc + part
        o = acc[:, 0:LANES] / (acc[:, LANES:2 * LANES] + jnp.exp2(sink_rep - m))
        for t in range(rep // 2):
            tile = jnp.where(left, o[(2 * t) * _WB:(2 * t + 1) * _WB], o[(2 * t + 1) * _WB:(2 * t + 2) * _WB])
            c0 = (g * rep // 2 + t) * LANES
            o_ref[0, r0:r0 + _WB, c0:c0 + LANES] = tile.astype(BF16)

    streams = [(j, g) for j in range(qb) for g in range(W_KV_HEADS)]
    ahead = 3
    pending = []
    for i, (j, g) in enumerate(streams):
        pending.append((j, g) + logits_of(j, g))
        if i >= ahead:
            finish(*pending.pop(0))
    for item in pending:
        finish(*item)


def _window_attention(q, kx, vx, sink, B, tq, tcx, lat_kv=None, qb=4):
    q3 = q.reshape(B, tq, BRANCH)
    kx3, vx3 = kx.reshape(B, tcx, 2 * LANES), vx.reshape(B, tcx, 2 * LANES)
    nb = tq // _WB
    qb = math.gcd(qb, nb)
    has_latent = lat_kv is not None
    whole = lambda t: pl.BlockSpec((1, t, 2 * LANES), lambda b, n: (b, 0, 0))
    in_specs = [pl.BlockSpec(memory_space=pltpu.SMEM),
                pl.BlockSpec((1, qb * _WB, BRANCH), lambda b, n: (b, n, 0))]
    args = [sink.astype(F32), q3]
    if has_latent:
        assert tq >= 3 * _WB
        in_specs += [whole(tq), whole(tq)]
        args += [a.reshape(B, tq, 2 * LANES) for a in lat_kv]
    in_specs += [whole(tcx), whole(tcx)]
    args += [kx3, vx3]
    out = pl.pallas_call(
        functools.partial(_win_kernel, has_latent=has_latent, qb=qb, seq=tq),
        grid=(B, nb // qb),
        in_specs=in_specs,
        out_specs=pl.BlockSpec((1, qb * _WB, BRANCH), lambda b, n: (b, n, 0)),
        out_shape=jax.ShapeDtypeStruct((B, tq, BRANCH), BF16),
        compiler_params=_params(("arbitrary", "arbitrary")),
        name="win_lat" if has_latent else "win_ctx",
    )(*args)
    return out.reshape(B * tq, BRANCH)


def _diff_kernel(lam_ref, q_ref, *refs, tq, nsub, tk, tcx, tlat, lam_init):
    if tlat:
        kx_ref, vx_ref, kl_ref, vl_ref, nrm_ref, o_ref = refs
    else:
        kx_ref, vx_ref, nrm_ref, o_ref = refs
        kl_ref = vl_ref = None
    sub = tq // nsub
    lane = lax.broadcasted_iota(jnp.int32, (sub, LANES), 1)
    left = lane < LANES // 2
    zero = jnp.zeros((sub, LANES), BF16)
    tiles = [(kx_ref, vx_ref, 0, tcx)] + [(kl_ref, vl_ref, j * tk, tk) for j in range(tlat // tk)]
    ones = jnp.ones((max(tk, tcx), LANES), BF16)

    lhs = []
    for u in range(nsub):
        q = q_ref[0, u * sub:(u + 1) * sub, :]
        lhs += [jnp.where(left, q, zero), jnp.where(left, zero, q)]
    state = [None] * (2 * nsub)
    for k_ref, v_ref, r0, n in tiles:
        vx = jnp.concatenate([v_ref[0, r0:r0 + n, :], ones[0:n]], axis=1)
        k = k_ref[0, r0:r0 + n, :]
        for i in range(2 * nsub):
            s = _dot_nt(lhs[i], k)
            m_part = s[:, 0:LANES]
            for t in range(1, n // LANES):
                m_part = jnp.maximum(m_part, s[:, t * LANES:(t + 1) * LANES])
            m_j = jnp.max(m_part, axis=-1, keepdims=True)
            part = _dot(jnp.exp2(s - m_j).astype(BF16), vx)
            if state[i] is None:
                state[i] = (m_j, part)
            else:
                m_run, acc = state[i]
                m_new = jnp.maximum(m_run, m_j)
                state[i] = (m_new, acc * jnp.exp2(m_run - m_new) + part * jnp.exp2(m_j - m_new))
    outs = [acc[:, 0:LANES] / acc[:, LANES:2 * LANES] for _, acc in state]
    lv = lam_ref[...]
    s1 = jnp.sum(lv[0:1] * lv[1:2], axis=-1, keepdims=True)
    s2 = jnp.sum(lv[2:3] * lv[3:4], axis=-1, keepdims=True)
    lam = jnp.exp(s1) - jnp.exp(s2) + lam_init
    for u in range(nsub):
        y = outs[2 * u] - lam * outs[2 * u + 1]
        ms = jnp.mean(y * y, axis=-1, keepdims=True)
        o_ref[0, u * sub:(u + 1) * sub, :] = (y * lax.rsqrt(ms + EPS) * nrm_ref[...] * (1.0 - lam_init)).astype(BF16)


def _diff_attention(q, kx, vx, lamv, d_norm, lam_init, B, tq_total, tcx, lat_kv=None, sub=1024, nsub=2, tk=1024):
    q3 = q.reshape(B, tq_total, BRANCH)
    kx3, vx3 = kx.reshape(B, tcx, BRANCH), vx.reshape(B, tcx, BRANCH)
    sub = min(sub, tq_total)
    nsub = math.gcd(nsub, tq_total // sub)
    tq = sub * nsub
    head_blk = lambda t, f: pl.BlockSpec((1, t, LANES), f)
    whole = lambda b, h, i: (b, 0, h)
    in_specs = [pl.BlockSpec((8, LANES), lambda b, h, i: (0, 0)),
                head_blk(tq, lambda b, h, i: (b, i, h)),
                head_blk(tcx, whole), head_blk(tcx, whole)]
    args = [lamv, q3, kx3, vx3]
    tlat = 0
    if lat_kv is not None:
        tlat = tq_total
        k3, v3 = (a.reshape(B, tlat, BRANCH) for a in lat_kv)
        in_specs += [head_blk(tlat, whole), head_blk(tlat, whole)]
        args += [k3, v3]
    in_specs.append(pl.BlockSpec((1, LANES), lambda b, h, i: (0, h)))
    args.append(d_norm.reshape(1, -1))
    tk = min(tk, max(tlat, LANES))
    out = pl.pallas_call(
        functools.partial(_diff_kernel, tq=tq, nsub=nsub, tk=tk, tcx=tcx, tlat=tlat, lam_init=lam_init),
        grid=(B, DF_HEADS, tq_total // tq),
        in_specs=in_specs,
        out_specs=head_blk(tq, lambda b, h, i: (b, i, h)),
        out_shape=jax.ShapeDtypeStruct((B, tq_total, BRANCH), BF16),
        compiler_params=_params(("arbitrary", "arbitrary", "arbitrary")),
        name="diff_lat" if tlat else "diff_ctx",
    )(*args)
    return out.reshape(B * tq_total, BRANCH)


def _merge_kernel(ya_ref, yb_ref, yc_ref, gate_ref, x_ref, mod_ref, wb_ref, wo_ref, o_ref, *, d):
    merged = None
    for i, y_ref in enumerate((ya_ref, yb_ref, yc_ref)):
        gate = jax.nn.sigmoid(gate_ref[:, i * d:(i + 1) * d].astype(F32))
        term = gate * _dot(y_ref[...], wb_ref[i])
        merged = term if merged is None else merged + term
    out = _dot(merged.astype(BF16), wo_ref[...])
    gt = mod_ref[0][:, 2 * d:3 * d]
    o_ref[...] = x_ref[...] + gt * out


def _merge(ya, yb, yc, gate, x2, mod, wb, wo, *, tm, rows_per_mod):
    N, D = x2.shape
    row = lambda w: pl.BlockSpec((tm, w), lambda i: (i, 0))
    return pl.pallas_call(
        functools.partial(_merge_kernel, d=D),
        grid=(N // tm,),
        in_specs=[row(BRANCH), row(BRANCH), row(BRANCH), row(N_BRANCH * D), row(D),
                  pl.BlockSpec((1, 1, 6 * D), lambda i: ((i * tm) // rows_per_mod, 0, 0)),
                  _resident((N_BRANCH, BRANCH, D), lambda i: (0, 0, 0)),
                  _resident((D, D), lambda i: (0, 0))],
        out_specs=row(D),
        out_shape=jax.ShapeDtypeStruct((N, D), F32),
        compiler_params=_params(("arbitrary",)),
        name="merge",
    )(ya, yb, yc, gate, x2, mod, wb, wo)


_FFN_CHUNK = 256


def _ffn_kernel(x_ref, mod_ref, g_ref, wi_ref, wo_ref, gf_ref, o_ref, *, d, hidden, final_norm):
    x = x_ref[...]
    mod = mod_ref[0]
    h = _modnorm(x, g_ref[...], mod[:, 3 * d:4 * d], mod[:, 4 * d:5 * d]).astype(BF16)
    acc = None
    for c0 in range(0, hidden, _FFN_CHUNK):
        gate = _dot(h, wi_ref[:, c0:c0 + _FFN_CHUNK])
        up = _dot(h, wi_ref[:, hidden + c0:hidden + c0 + _FFN_CHUNK])
        a = (gate * jax.nn.sigmoid(gate) * up).astype(BF16)
        part = _dot(a, wo_ref[c0:c0 + _FFN_CHUNK, :])
        acc = part if acc is None else acc + part
    y = x + mod[:, 5 * d:6 * d] * acc
    if final_norm:
        ms = jnp.mean(y * y, axis=-1, keepdims=True)
        y = y * lax.rsqrt(ms + EPS) * gf_ref[...]
    o_ref[...] = y


def _ffn(x2, mod, g, wi, wo, g_final, *, tm, rows_per_mod, final_norm):
    N, D = x2.shape
    hidden = wo.shape[0]
    return pl.pallas_call(
        functools.partial(_ffn_kernel, d=D, hidden=hidden, final_norm=final_norm),
        grid=(N // tm,),
        in_specs=[pl.BlockSpec((tm, D), lambda i: (i, 0)),
                  pl.BlockSpec((1, 1, 6 * D), lambda i: ((i * tm) // rows_per_mod, 0, 0)),
                  pl.BlockSpec((1, D), lambda i: (0, 0)),
                  _resident((D, 2 * hidden), lambda i: (0, 0)),
                  _resident((hidden, D), lambda i: (0, 0)),
                  pl.BlockSpec((1, D), lambda i: (0, 0))],
        out_specs=pl.BlockSpec((tm, D), lambda i: (i, 0)),
        out_shape=jax.ShapeDtypeStruct((N, D), F32),
        compiler_params=_params(("arbitrary",)),
        name="ffn",
    )(x2, mod, g, wi, wo, g_final)


def _row_tile(n, want=512):
    t = want
    while n % t:
        t //= 2
    return t


def kernel(x, c, ctx, c_ctx, w_mod, b_mod, g_mix, g_ffn, w_in, b_gate, conv_w, m_norm, sink,
           lam_q1, lam_k1, lam_q2, lam_k2, d_norm, w_branch, w_out, w_ffn_in, w_ffn_out, g_final):
    B, T, D = x.shape
    Tc = ctx.shape[1]
    depth = w_mod.shape[0]
    assert T % CHUNK == 0 and Tc % CHUNK == 0 and T % GRID_W == 0

    n_rows = -(-(B + 1) // 16) * 16
    cc = jnp.concatenate([c, c_ctx[None, :], jnp.zeros((n_rows - B - 1, D), F32)], axis=0)
    mods = _adaln(cc, w_mod, b_mod)

    rope = _rope_tables(T)
    no_rope = tuple(jnp.zeros((Tc, LANES), F32) for _ in range(3))
    tm = _row_tile(T)
    tmc = _row_tile(Tc)
    names = [s[0] for s in _SEGS]

    xl = x.reshape(B * T, D)
    xc = ctx.reshape(B * Tc, D)
    for l in range(depth):
        need_ctx = l < depth - 1
        lam_init = 0.8 - 0.6 * math.exp(-0.3 * l)
        mod_l = mods[l, :B].reshape(B, 1, 6 * D)
        mod_c = mods[l, B:B + 1].reshape(1, 1, 6 * D)
        w_ext = _extend_w_in(w_in[l])
        g_mix_l = g_mix[l].reshape(1, D)
        g_ffn_l = g_ffn[l].reshape(1, D)
        lamv = jnp.zeros((8, LANES), F32)
        for i, v in enumerate((lam_q1[l], lam_k1[l], lam_q2[l], lam_k2[l])):
            lamv = lamv.at[i, :v.shape[0]].set(v.astype(F32))

        pl_ = dict(zip(names, _inproj(xl, mod_l, g_mix_l, rope, w_ext, tm=tm, rows_per_mod=T, seq=T, use_rope=True)))
        pc_ = dict(zip(names, _inproj(xc, mod_c, g_mix_l, no_rope, w_ext, tm=tmc, rows_per_mod=B * Tc, seq=Tc,
                                      use_rope=False)))

        gates_l = _gates(pl_["mg"], b_gate[l], B, T)
        gates_c = _gates(pc_["mg"], b_gate[l], B, Tc)
        ya, ya_c = _mlstm(pc_, pl_, gates_c, gates_l, conv_w[l], m_norm[l], B, Tc, T, need_ctx)
        yb = _window_attention(pl_["wq"], pc_["wk"], pc_["wv"], sink[l], B, T, Tc, lat_kv=(pl_["wk"], pl_["wv"]))
        yc = _diff_attention(pl_["dq"], pc_["dk"], pc_["dv"], lamv, d_norm[l], lam_init, B, T, Tc,
                             lat_kv=(pl_["dk"], pl_["dv"]))
        wb = w_branch[l].astype(BF16)
        wo = w_out[l].astype(BF16)
        wfi = w_ffn_in[l].astype(BF16)
        wfo = w_ffn_out[l].astype(BF16)
        gfin = g_final.reshape(1, D)
        last = l == depth - 1
        xl = _merge(ya, yb, yc, pl_["gate"], xl, mod_l, wb, wo, tm=tm, rows_per_mod=T)
        xl = _ffn(xl, mod_l, g_ffn_l, wfi, wfo, gfin, tm=tm, rows_per_mod=T, final_norm=last)
        if need_ctx:
            yb_c = _window_attention(pc_["wq"], pc_["wk"], pc_["wv"], sink[l], B, Tc, Tc)
            yc_c = _diff_attention(pc_["dq"], pc_["dk"], pc_["dv"], lamv, d_norm[l], lam_init, B, Tc, Tc)
            xc = _merge(ya_c, yb_c, yc_c, pc_["gate"], xc, mod_c, wb, wo, tm=tmc, rows_per_mod=B * Tc)
            xc = _ffn(xc, mod_c, g_ffn_l, wfi, wfo, gfin, tm=tmc, rows_per_mod=B * Tc, final_norm=False)
    return xl.reshape(B, T, D)
```

```python
import functools
import math

import jax
import jax.numpy as jnp
import numpy as np
from jax import lax
from jax.experimental import pallas as pl
from jax.experimental.pallas import tpu as pltpu

F32 = jnp.float32
BF16 = jnp.bfloat16

GRID_W = 64
BRANCH = 512
N_BRANCH = 3
M_HEADS = 4
M_DIM = 128
W_HEADS = 8
W_KV_HEADS = 2
W_DIM = 64
DF_HEADS = 4
ROPE_DIM = 64
ROPE_BASE = 10000.0
EPS = 1e-6
LANES = 128
CHUNK = 128
VMEM_LIMIT = 56 * 1024 * 1024

_NT = (((1,), (1,)), ((), ()))
LOG2E = math.log2(math.e)
_Q_SCALE = W_DIM ** -0.5 * LOG2E


def _dot(a, b):
    return jnp.dot(a, b, preferred_element_type=F32)


def _dot_nt(a, b):
    return lax.dot_general(a, b, _NT, preferred_element_type=F32)


def _params(sem):
    return pltpu.CompilerParams(dimension_semantics=sem, vmem_limit_bytes=VMEM_LIMIT)


def _resident(shape, index_map):
    return pl.BlockSpec(shape, index_map, pipeline_mode=pl.Buffered(1))


def _adaln_kernel(c_ref, w_ref, b_ref, o_ref):
    c = c_ref[...]
    s = (c * jax.nn.sigmoid(c)).astype(BF16)
    o_ref[0] = _dot(s, w_ref[0].astype(BF16)) + b_ref[0]


def _adaln(cc, w_mod, b_mod):
    L, D, N = w_mod.shape
    R = cc.shape[0]
    tn = 1536
    return pl.pallas_call(
        _adaln_kernel,
        grid=(L, N // tn),
        in_specs=[pl.BlockSpec((R, D), lambda l, j: (0, 0)),
                  pl.BlockSpec((1, D, tn), lambda l, j: (l, 0, j)),
                  pl.BlockSpec((1, 1, tn), lambda l, j: (l, 0, j))],
        out_specs=pl.BlockSpec((1, R, tn), lambda l, j: (l, 0, j)),
        out_shape=jax.ShapeDtypeStruct((L, R, N), F32),
        compiler_params=_params(("arbitrary", "arbitrary")),
        name="adaln",
    )(cc, w_mod, b_mod.reshape(L, 1, N))


def _modnorm(x, g, shift, scale):
    ms = jnp.mean(x * x, axis=-1, keepdims=True)
    return x * lax.rsqrt(ms + EPS) * (g * (1.0 + scale)) + shift


_SEGS = (("mqk", 1024, BF16, "plain"), ("mv", 512, BF16, "plain"), ("mo", 512, BF16, "plain"),
         ("mg", 128, F32, "plain"), ("wq", 512, BF16, "ropeq"), ("wk", 256, BF16, "rope"),
         ("wv", 256, BF16, "plain"), ("dq", 512, BF16, "ropeq"), ("dk", 512, BF16, "rope"),
         ("dv", 512, BF16, "plain"), ("gate", 3072, BF16, "plain"))
_EXT_WIDTH = sum(s[1] for s in _SEGS)
_COL_CHUNK = 512


def _inproj_kernel(x_ref, mod_ref, g_ref, cos_ref, sa_ref, sb_ref, w_ref, *out_refs, d, use_rope):
    x = x_ref[...]
    mod = mod_ref[0]
    h = _modnorm(x, g_ref[...], mod[:, 0:d], mod[:, d:2 * d]).astype(BF16)
    if use_rope:
        cos, sa, sb = cos_ref[...], sa_ref[...], sb_ref[...]
    off = 0
    for (name, width, dt, kind), o_ref in zip(_SEGS, out_refs):
        for c0 in range(0, width, _COL_CHUNK):
            cw = min(_COL_CHUNK, width - c0)
            acc = _dot(h, w_ref[:, off + c0:off + c0 + cw])
            if kind != "plain":
                tiles = []
                for t in range(cw // LANES):
                    a = acc[:, t * LANES:(t + 1) * LANES]
                    if use_rope:
                        a = (a * cos + pltpu.roll(a, LANES - 16, 1) * sa + pltpu.roll(a, 16, 1) * sb)
                    if kind == "ropeq":
                        a = a * _Q_SCALE
                    tiles.append(a)
                acc = jnp.concatenate(tiles, axis=1) if len(tiles) > 1 else tiles[0]
            o_ref[:, c0:c0 + cw] = acc.astype(dt)
        off += width


def _inproj(x2, mod, g, rope, w_ext, *, tm, rows_per_mod, seq, use_rope):
    N, D = x2.shape
    nt = seq // tm
    kern = functools.partial(_inproj_kernel, d=D, use_rope=use_rope)
    rope_spec = pl.BlockSpec((tm, LANES), lambda i: (i % nt, 0))
    out_shape = [jax.ShapeDtypeStruct((N, w), dt) for _, w, dt, _ in _SEGS]
    out_specs = [pl.BlockSpec((tm, w), lambda i: (i, 0)) for _, w, _, _ in _SEGS]
    return pl.pallas_call(
        kern,
        grid=(N // tm,),
        in_specs=[pl.BlockSpec((tm, D), lambda i: (i, 0)),
                  pl.BlockSpec((1, 1, 6 * D), lambda i: ((i * tm) // rows_per_mod, 0, 0)),
                  pl.BlockSpec((1, D), lambda i: (0, 0)),
                  rope_spec, rope_spec, rope_spec,
                  _resident((D, _EXT_WIDTH), lambda i: (0, 0))],
        out_specs=out_specs,
        out_shape=out_shape,
        compiler_params=_params(("arbitrary",)),
        name="inproj_lat" if use_rope else "inproj_ctx",
    )(x2, mod, g, *rope, w_ext)


def _extend_w_in(w):
    D = w.shape[0]
    o = np.cumsum([0, 512, 512, 512, 512, 16, 512, 128, 128, 512, 512, 512, 3072])
    p = [w[:, o[i]:o[i + 1]] for i in range(12)]
    dup = lambda a: jnp.concatenate([a[:, 0:64], a[:, 0:64], a[:, 64:128], a[:, 64:128]], axis=1)
    mg = jnp.concatenate([p[4], jnp.zeros((D, LANES - 16), w.dtype)], axis=1)
    ext = jnp.concatenate([p[0], p[1], p[2], p[3], mg, p[5], dup(p[6]), dup(p[7]), p[8], p[9], p[10], p[11]], axis=1)
    return ext.astype(BF16)


def _rope_tables(n_tokens):
    rows = n_tokens // GRID_W
    r, col = jnp.meshgrid(jnp.arange(rows), jnp.arange(GRID_W), indexing="ij")
    half = ROPE_DIM // 2
    inv = ROPE_BASE ** (-jnp.arange(0, half, 2, dtype=F32) / half)
    ang_r = r.reshape(-1, 1).astype(F32) * inv
    ang_c = col.reshape(-1, 1).astype(F32) * inv
    ang = jnp.concatenate([ang_r, ang_r, ang_c, ang_c], axis=-1)
    cos, sin = jnp.cos(ang), jnp.sin(ang)
    cos2 = jnp.concatenate([cos, cos], axis=-1)
    sin2 = jnp.concatenate([sin, sin], axis=-1)
    first = (jnp.arange(LANES) % 32) < 16
    sin_a = jnp.where(first, -sin2, 0.0)
    sin_b = jnp.where(first, 0.0, sin2)
    return cos2, sin_a, sin_b


def _scan_lanes(x, op, fill, reverse):
    lane = lax.broadcasted_iota(jnp.int32, x.shape, 1)
    d = 1
    while d < LANES:
        if reverse:
            shifted = jnp.where(lane < LANES - d, pltpu.roll(x, LANES - d, 1), fill)
        else:
            shifted = jnp.where(lane >= d, pltpu.roll(x, d, 1), fill)
        x = op(x, shifted)
        d *= 2
    return x


def _gates_kernel(x_ref, bias_ref, o_ref, *, rows):
    x = x_ref[0] + bias_ref[...]
    lane = lax.broadcasted_iota(jnp.int32, (rows, LANES), 1)
    for d, reverse in enumerate((False, True)):
        i_pre = x[(2 * d) * rows:(2 * d + 1) * rows]
        f_pre = x[(2 * d + 1) * rows:(2 * d + 2) * rows]
        log_f = jax.nn.log_sigmoid(f_pre)
        b = _scan_lanes(log_f, jnp.add, 0.0, reverse)
        a = i_pre - b
        pm = _scan_lanes(a, jnp.maximum, -jnp.inf, reverse)
        last = 0 if reverse else LANES - 1
        b_end = jnp.sum(jnp.where(lane == last, b, 0.0), axis=1, keepdims=True)
        a_max = jnp.max(a, axis=1, keepdims=True)
        o_ref[0, 5 * d + 0] = b
        o_ref[0, 5 * d + 1] = a
        o_ref[0, 5 * d + 2] = pm
        o_ref[0, 5 * d + 3] = jnp.broadcast_to(b_end, (rows, LANES))
        o_ref[0, 5 * d + 4] = jnp.broadcast_to(a_max, (rows, LANES))


def _gates(mg, b_gate, B, T):
    nc = T // CHUNK
    rows = M_HEADS * nc
    g = mg[:, :4 * M_HEADS].reshape(B, T, 4 * M_HEADS)
    gt = jnp.transpose(g, (0, 2, 1)).reshape(B, 4 * rows, LANES)
    bias = jnp.repeat(b_gate.astype(F32), nc).reshape(4 * rows, 1)
    return pl.pallas_call(
        functools.partial(_gates_kernel, rows=rows),
        grid=(B,),
        in_specs=[pl.BlockSpec((1, 4 * rows, LANES), lambda b: (b, 0, 0)),
                  pl.BlockSpec((4 * rows, 1), lambda b: (0, 0))],
        out_specs=pl.BlockSpec((1, 10, rows, LANES), lambda b: (b, 0, 0, 0)),
        out_shape=jax.ShapeDtypeStruct((B, 10, rows, LANES), F32),
        compiler_params=_params(("arbitrary",)),
        name="mlstm_gates",
    )(gt, bias)


_CONV_TILE = 256
_HB = 2


def _mlstm_kernel(qc_ref, kc_ref, vc_ref, oc_ref, pc_ref, ql_ref, kl_ref, vl_ref, ol_ref, pl_ref,
                  cwq_ref, cwk_ref, nrm_ref, *rest, tc, tl, need_ctx):
    if need_ctx:
        yl_ref, yc_ref = rest[0], rest[1]
        scr = rest[2:]
    else:
        yl_ref, yc_ref = rest[0], None
        scr = rest[1:]
    xpad, qs, ks, hf, hb, cx, mst = scr
    head0 = pl.program_id(1) * _HB

    def conv_stream(u_ref, w_ref, dst, dst_off, ts, scale):
        for hh in range(_HB):
            xpad[hh, 0:8, :] = jnp.zeros((8, LANES), F32)
            xpad[hh, 8:8 + ts, :] = u_ref[0, :, hh * LANES:(hh + 1) * LANES].astype(F32)
            xpad[hh, 8 + ts:16 + ts, :] = jnp.zeros((8, LANES), F32)
        w = w_ref[...]
        w0, w1, w2 = w[0:1], w[1:2], w[2:3]
        tile = min(_CONV_TILE, ts)

        def body(i, _):
            r0 = pl.multiple_of(i * tile, 8)
            for hh in range(_HB):
                lanes = slice(hh * LANES, (hh + 1) * LANES)
                prev = xpad[hh, pl.ds(r0 + 7, tile), :]
                cur = xpad[hh, pl.ds(r0 + 8, tile), :]
                nxt = xpad[hh, pl.ds(r0 + 9, tile), :]
                y = prev * w0[:, lanes] + cur * w1[:, lanes] + nxt * w2[:, lanes]
                y = y * jax.nn.sigmoid(y) * scale
                dst[pl.ds(pl.multiple_of(dst_off + r0, 16), tile), lanes] = y.astype(BF16)
            return 0

        lax.fori_loop(0, ts // tile, body, 0)

    kscale = M_DIM ** -0.5
    conv_stream(qc_ref, cwq_ref, qs, 0, tc, 1.0)
    conv_stream(kc_ref, cwk_ref, ks, 0, tc, kscale)
    conv_stream(ql_ref, cwq_ref, qs, tc, tl, 1.0)
    conv_stream(kl_ref, cwk_ref, ks, tc, tl, kscale)

    cx[...] = jnp.zeros(cx.shape, F32)
    mst[...] = jnp.zeros(mst.shape, F32)

    row_i = lax.broadcasted_iota(jnp.int32, (CHUNK, CHUNK), 0)
    col_i = lax.broadcasted_iota(jnp.int32, (CHUNK, CHUNK), 1)
    masks = (col_i <= row_i, col_i >= row_i)
    ones_b = jnp.ones((CHUNK, LANES), BF16)
    ones_f = jnp.ones((LANES, CHUNK), F32)

    def colify(row):
        return jnp.broadcast_to(row, (CHUNK, CHUNK)).T

    def chunk_step(hh, d, c, p_ref, v_ref, nc, base, want_h):
        r = (head0 + hh) * nc + c
        lanes = slice(hh * LANES, (hh + 1) * LANES)
        st = 2 * hh + d
        plane = lambda j: p_ref[0, 5 * d + j, pl.ds(r, 1), :]
        b_row, a_row, pm_row, bend_row, amax_row = (plane(j) for j in range(5))
        m_prev = mst[st:st + 1, :]
        m_end = jnp.maximum(m_prev, amax_row)
        w_end = jnp.exp(a_row - m_end)
        decay = jnp.exp(m_prev - m_end)
        row0 = pl.multiple_of(base + c * CHUNK, CHUNK)
        q = qs[pl.ds(row0, CHUNK), lanes]
        k = ks[pl.ds(row0, CHUNK), lanes]
        v = v_ref[0, pl.ds(pl.multiple_of(c * CHUNK, CHUNK), CHUNK), lanes]
        c_old = cx[st]
        if want_h:
            m_col = jnp.maximum(m_prev, colify(pm_row))
            w_intra = jnp.exp(jnp.where(masks[d], a_row - m_col, -jnp.inf))
            w_inter = jnp.exp(m_prev - m_col)
            floor = jnp.exp(-(colify(b_row) + m_col))
            s = _dot_nt(q, k) * w_intra
            intra = _dot(s.astype(BF16), jnp.concatenate([v, ones_b], axis=1))
            inter = _dot_nt(q, c_old.astype(BF16))
            num = intra[:, 0:M_DIM] + w_inter * inter[:, 0:M_DIM]
            den = intra[:, M_DIM:2 * M_DIM] + w_inter * inter[:, M_DIM:2 * M_DIM]
            hval = num / jnp.maximum(jnp.abs(den), floor)
            dst = hf if d == 0 else hb
            dst[pl.ds(row0, CHUNK), lanes] = hval
        vxt = jnp.concatenate([v.astype(F32).T, ones_f], axis=0)
        cx[st] = decay * c_old + _dot((vxt * w_end).astype(BF16), k)
        mst[st:st + 1, :] = bend_row + m_end

    ncc, ncl = tc // CHUNK, tl // CHUNK
    for c in range(ncc):
        for hh in range(_HB):
            chunk_step(hh, 0, c, pc_ref, vc_ref, ncc, 0, need_ctx)
            chunk_step(hh, 1, ncc - 1 - c, pc_ref, vc_ref, ncc, 0, need_ctx)

    def lat_body(c, _):
        for hh in range(_HB):
            chunk_step(hh, 0, c, pl_ref, vl_ref, ncl, tc, True)
            chunk_step(hh, 1, ncl - 1 - c, pl_ref, vl_ref, ncl, tc, True)
        return 0

    lax.fori_loop(0, ncl, lat_body, 0, unroll=8)

    nrm = nrm_ref[...]

    def finish(o_ref, y_ref, base, ts):
        def body(i, _):
            r0 = pl.multiple_of(i * CHUNK, CHUNK)
            hsum = hf[pl.ds(base + r0, CHUNK), :] + hb[pl.ds(base + r0, CHUNK), :]
            y = hsum * jax.nn.sigmoid(o_ref[0, pl.ds(r0, CHUNK), :].astype(F32))
            for hh in range(_HB):
                lanes = slice(hh * LANES, (hh + 1) * LANES)
                yh = y[:, lanes]
                ms = jnp.mean(yh * yh, axis=-1, keepdims=True)
                y_ref[0, pl.ds(r0, CHUNK), lanes] = (yh * lax.rsqrt(ms + EPS) * nrm[:, lanes]).astype(BF16)
            return 0

        lax.fori_loop(0, ts // CHUNK, body, 0, unroll=min(4, ts // CHUNK))

    finish(ol_ref, yl_ref, tc, tl)
    if need_ctx:
        finish(oc_ref, yc_ref, 0, tc)


def _mlstm(ctx_p, lat_p, gates_c, gates_l, conv_w, m_norm, B, tc, tl, need_ctx):
    r3 = lambda a, t: a.reshape(B, t, a.shape[-1])
    qk_c, v_c, o_c = r3(ctx_p["mqk"], tc), r3(ctx_p["mv"], tc), r3(ctx_p["mo"], tc)
    qk_l, v_l, o_l = r3(lat_p["mqk"], tl), r3(lat_p["mv"], tl), r3(lat_p["mo"], tl)
    G = M_HEADS // _HB
    width = _HB * LANES
    head_blk = lambda t, off: pl.BlockSpec((1, t, width), lambda b, g: (b, 0, g + off))
    gate_blk = lambda g: pl.BlockSpec((1,) + g.shape[1:], lambda b, h: (b, 0, 0, 0))
    in_specs = [head_blk(tc, 0), head_blk(tc, G), head_blk(tc, 0), head_blk(tc, 0), gate_blk(gates_c),
                head_blk(tl, 0), head_blk(tl, G), head_blk(tl, 0), head_blk(tl, 0), gate_blk(gates_l),
                pl.BlockSpec((3, width), lambda b, g: (0, g)),
                pl.BlockSpec((3, width), lambda b, g: (0, g + G)),
                pl.BlockSpec((1, width), lambda b, g: (0, g))]
    out_shape = [jax.ShapeDtypeStruct((B, tl, BRANCH), BF16)]
    out_specs = [head_blk(tl, 0)]
    if need_ctx:
        out_shape.append(jax.ShapeDtypeStruct((B, tc, BRANCH), BF16))
        out_specs.append(head_blk(tc, 0))
    tot = tc + tl
    scratch = [pltpu.VMEM((_HB, max(tc, tl) + 16, LANES), F32),
               pltpu.VMEM((tot, width), BF16), pltpu.VMEM((tot, width), BF16),
               pltpu.VMEM((tot, width), F32), pltpu.VMEM((tot, width), F32),
               pltpu.VMEM((2 * _HB, 2 * LANES, LANES), F32), pltpu.VMEM((8, LANES), F32)]
    outs = pl.pallas_call(
        functools.partial(_mlstm_kernel, tc=tc, tl=tl, need_ctx=need_ctx),
        grid=(B, G),
        in_specs=in_specs,
        out_specs=out_specs,
        out_shape=out_shape,
        scratch_shapes=scratch,
        compiler_params=_params(("arbitrary", "arbitrary")),
        name="mlstm",
    )(qk_c, qk_c, v_c, o_c, gates_c, qk_l, qk_l, v_l, o_l, gates_l,
      conv_w, conv_w, m_norm.reshape(1, -1))
    ya = outs[0].reshape(B * tl, BRANCH)
    ya_c = outs[1].reshape(B * tc, BRANCH) if need_ctx else None
    return ya, ya_c


_WB = 128


def _win_kernel(sink_ref, q_ref, *refs, has_latent, qb, seq):
    if has_latent:
        kl_ref, vl_ref, kx_ref, vx_ref, o_ref = refs
    else:
        kx_ref, vx_ref, o_ref = refs
    n = pl.program_id(1)
    rep = W_HEADS // W_KV_HEADS
    rows = rep * _WB
    span = 3 * _WB
    lane = lax.broadcasted_iota(jnp.int32, (_WB, LANES), 1)
    left = lane < W_DIM
    zero = jnp.zeros((_WB, LANES), BF16)
    if has_latent:
        qq = jnp.bitwise_and(lax.broadcasted_iota(jnp.int32, (rows, span), 0), _WB - 1)
        dist = lax.broadcasted_iota(jnp.int32, (rows, span), 1) - qq
    windows = {}

    def window_of(j):
        if j not in windows:
            q_start = (n * qb + j) * _WB
            k_start = pl.multiple_of(jnp.clip(q_start - _WB, 0, seq - span), _WB)
            off = pltpu.bitcast(dist + (k_start - q_start + _WB), jnp.uint32)
            windows[j] = (k_start, off <= jnp.uint32(2 * _WB))
        return windows[j]

    def logits_of(j, g):
        r0 = j * _WB
        pieces = []
        for t in range(rep // 2):
            qt = q_ref[0, r0:r0 + _WB, (g * rep // 2 + t) * LANES:(g * rep // 2 + t + 1) * LANES]
            pieces += [jnp.where(left, qt, zero), jnp.where(left, zero, qt)]
        lhs = jnp.concatenate(pieces, axis=0)
        gsl = slice(g * LANES, (g + 1) * LANES)
        logits = [_dot_nt(lhs, kx_ref[0, :, gsl])]
        vals = [vx_ref[0, :, gsl]]
        if has_latent:
            k_start, near = window_of(j)
            s_win = _dot_nt(lhs, kl_ref[0, pl.ds(k_start, span), gsl])
            logits.append(jnp.where(near, s_win, -jnp.inf))
            vals.append(vl_ref[0, pl.ds(k_start, span), gsl])
        return logits, vals

    def finish(j, g, logits, vals):
        r0 = j * _WB
        sink_rep = jnp.concatenate(
            [jnp.full((_WB, LANES), sink_ref[g * rep + r] * LOG2E, F32) for r in range(rep)], axis=0)
        m_part = None
        for s in logits:
            for t in range(s.shape[1] // LANES):
                st = s[:, t * LANES:(t + 1) * LANES]
                m_part = st if m_part is None else jnp.maximum(m_part, st)
        m = jnp.maximum(sink_rep, jnp.max(m_part, axis=-1, keepdims=True))
        acc = None
        for s, v in zip(logits, vals):
            p = jnp.concatenate([jnp.exp2(s[:, t * LANES:(t + 1) * LANES] - m)
                                 for t in range(s.shape[1] // LANES)], axis=1)
            part = _dot(p.astype(BF16), jnp.concatenate([v, jnp.ones(v.shape, BF16)], axis=1))
            acc = part if acc is None else acc + part
        o = acc[:, 0:LANES] / (acc[:, LANES:2 * LANES] + jnp.exp2(sink_rep - m))
        for t in range(rep // 2):
            tile = jnp.where(left, o[(2 * t) * _WB:(2 * t + 1) * _WB], o[(2 * t + 1) * _WB:(2 * t + 2) * _WB])
            c0 = (g * rep // 2 + t) * LANES
            o_ref[0, r0:r0 + _WB, c0:c0 + LANES] = tile.astype(BF16)

    streams = [(j, g) for j in range(qb) for g in range(W_KV_HEADS)]
    ahead = 3
    pending = []
    for i, (j, g) in enumerate(streams):
        pending.append((j, g) + logits_of(j, g))
        if i >= ahead:
            finish(*pending.pop(0))
    for item in pending:
        finish(*item)


def _window_attention(q, kx, vx, sink, B, tq, tcx, lat_kv=None, qb=4):
    q3 = q.reshape(B, tq, BRANCH)
    kx3, vx3 = kx.reshape(B, tcx, 2 * LANES), vx.reshape(B, tcx, 2 * LANES)
    nb = tq // _WB
    qb = math.gcd(qb, nb)
    has_latent = lat_kv is not None
    whole = lambda t: pl.BlockSpec((1, t, 2 * LANES), lambda b, n: (b, 0, 0))
    in_specs = [pl.BlockSpec(memory_space=pltpu.SMEM),
                pl.BlockSpec((1, qb * _WB, BRANCH), lambda b, n: (b, n, 0))]
    args = [sink.astype(F32), q3]
    if has_latent:
        assert tq >= 3 * _WB
        in_specs += [whole(tq), whole(tq)]
        args += [a.reshape(B, tq, 2 * LANES) for a in lat_kv]
    in_specs += [whole(tcx), whole(tcx)]
    args += [kx3, vx3]
    out = pl.pallas_call(
        functools.partial(_win_kernel, has_latent=has_latent, qb=qb, seq=tq),
        grid=(B, nb // qb),
        in_specs=in_specs,
        out_specs=pl.BlockSpec((1, qb * _WB, BRANCH), lambda b, n: (b, n, 0)),
        out_shape=jax.ShapeDtypeStruct((B, tq, BRANCH), BF16),
        compiler_params=_params(("arbitrary", "arbitrary")),
        name="win_lat" if has_latent else "win_ctx",
    )(*args)
    return out.reshape(B * tq, BRANCH)


def _diff_kernel(lam_ref, q_ref, *refs, tq, nsub, tk, tcx, tlat, lam_init):
    if tlat:
        kx_ref, vx_ref, kl_ref, vl_ref, nrm_ref, o_ref = refs
    else:
        kx_ref, vx_ref, nrm_ref, o_ref = refs
        kl_ref = vl_ref = None
    sub = tq // nsub
    lane = lax.broadcasted_iota(jnp.int32, (sub, LANES), 1)
    left = lane < LANES // 2
    zero = jnp.zeros((sub, LANES), BF16)
    tiles = [(kx_ref, vx_ref, 0, tcx)] + [(kl_ref, vl_ref, j * tk, tk) for j in range(tlat // tk)]
    ones = jnp.ones((max(tk, tcx), LANES), BF16)

    lhs = []
    for u in range(nsub):
        q = q_ref[0, u * sub:(u + 1) * sub, :]
        lhs += [jnp.where(left, q, zero), jnp.where(left, zero, q)]
    state = [None] * (2 * nsub)
    for k_ref, v_ref, r0, n in tiles:
        vx = jnp.concatenate([v_ref[0, r0:r0 + n, :], ones[0:n]], axis=1)
        k = k_ref[0, r0:r0 + n, :]
        for i in range(2 * nsub):
            s = _dot_nt(lhs[i], k)
            m_part = s[:, 0:LANES]
            for t in range(1, n // LANES):
                m_part = jnp.maximum(m_part, s[:, t * LANES:(t + 1) * LANES])
            m_j = jnp.max(m_part, axis=-1, keepdims=True)
            part = _dot(jnp.exp2(s - m_j).astype(BF16), vx)
            if state[i] is None:
                state[i] = (m_j, part)
            else:
                m_run, acc = state[i]
                m_new = jnp.maximum(m_run, m_j)
                state[i] = (m_new, acc * jnp.exp2(m_run - m_new) + part * jnp.exp2(m_j - m_new))
    outs = [acc[:, 0:LANES] / acc[:, LANES:2 * LANES] for _, acc in state]
    lv = lam_ref[...]
    s1 = jnp.sum(lv[0:1] * lv[1:2], axis=-1, keepdims=True)
    s2 = jnp.sum(lv[2:3] * lv[3:4], axis=-1, keepdims=True)
    lam = jnp.exp(s1) - jnp.exp(s2) + lam_init
    for u in range(nsub):
        y = outs[2 * u] - lam * outs[2 * u + 1]
        ms = jnp.mean(y * y, axis=-1, keepdims=True)
        o_ref[0, u * sub:(u + 1) * sub, :] = (y * lax.rsqrt(ms + EPS) * nrm_ref[...] * (1.0 - lam_init)).astype(BF16)


def _diff_attention(q, kx, vx, lamv, d_norm, lam_init, B, tq_total, tcx, lat_kv=None, sub=1024, nsub=2, tk=1024):
    q3 = q.reshape(B, tq_total, BRANCH)
    kx3, vx3 = kx.reshape(B, tcx, BRANCH), vx.reshape(B, tcx, BRANCH)
    sub = min(sub, tq_total)
    nsub = math.gcd(nsub, tq_total // sub)
    tq = sub * nsub
    head_blk = lambda t, f: pl.BlockSpec((1, t, LANES), f)
    whole = lambda b, h, i: (b, 0, h)
    in_specs = [pl.BlockSpec((8, LANES), lambda b, h, i: (0, 0)),
                head_blk(tq, lambda b, h, i: (b, i, h)),
                head_blk(tcx, whole), head_blk(tcx, whole)]
    args = [lamv, q3, kx3, vx3]
    tlat = 0
    if lat_kv is not None:
        tlat = tq_total
        k3, v3 = (a.reshape(B, tlat, BRANCH) for a in lat_kv)
        in_specs += [head_blk(tlat, whole), head_blk(tlat, whole)]
        args += [k3, v3]
    in_specs.append(pl.BlockSpec((1, LANES), lambda b, h, i: (0, h)))
    args.append(d_norm.reshape(1, -1))
    tk = min(tk, max(tlat, LANES))
    out = pl.pallas_call(
        functools.partial(_diff_kernel, tq=tq, nsub=nsub, tk=tk, tcx=tcx, tlat=tlat, lam_init=lam_init),
        grid=(B, DF_HEADS, tq_total // tq),
        in_specs=in_specs,
        out_specs=head_blk(tq, lambda b, h, i: (b, i, h)),
        out_shape=jax.ShapeDtypeStruct((B, tq_total, BRANCH), BF16),
        compiler_params=_params(("arbitrary", "arbitrary", "arbitrary")),
        name="diff_lat" if tlat else "diff_ctx",
    )(*args)
    return out.reshape(B * tq_total, BRANCH)


def _merge_kernel(ya_ref, yb_ref, yc_ref, gate_ref, x_ref, mod_ref, wb_ref, wo_ref, o_ref, *, d):
    merged = None
    for i, y_ref in enumerate((ya_ref, yb_ref, yc_ref)):
        gate = jax.nn.sigmoid(gate_ref[:, i * d:(i + 1) * d].astype(F32))
        term = gate * _dot(y_ref[...], wb_ref[i])
        merged = term if merged is None else merged + term
    out = _dot(merged.astype(BF16), wo_ref[...])
    gt = mod_ref[0][:, 2 * d:3 * d]
    o_ref[...] = x_ref[...] + gt * out


def _merge(ya, yb, yc, gate, x2, mod, wb, wo, *, tm, rows_per_mod):
    N, D = x2.shape
    row = lambda w: pl.BlockSpec((tm, w), lambda i: (i, 0))
    return pl.pallas_call(
        functools.partial(_merge_kernel, d=D),
        grid=(N // tm,),
        in_specs=[row(BRANCH), row(BRANCH), row(BRANCH), row(N_BRANCH * D), row(D),
                  pl.BlockSpec((1, 1, 6 * D), lambda i: ((i * tm) // rows_per_mod, 0, 0)),
                  _resident((N_BRANCH, BRANCH, D), lambda i: (0, 0, 0)),
                  _resident((D, D), lambda i: (0, 0))],
        out_specs=row(D),
        out_shape=jax.ShapeDtypeStruct((N, D), F32),
        compiler_params=_params(("arbitrary",)),
        name="merge",
    )(ya, yb, yc, gate, x2, mod, wb, wo)


_FFN_CHUNK = 256


def _ffn_kernel(x_ref, mod_ref, g_ref, wi_ref, wo_ref, gf_ref, o_ref, *, d, hidden, final_norm):
    x = x_ref[...]
    mod = mod_ref[0]
    h = _modnorm(x, g_ref[...], mod[:, 3 * d:4 * d], mod[:, 4 * d:5 * d]).astype(BF16)
    acc = None
    for c0 in range(0, hidden, _FFN_CHUNK):
        gate = _dot(h, wi_ref[:, c0:c0 + _FFN_CHUNK])
        up = _dot(h, wi_ref[:, hidden + c0:hidden + c0 + _FFN_CHUNK])
        a = (gate * jax.nn.sigmoid(gate) * up).astype(BF16)
        part = _dot(a, wo_ref[c0:c0 + _FFN_CHUNK, :])
        acc = part if acc is None else acc + part
    y = x + mod[:, 5 * d:6 * d] * acc
    if final_norm:
        ms = jnp.mean(y * y, axis=-1, keepdims=True)
        y = y * lax.rsqrt(ms + EPS) * gf_ref[...]
    o_ref[...] = y


def _ffn(x2, mod, g, wi, wo, g_final, *, tm, rows_per_mod, final_norm):
    N, D = x2.shape
    hidden = wo.shape[0]
    return pl.pallas_call(
        functools.partial(_ffn_kernel, d=D, hidden=hidden, final_norm=final_norm),
        grid=(N // tm,),
        in_specs=[pl.BlockSpec((tm, D), lambda i: (i, 0)),
                  pl.BlockSpec((1, 1, 6 * D), lambda i: ((i * tm) // rows_per_mod, 0, 0)),
                  pl.BlockSpec((1, D), lambda i: (0, 0)),
                  _resident((D, 2 * hidden), lambda i: (0, 0)),
                  _resident((hidden, D), lambda i: (0, 0)),
                  pl.BlockSpec((1, D), lambda i: (0, 0))],
        out_specs=pl.BlockSpec((tm, D), lambda i: (i, 0)),
        out_shape=jax.ShapeDtypeStruct((N, D), F32),
        compiler_params=_params(("arbitrary",)),
        name="ffn",
    )(x2, mod, g, wi, wo, g_final)


def _merge_ffn_kernel(ya_ref, yb_ref, yc_ref, gate_ref, x_ref, mod_ref, wb_ref, wo_ref,
                      g_ref, wi_ref, wf_ref, gf_ref, o_ref, mid_ref, *, d, hidden, final_norm):
    _merge_kernel(ya_ref, yb_ref, yc_ref, gate_ref, x_ref, mod_ref, wb_ref, wo_ref, mid_ref, d=d)
    _ffn_kernel(mid_ref, mod_ref, g_ref, wi_ref, wf_ref, gf_ref, o_ref, d=d, hidden=hidden, final_norm=final_norm)


def _merge_ffn(ya, yb, yc, gate, x2, mod, wb, wo, g, wi, wf, g_final, *, tm, rows_per_mod, final_norm):
    N, D = x2.shape
    hidden = wf.shape[0]
    row = lambda w: pl.BlockSpec((tm, w), lambda i: (i, 0))
    vec = pl.BlockSpec((1, D), lambda i: (0, 0))
    return pl.pallas_call(
        functools.partial(_merge_ffn_kernel, d=D, hidden=hidden, final_norm=final_norm),
        grid=(N // tm,),
        in_specs=[row(BRANCH), row(BRANCH), row(BRANCH), row(N_BRANCH * D), row(D),
                  pl.BlockSpec((1, 1, 6 * D), lambda i: ((i * tm) // rows_per_mod, 0, 0)),
                  _resident((N_BRANCH, BRANCH, D), lambda i: (0, 0, 0)),
                  _resident((D, D), lambda i: (0, 0)),
                  vec,
                  _resident((D, 2 * hidden), lambda i: (0, 0)),
                  _resident((hidden, D), lambda i: (0, 0)),
                  vec],
        out_specs=row(D),
        out_shape=jax.ShapeDtypeStruct((N, D), F32),
        scratch_shapes=[pltpu.VMEM((tm, D), F32)],
        compiler_params=_params(("arbitrary",)),
        name="merge_ffn",
    )(ya, yb, yc, gate, x2, mod, wb, wo, g, wi, wf, g_final)


def _row_tile(n, want=512):
    t = want
    while n % t:
        t //= 2
    return t


def kernel(x, c, ctx, c_ctx, w_mod, b_mod, g_mix, g_ffn, w_in, b_gate, conv_w, m_norm, sink,
           lam_q1, lam_k1, lam_q2, lam_k2, d_norm, w_branch, w_out, w_ffn_in, w_ffn_out, g_final):
    B, T, D = x.shape
    Tc = ctx.shape[1]
    depth = w_mod.shape[0]
    assert T % CHUNK == 0 and Tc % CHUNK == 0 and T % GRID_W == 0

    n_rows = -(-(B + 1) // 16) * 16
    cc = jnp.concatenate([c, c_ctx[None, :], jnp.zeros((n_rows - B - 1, D), F32)], axis=0)
    mods = _adaln(cc, w_mod, b_mod)

    rope = _rope_tables(T)
    no_rope = tuple(jnp.zeros((Tc, LANES), F32) for _ in range(3))
    tm = _row_tile(T)
    tmc = _row_tile(Tc)
    names = [s[0] for s in _SEGS]

    xl = x.reshape(B * T, D)
    xc = ctx.reshape(B * Tc, D)
    for l in range(depth):
        need_ctx = l < depth - 1
        lam_init = 0.8 - 0.6 * math.exp(-0.3 * l)
        mod_l = mods[l, :B].reshape(B, 1, 6 * D)
        mod_c = mods[l, B:B + 1].reshape(1, 1, 6 * D)
        w_ext = _extend_w_in(w_in[l])
        g_mix_l = g_mix[l].reshape(1, D)
        g_ffn_l = g_ffn[l].reshape(1, D)
        lamv = jnp.zeros((8, LANES), F32)
        for i, v in enumerate((lam_q1[l], lam_k1[l], lam_q2[l], lam_k2[l])):
            lamv = lamv.at[i, :v.shape[0]].set(v.astype(F32))

        pl_ = dict(zip(names, _inproj(xl, mod_l, g_mix_l, rope, w_ext, tm=tm, rows_per_mod=T, seq=T, use_rope=True)))
        pc_ = dict(zip(names, _inproj(xc, mod_c, g_mix_l, no_rope, w_ext, tm=tmc, rows_per_mod=B * Tc, seq=Tc,
                                      use_rope=False)))

        gates_l = _gates(pl_["mg"], b_gate[l], B, T)
        gates_c = _gates(pc_["mg"], b_gate[l], B, Tc)
        ya, ya_c = _mlstm(pc_, pl_, gates_c, gates_l, conv_w[l], m_norm[l], B, Tc, T, need_ctx)
        yb = _window_attention(pl_["wq"], pc_["wk"], pc_["wv"], sink[l], B, T, Tc, lat_kv=(pl_["wk"], pl_["wv"]))
        yc = _diff_attention(pl_["dq"], pc_["dk"], pc_["dv"], lamv, d_norm[l], lam_init, B, T, Tc,
                             lat_kv=(pl_["dk"], pl_["dv"]))
        wb = w_branch[l].astype(BF16)
        wo = w_out[l].astype(BF16)
        wfi = w_ffn_in[l].astype(BF16)
        wfo = w_ffn_out[l].astype(BF16)
        gfin = g_final.reshape(1, D)
        last = l == depth - 1
        xl = _merge_ffn(ya, yb, yc, pl_["gate"], xl, mod_l, wb, wo, g_ffn_l, wfi, wfo, gfin,
                        tm=tm, rows_per_mod=T, final_norm=last)
        if need_ctx:
            yb_c = _window_attention(pc_["wq"], pc_["wk"], pc_["wv"], sink[l], B, Tc, Tc)
            yc_c = _diff_attention(pc_["dq"], pc_["dk"], pc_["dv"], lamv, d_norm[l], lam_init, B, Tc, Tc)
            xc = _merge(ya_c, yb_c, yc_c, pc_["gate"], xc, mod_c, wb, wo, tm=tmc, rows_per_mod=B * Tc)
            xc = _ffn(xc, mod_c, g_ffn_l, wfi, wfo, gfin, tm=tmc, rows_per_mod=B * Tc, final_norm=False)
    return xl.reshape(B, T, D)
```
